```python
import math
import jax, jax.numpy as jnp
from jax import lax
import numpy as np

D_MODEL = 2048
BATCH = 2
SEQ = 8192
DEPTH = 1

D_RNN = D_MODEL
LRU_BLOCK = 128
N_LRU_BLOCKS = D_RNN // LRU_BLOCK
CONV_W = 4
C_LRU = 8.0
HEAD_DIM = 128
N_HEADS = D_MODEL // (2 * HEAD_DIM)
V_DIM = 2 * HEAD_DIM
QK_W = N_HEADS * 2 * HEAD_DIM
ATTN_W = N_HEADS * V_DIM
Q_BLOCK = 128
ROPE_THETA = 10000.0
SUBLN_EPS = 1e-5
N_GROUPS = 4
EXPERTS_PER_GROUP = 8
N_EXPERTS = N_GROUPS * EXPERTS_PER_GROUP
TOP_K = 2
D_EXPERT = D_MODEL // 2
MOE_BLOCK = 128
D_IN = 2 * D_RNN + 2 * QK_W + ATTN_W + 2 * D_MODEL
NORM_EPS = 1e-6

kernel_name = 'hybrid_rglru_diffattn_hmoe'


def rmsnorm(x, g, eps=NORM_EPS):
    xf = x.astype(jnp.float32)
    y = xf * lax.rsqrt(jnp.mean(xf * xf, axis=-1, keepdims=True) + eps)
    return (y * g.astype(jnp.float32)).astype(x.dtype)


def rope(t, positions):
    inv = 1.0 / (ROPE_THETA ** (jnp.arange(0, HEAD_DIM, 2, dtype=jnp.float32) / HEAD_DIM))
    ang = positions.astype(jnp.float32)[..., None] * inv
    ang = jnp.concatenate([ang, ang], axis=-1)[:, :, None, None, :]
    cos, sin = jnp.cos(ang), jnp.sin(ang)
    tf = t.astype(jnp.float32)
    t1, t2 = jnp.split(tf, 2, axis=-1)
    rot = jnp.concatenate([-t2, t1], axis=-1)
    return (tf * cos + rot * sin).astype(t.dtype)


def causal_conv(u, w, b):
    C = u.shape[-1]
    out = lax.conv_general_dilated(
        u, w[:, None, :].astype(u.dtype), window_strides=(1,),
        padding=[(CONV_W - 1, 0)], dimension_numbers=('NWC', 'WIO', 'NWC'),
        feature_group_count=C)
    return out + b.astype(u.dtype)


def rg_lru(u, positions, w_a, b_a, w_x, b_x, lru_param):
    B, S, C = u.shape
    ub = u.reshape(B, S, N_LRU_BLOCKS, LRU_BLOCK)
    r = jax.nn.sigmoid(jnp.einsum('bsnc,ncd->bsnd', ub, w_a).reshape(B, S, C).astype(jnp.float32)
                       + b_a.astype(jnp.float32))
    i = jax.nn.sigmoid(jnp.einsum('bsnc,ncd->bsnd', ub, w_x).reshape(B, S, C).astype(jnp.float32)
                       + b_x.astype(jnp.float32))
    log_a = -C_LRU * r * jax.nn.softplus(-lru_param.astype(jnp.float32))
    reset = (positions == 0)[..., None]
    a = jnp.where(reset, 0.0, jnp.exp(log_a))
    mult = jnp.where(reset, 1.0, jnp.sqrt(-jnp.expm1(2.0 * log_a)))
    bvals = u.astype(jnp.float32) * i * mult

    def combine(left, right):
        a1, b1 = left
        a2, b2 = right
        return a1 * a2, a2 * b1 + b2

    _, h = lax.associative_scan(combine, (a, bvals), axis=1)
    return h.astype(u.dtype)


def diff_attention(q, k, v, positions, lq1, lk1, lq2, lk2, subln_g, lam_init):
    B, S, _ = q.shape
    q = rope(q.reshape(B, S, N_HEADS, 2, HEAD_DIM), positions)
    k = rope(k.reshape(B, S, N_HEADS, 2, HEAD_DIM), positions)
    v = v.reshape(B, S, N_HEADS, V_DIM)
    lam = (jnp.exp(jnp.sum(lq1.astype(jnp.float32) * lk1.astype(jnp.float32)))
           - jnp.exp(jnp.sum(lq2.astype(jnp.float32) * lk2.astype(jnp.float32))) + lam_init)
    nb = S // Q_BLOCK
    qb = q.reshape(B, nb, Q_BLOCK, N_HEADS, 2, HEAD_DIM).transpose(1, 0, 2, 3, 4, 5)
    kidx = jnp.arange(S)
    scale = HEAD_DIM ** -0.5

    def block(args):
        qi, bi = args
        s = jnp.einsum('bqhmd,bkhmd->bhmqk', qi, k).astype(jnp.float32) * scale
        qidx = bi * Q_BLOCK + jnp.arange(Q_BLOCK)
        mask = kidx[None, :] <= qidx[:, None]
        s = jnp.where(mask, s, -jnp.inf)
        p = jax.nn.softmax(s, axis=-1)
        amap = p[:, :, 0] - lam * p[:, :, 1]
        return jnp.einsum('bhqk,bkhe->bqhe', amap.astype(v.dtype), v)

    o = lax.map(block, (qb, jnp.arange(nb)))
    o = o.transpose(1, 0, 2, 3, 4).reshape(B, S, N_HEADS, V_DIM)
    o = rmsnorm(o, subln_g, eps=SUBLN_EPS) * (1.0 - lam_init)
    return o.reshape(B, S, ATTN_W)


def hier_moe(h, w_grp, w_exp, w_gate, w_up, w_down):
    B, S, D = h.shape
    N = B * S
    hf = h.reshape(N, D)
    gp = jax.nn.softmax((hf @ w_grp).astype(jnp.float32), axis=-1)
    g_idx = jnp.argmax(gp, axis=-1).astype(jnp.int32)
    g_w = jnp.take_along_axis(gp, g_idx[:, None], axis=-1)
    el = (hf @ w_exp).astype(jnp.float32).reshape(N, N_GROUPS, EXPERTS_PER_GROUP)
    el = jnp.take_along_axis(el, g_idx[:, None, None], axis=1)[:, 0]
    ep = jax.nn.softmax(el, axis=-1)
    top_v, top_i = lax.top_k(ep, TOP_K)
    top_v = top_v / jnp.sum(top_v, axis=-1, keepdims=True)
    wts = g_w * top_v
    eid = g_idx[:, None] * EXPERTS_PER_GROUP + top_i.astype(jnp.int32)
    NK = N * TOP_K
    flat_e = eid.reshape(NK)
    flat_t = jnp.repeat(jnp.arange(N, dtype=jnp.int32), TOP_K)
    flat_w = wts.reshape(NK)
    order = jnp.argsort(flat_e)
    se = flat_e[order]
    counts = jnp.bincount(flat_e, length=N_EXPERTS).astype(jnp.int32)
    offsets = jnp.cumsum(counts) - counts
    pcounts = ((counts + MOE_BLOCK - 1) // MOE_BLOCK) * MOE_BLOCK
    pends = jnp.cumsum(pcounts)
    pstarts = pends - pcounts
    dest = pstarts[se] + jnp.arange(NK, dtype=jnp.int32) - offsets[se]
    P = NK + N_EXPERTS * MOE_BLOCK
    nb = P // MOE_BLOCK
    slot_tok = jnp.full((P,), N, dtype=jnp.int32).at[dest].set(flat_t[order])
    slot_w = jnp.zeros((P,), jnp.float32).at[dest].set(flat_w[order])
    blk_e = jnp.clip(jnp.searchsorted(pends, jnp.arange(nb, dtype=jnp.int32) * MOE_BLOCK,
                                      side='right'), 0, N_EXPERTS - 1)
    xpad = jnp.concatenate([hf, jnp.zeros((1, D), hf.dtype)], axis=0)
    xs = xpad[slot_tok].reshape(nb, MOE_BLOCK, D)

    def expert_block(args):
        xb, e = args
        return (jax.nn.silu(xb @ w_gate[e]) * (xb @ w_up[e])) @ w_down[e]

    ys = lax.map(expert_block, (xs, blk_e)).reshape(P, D)
    ys = ys * slot_w[:, None].astype(ys.dtype)
    out = jnp.zeros((N + 1, D), h.dtype).at[slot_tok].add(ys)[:N]
    return out.reshape(B, S, D)


def setup_inputs(seed: int = 0) -> dict:
    key = jax.random.key(seed)
    ks = jax.random.split(key, 24)
    f32 = jnp.float32
    nrm = lambda k, shape, s: jax.random.normal(k, shape, f32) * s
    L = DEPTH
    u = jax.random.uniform(ks[8], (L, D_RNN), f32, 0.9, 0.999)
    a0 = u ** (1.0 / C_LRU)
    lru_param = jnp.log(a0) - jnp.log1p(-a0)
    return {
        'x': nrm(ks[0], (BATCH, SEQ, D_MODEL), 1.0),
        'positions': jnp.broadcast_to(jnp.arange(SEQ, dtype=jnp.int32), (BATCH, SEQ)),
        'g_mix': 1.0 + nrm(ks[1], (L, D_MODEL), 0.02),
        'w_in': nrm(ks[2], (L, D_MODEL, D_IN), D_MODEL ** -0.5),
        'conv_w': nrm(ks[3], (L, CONV_W, D_RNN), CONV_W ** -0.5),
        'conv_b': nrm(ks[4], (L, D_RNN), 0.01),
        'w_rg_a': nrm(ks[5], (L, N_LRU_BLOCKS, LRU_BLOCK, LRU_BLOCK), LRU_BLOCK ** -0.5),
        'b_rg_a': nrm(ks[6], (L, D_RNN), 0.01),
        'w_rg_x': nrm(ks[7], (L, N_LRU_BLOCKS, LRU_BLOCK, LRU_BLOCK), LRU_BLOCK ** -0.5),
        'b_rg_x': nrm(ks[9], (L, D_RNN), 0.01),
        'lru_param': lru_param,
        'lambda_q1': nrm(ks[10], (L, HEAD_DIM), 0.1),
        'lambda_k1': nrm(ks[11], (L, HEAD_DIM), 0.1),
        'lambda_q2': nrm(ks[12], (L, HEAD_DIM), 0.1),
        'lambda_k2': nrm(ks[13], (L, HEAD_DIM), 0.1),
        'subln_g': 1.0 + nrm(ks[14], (L, V_DIM), 0.02),
        'w_br_rnn': nrm(ks[15], (L, D_RNN, D_MODEL), D_RNN ** -0.5),
        'w_br_attn': nrm(ks[16], (L, ATTN_W, D_MODEL), ATTN_W ** -0.5),
        'w_out': nrm(ks[17], (L, D_MODEL, D_MODEL), D_MODEL ** -0.5),
        'g_ffn': 1.0 + nrm(ks[18], (L, D_MODEL), 0.02),
        'w_grp_router': nrm(ks[19], (L, D_MODEL, N_GROUPS), D_MODEL ** -0.5),
        'w_exp_router': nrm(ks[20], (L, D_MODEL, N_EXPERTS), D_MODEL ** -0.5),
        'w_gate': nrm(ks[21], (L, N_EXPERTS, D_MODEL, D_EXPERT), D_MODEL ** -0.5),
        'w_up': nrm(ks[22], (L, N_EXPERTS, D_MODEL, D_EXPERT), D_MODEL ** -0.5),
        'w_down': nrm(ks[23], (L, N_EXPERTS, D_EXPERT, D_MODEL), D_EXPERT ** -0.5),
        'g_final': 1.0 + nrm(jax.random.fold_in(key, 99), (D_MODEL,), 0.02),
    }


def reference(x, positions, g_mix, w_in, conv_w, conv_b, w_rg_a, b_rg_a, w_rg_x, b_rg_x,
              lru_param, lambda_q1, lambda_k1, lambda_q2, lambda_k2, subln_g, w_br_rnn,
              w_br_attn, w_out, g_ffn, w_grp_router, w_exp_router, w_gate, w_up, w_down,
              g_final):
    splits = [D_RNN, 2 * D_RNN, 2 * D_RNN + QK_W, 2 * D_RNN + 2 * QK_W,
              2 * D_RNN + 2 * QK_W + ATTN_W, 2 * D_RNN + 2 * QK_W + ATTN_W + D_MODEL]
    for l in range(DEPTH):
        lam_init = 0.8 - 0.6 * math.exp(-0.3 * l)
        h = rmsnorm(x, g_mix[l])
        proj = h @ w_in[l]
        u, gb, q, k, v, gr, ga = jnp.split(proj, splits, axis=-1)
        u = causal_conv(u, conv_w[l], conv_b[l])
        hr = rg_lru(u, positions, w_rg_a[l], b_rg_a[l], w_rg_x[l], b_rg_x[l], lru_param[l])
        y_rnn = hr * jax.nn.gelu(gb, approximate=True)
        y_attn = diff_attention(q, k, v, positions, lambda_q1[l], lambda_k1[l],
                                lambda_q2[l], lambda_k2[l], subln_g[l], lam_init)
        merged = (jax.nn.sigmoid(gr) * (y_rnn @ w_br_rnn[l])
                  + jax.nn.sigmoid(ga) * (y_attn @ w_br_attn[l]))
        x = x + merged @ w_out[l]
        x = x + hier_moe(rmsnorm(x, g_ffn[l]), w_grp_router[l], w_exp_router[l],
                         w_gate[l], w_up[l], w_down[l])
    return rmsnorm(x, g_final)
```

```python
import functools
import math

import numpy as np
import jax
import jax.numpy as jnp
from jax import lax
from jax.experimental import pallas as pl
from jax.experimental.pallas import tpu as pltpu

D_MODEL = 2048
D_RNN = D_MODEL
LRU_BLOCK = 128
N_LRU_BLOCKS = D_RNN // LRU_BLOCK
CONV_W = 4
C_LRU = 8.0
HEAD_DIM = 128
N_HEADS = D_MODEL // (2 * HEAD_DIM)
V_DIM = 2 * HEAD_DIM
QK_W = N_HEADS * 2 * HEAD_DIM
ATTN_W = N_HEADS * V_DIM
ROPE_THETA = 10000.0
SUBLN_EPS = 1e-5
N_GROUPS = 4
EXPERTS_PER_GROUP = 8
N_EXPERTS = N_GROUPS * EXPERTS_PER_GROUP
TOP_K = 2
D_EXPERT = D_MODEL // 2
NORM_EPS = 1e-6
LAM_INIT = 0.8 - 0.6 * math.exp(-0.3 * 0)

SLOT_BLOCK = 128
ROUTER_ROWS = 8 + N_EXPERTS
MASK_VALUE = -1e30
LOG2E = 1.4426950408889634
VMEM_LIMIT = 56 * 1024 * 1024

_F32 = jnp.float32
_BF16 = jnp.bfloat16


def _cparams(n_axes):
    return pltpu.CompilerParams(dimension_semantics=("arbitrary",) * n_axes,
                                vmem_limit_bytes=VMEM_LIMIT)


def _rmsnorm_kernel(x_ref, g_ref, o_ref):
    x = x_ref[...]
    y = x * lax.rsqrt(jnp.mean(x * x, axis=-1, keepdims=True) + NORM_EPS)
    o_ref[...] = (y * g_ref[...]).astype(o_ref.dtype)


def _rmsnorm(x, g, out_dtype):
    n, d = x.shape
    tm = min(512, n)
    return pl.pallas_call(
        _rmsnorm_kernel,
        grid=(n // tm,),
        in_specs=[pl.BlockSpec((tm, d), lambda i: (i, 0)),
                  pl.BlockSpec((1, d), lambda i: (0, 0))],
        out_specs=pl.BlockSpec((tm, d), lambda i: (i, 0)),
        out_shape=jax.ShapeDtypeStruct((n, d), out_dtype),
        compiler_params=_cparams(1),
        name="rmsnorm",
    )(x, g)


def _rope_table_kernel(pos_ref, inv_ref, sign_ref, cos_ref, sin_ref):
    ang = pos_ref[...].astype(_F32) * inv_ref[...]
    cos_ref[...] = jnp.cos(ang)
    sin_ref[...] = jnp.sin(ang) * sign_ref[...]


def _rope_tables(pos_col):
    n = pos_col.shape[0]
    tm = min(1024, n)
    inv = 1.0 / (ROPE_THETA ** (np.arange(0, HEAD_DIM, 2, dtype=np.float32) / np.float32(HEAD_DIM)))
    inv = np.concatenate([inv, inv]).astype(np.float32)[None, :]
    sign = np.concatenate([-np.ones(HEAD_DIM // 2), np.ones(HEAD_DIM // 2)]).astype(np.float32)[None, :]
    return pl.pallas_call(
        _rope_table_kernel,
        grid=(n // tm,),
        in_specs=[pl.BlockSpec((tm, 1), lambda i: (i, 0)),
                  pl.BlockSpec((1, HEAD_DIM), lambda i: (0, 0)),
                  pl.BlockSpec((1, HEAD_DIM), lambda i: (0, 0))],
        out_specs=[pl.BlockSpec((tm, HEAD_DIM), lambda i: (i, 0))] * 2,
        out_shape=[jax.ShapeDtypeStruct((n, HEAD_DIM), _F32)] * 2,
        compiler_params=_cparams(1),
        name="rope_tables",
    )(pos_col, jnp.asarray(inv), jnp.asarray(sign))


def _proj_plain_kernel(h_ref, w_ref, o_ref):
    o_ref[...] = jnp.dot(h_ref[...], w_ref[...], preferred_element_type=_F32).astype(o_ref.dtype)


def _proj_sigmoid_kernel(h_ref, w_ref, o_ref):
    acc = jnp.dot(h_ref[...], w_ref[...], preferred_element_type=_F32)
    o_ref[...] = jax.nn.sigmoid(acc).astype(o_ref.dtype)


def _proj_rope_kernel(h_ref, w_ref, cos_ref, sin_ref, o_ref, *, q_tiles, q_scale):
    acc = jnp.dot(h_ref[...], w_ref[...], preferred_element_type=_F32)
    cos = cos_ref[...]
    sin = sin_ref[...]
    scale = jnp.where(pl.program_id(0) < q_tiles, q_scale, 1.0).astype(_F32)
    for g in range(acc.shape[1] // HEAD_DIM):
        t = acc[:, g * HEAD_DIM:(g + 1) * HEAD_DIM]
        r = (t * cos + pltpu.roll(t, HEAD_DIM // 2, axis=1) * sin) * scale
        o_ref[:, g * HEAD_DIM:(g + 1) * HEAD_DIM] = r.astype(o_ref.dtype)


def _projection(kernel_fn, h, w, col_off, n_cols, out_dtype, extra=(), name="proj"):
    n, k = h.shape
    tm = min(1024, n)
    tn = 1024
    off = col_off // tn
    extra_specs = [pl.BlockSpec((tm, e.shape[1]), lambda j, i: (i, 0)) for e in extra]
    return pl.pallas_call(
        kernel_fn,
        grid=(n_cols // tn, n // tm),
        in_specs=[pl.BlockSpec((tm, k), lambda j, i: (i, 0)),
                  pl.BlockSpec((k, tn), lambda j, i: (0, j + off))] + extra_specs,
        out_specs=pl.BlockSpec((tm, tn), lambda j, i: (i, j)),
        out_shape=jax.ShapeDtypeStruct((n, n_cols), out_dtype),
        compiler_params=_cparams(2),
        name=name,
    )(h, w, *extra)


def _lru_kernel(pos_ref, u_ref, gb_ref, cw_ref, cb_ref, wrg_ref, ba_ref, bx_ref, lp_ref, o_ref,
                ubuf, hcar, a_s, b_s, h_s):
    t_rows, cb = u_ref.shape
    n_grp = t_rows // 8

    @pl.when(pl.program_id(2) == 0)
    def _():
        ubuf[0:8, :] = jnp.zeros((8, cb), _F32)
        hcar[...] = jnp.zeros((8, cb), _F32)

    ubuf[8:8 + t_rows, :] = u_ref[...]
    cw = cw_ref[...]
    uc = cb_ref[...] + cw[0:1] * ubuf[5:5 + t_rows, :]
    uc = uc + cw[1:2] * ubuf[6:6 + t_rows, :]
    uc = uc + cw[2:3] * ubuf[7:7 + t_rows, :]
    uc = uc + cw[3:4] * ubuf[8:8 + t_rows, :]
    ubuf[0:8, :] = ubuf[t_rows:t_rows + 8, :]

    ucb = uc.astype(_BF16)
    r_parts, i_parts = [], []
    for j in range(cb // LRU_BLOCK):
        g = jnp.dot(ucb[:, j * LRU_BLOCK:(j + 1) * LRU_BLOCK], wrg_ref[j], preferred_element_type=_F32)
        r_parts.append(g[:, :LRU_BLOCK])
        i_parts.append(g[:, LRU_BLOCK:])
    r = jax.nn.sigmoid(jnp.concatenate(r_parts, axis=1) + ba_ref[...])
    gi = jax.nn.sigmoid(jnp.concatenate(i_parts, axis=1) + bx_ref[...])

    log_a = (-C_LRU) * r * jax.nn.softplus(-lp_ref[...])
    a_raw = jnp.exp(log_a)
    mult = jnp.sqrt(1.0 - a_raw * a_raw)
    reset = pos_ref[...] == 0
    a = jnp.where(reset, 0.0, a_raw)
    mult = jnp.where(reset, 1.0, mult)
    bv = uc * gi * mult

    a3 = a.reshape(n_grp, 8, cb)
    b3 = bv.reshape(n_grp, 8, cb)
    row = lax.broadcasted_iota(jnp.int32, (n_grp, 8, cb), 1)
    for d in (1, 2, 4):
        a_sh = pltpu.roll(a3, d, axis=1)
        b_sh = pltpu.roll(b3, d, axis=1)
        take = row >= d
        b3 = jnp.where(take, a3 * b_sh + b3, b3)
        a3 = jnp.where(take, a3 * a_sh, a3)
    a_s[...] = a3
    b_s[...] = b3

    def carry_step(g, carry):
        h = b_s[g] + a_s[g] * carry
        h_s[g] = h
        return jnp.broadcast_to(h[7:8, :], (8, cb))

    hcar[...] = lax.fori_loop(0, n_grp, carry_step, hcar[...], unroll=8)

    hr = h_s[...].reshape(t_rows, cb)
    o_ref[...] = (hr * jax.nn.gelu(gb_ref[...], approximate=True)).astype(o_ref.dtype)


def _lru(pos_col, ug, conv_w, conv_b, w_rg, b_a, b_x, lru_param, batch, seq):
    cb = 512
    t_rows = min(512, seq)
    n_cb = D_RNN // cb
    n_t = seq // t_rows
    row_map = lambda b, c, s: (b * n_t + s, 0)
    return pl.pallas_call(
        _lru_kernel,
        grid=(batch, n_cb, n_t),
        in_specs=[pl.BlockSpec((t_rows, 1), row_map),
                  pl.BlockSpec((t_rows, cb), lambda b, c, s: (b * n_t + s, c)),
                  pl.BlockSpec((t_rows, cb), lambda b, c, s: (b * n_t + s, n_cb + c)),
                  pl.BlockSpec((CONV_W, cb), lambda b, c, s: (0, c)),
                  pl.BlockSpec((1, cb), lambda b, c, s: (0, c)),
                  pl.BlockSpec((cb // LRU_BLOCK, LRU_BLOCK, 2 * LRU_BLOCK), lambda b, c, s: (c, 0, 0)),
                  pl.BlockSpec((1, cb), lambda b, c, s: (0, c)),
                  pl.BlockSpec((1, cb), lambda b, c, s: (0, c)),
                  pl.BlockSpec((1, cb), lambda b, c, s: (0, c))],
        out_specs=pl.BlockSpec((t_rows, cb), lambda b, c, s: (b * n_t + s, c)),
        out_shape=jax.ShapeDtypeStruct((batch * seq, D_RNN), _BF16),
        scratch_shapes=[pltpu.VMEM((t_rows + 8, cb), _F32),
                        pltpu.VMEM((8, cb), _F32),
                        pltpu.VMEM((t_rows // 8, 8, cb), _F32),
                        pltpu.VMEM((t_rows // 8, 8, cb), _F32),
                        pltpu.VMEM((t_rows // 8, 8, cb), _F32)],
        compiler_params=_cparams(3),
        name="conv_rglru",
    )(pos_col, ug, ug, conv_w, conv_b, w_rg, b_a, b_x, lru_param)


def _attn_kernel(q_ref, k_ref, v_ref, lq1_ref, lk1_ref, lq2_ref, lk2_ref, g_ref, o_ref,
                 m_s, l_s, acc_s, *, tk):
    tq = q_ref.shape[0]
    qi = pl.program_id(2)
    n_sub = tq // tk

    m_s[...] = jnp.full(m_s.shape, MASK_VALUE, _F32)
    l_s[...] = jnp.zeros(l_s.shape, _F32)
    acc_s[...] = jnp.zeros(acc_s.shape, _F32)

    def step(mp, r0, col0, masked):
        nr = tq - r0
        lanes = slice(mp * HEAD_DIM, (mp + 1) * HEAD_DIM)
        q = q_ref[r0:tq, lanes]
        k = k_ref[pl.ds(col0, tk), lanes]
        s = lax.dot_general(q, k, (((1,), (1,)), ((), ())), preferred_element_type=_F32)
        if masked:
            rr = lax.broadcasted_iota(jnp.int32, (nr, tk), 0)
            cc = lax.broadcasted_iota(jnp.int32, (nr, tk), 1)
            s = jnp.where(cc <= rr, s, MASK_VALUE)
        m_prev = m_s[mp, r0:tq, :]
        m_new = jnp.maximum(m_prev, jnp.max(s, axis=1, keepdims=True))
        alpha = jnp.exp2(m_prev - m_new)
        p = jnp.exp2(s - m_new)
        l_s[mp, r0:tq, :] = alpha * l_s[mp, r0:tq, :] + jnp.sum(p, axis=1, keepdims=True)
        pv = jnp.dot(p.astype(_BF16), v_ref[pl.ds(col0, tk), :], preferred_element_type=_F32)
        acc_s[mp, r0:tq, :] = alpha * acc_s[mp, r0:tq, :] + pv
        m_s[mp, r0:tq, :] = m_new

    def full_block(j, carry):
        col0 = pl.multiple_of(j * tk, tk)
        step(0, 0, col0, False)
        step(1, 0, col0, False)
        return carry

    lax.fori_loop(0, qi * n_sub, full_block, 0)
    for d in range(n_sub):
        col0 = pl.multiple_of(qi * tq + d * tk, tk)
        step(0, d * tk, col0, True)
        step(1, d * tk, col0, True)

    lam = (jnp.exp(jnp.sum(lq1_ref[...] * lk1_ref[...], axis=1, keepdims=True))
           - jnp.exp(jnp.sum(lq2_ref[...] * lk2_ref[...], axis=1, keepdims=True)) + LAM_INIT)
    o = acc_s[0] / l_s[0] - lam * (acc_s[1] / l_s[1])
    y = o * lax.rsqrt(jnp.mean(o * o, axis=-1, keepdims=True) + SUBLN_EPS)
    o_ref[...] = ((y * g_ref[...]) * (1.0 - LAM_INIT)).astype(o_ref.dtype)


def _attention(qk, v, lq1, lk1, lq2, lk2, subln_g, batch, seq):
    tq = min(1024, seq)
    tk = min(512, seq)
    nq = seq // tq
    vec = pl.BlockSpec((1, HEAD_DIM), lambda b, h, i: (0, 0))
    return pl.pallas_call(
        functools.partial(_attn_kernel, tk=tk),
        grid=(batch, N_HEADS, nq),
        in_specs=[pl.BlockSpec((tq, V_DIM), lambda b, h, i: (b * nq + i, h)),
                  pl.BlockSpec((seq, V_DIM), lambda b, h, i: (b, N_HEADS + h)),
                  pl.BlockSpec((seq, V_DIM), lambda b, h, i: (b, h)),
                  vec, vec, vec, vec,
                  pl.BlockSpec((1, V_DIM), lambda b, h, i: (0, 0))],
        out_specs=pl.BlockSpec((tq, V_DIM), lambda b, h, i: (b * nq + i, h)),
        out_shape=jax.ShapeDtypeStruct((batch * seq, ATTN_W), _BF16),
        scratch_shapes=[pltpu.VMEM((2, tq, 1), _F32),
                        pltpu.VMEM((2, tq, 1), _F32),
                        pltpu.VMEM((2, tq, V_DIM), _F32)],
        compiler_params=_cparams(3),
        name="diff_attention",
    )(qk, qk, v, lq1, lk1, lq2, lk2, subln_g)


def _merge_kernel(yr_ref, ya_ref, wr_ref, wa_ref, gr_ref, ga_ref, o_ref):
    pr = jnp.dot(yr_ref[...], wr_ref[...], preferred_element_type=_F32)
    pa = jnp.dot(ya_ref[...], wa_ref[...], preferred_element_type=_F32)
    o_ref[...] = (gr_ref[...].astype(_F32) * pr + ga_ref[...].astype(_F32) * pa).astype(o_ref.dtype)


def _merge(y_rnn, y_attn, w_r, w_a, gates):
    n, k = y_rnn.shape
    tm = min(512, n)
    tn = 1024
    n_tn = D_MODEL // tn
    return pl.pallas_call(
        _merge_kernel,
        grid=(n_tn, n // tm),
        in_specs=[pl.BlockSpec((tm, k), lambda j, i: (i, 0)),
                  pl.BlockSpec((tm, k), lambda j, i: (i, 0)),
                  pl.BlockSpec((k, tn), lambda j, i: (0, j)),
                  pl.BlockSpec((k, tn), lambda j, i: (0, j)),
                  pl.BlockSpec((tm, tn), lambda j, i: (i, j)),
                  pl.BlockSpec((tm, tn), lambda j, i: (i, n_tn + j))],
        out_specs=pl.BlockSpec((tm, tn), lambda j, i: (i, j)),
        out_shape=jax.ShapeDtypeStruct((n, D_MODEL), _BF16),
        compiler_params=_cparams(2),
        name="branch_merge",
    )(y_rnn, y_attn, w_r, w_a, gates, gates)


def _outproj_kernel(m_ref, w_ref, x_ref, g_ref, wr_ref, x2_ref, h2_ref, lt_ref):
    x2 = x_ref[...] + jnp.dot(m_ref[...], w_ref[...], preferred_element_type=_F32)
    x2_ref[...] = x2
    h2 = (x2 * lax.rsqrt(jnp.mean(x2 * x2, axis=-1, keepdims=True) + NORM_EPS)) * g_ref[...]
    h2_ref[...] = h2
    lt_ref[...] = lax.dot_general(wr_ref[...], h2, (((1,), (1,)), ((), ())),
                                  precision=lax.Precision.HIGHEST, preferred_element_type=_F32)


def _outproj(merged, w_out, x, g_ffn, w_router_t):
    n, d = x.shape
    tm = min(256, n)
    return pl.pallas_call(
        _outproj_kernel,
        grid=(n // tm,),
        in_specs=[pl.BlockSpec((tm, d), lambda i: (i, 0)),
                  pl.BlockSpec((d, d), lambda i: (0, 0)),
                  pl.BlockSpec((tm, d), lambda i: (i, 0)),
                  pl.BlockSpec((1, d), lambda i: (0, 0)),
                  pl.BlockSpec((ROUTER_ROWS, d), lambda i: (0, 0))],
        out_specs=[pl.BlockSpec((tm, d), lambda i: (i, 0)),
                   pl.BlockSpec((tm, d), lambda i: (i, 0)),
                   pl.BlockSpec((ROUTER_ROWS, tm), lambda i: (0, i))],
        out_shape=[jax.ShapeDtypeStruct((n, d), _F32),
                   jax.ShapeDtypeStruct((n, d), _F32),
                   jax.ShapeDtypeStruct((ROUTER_ROWS, n), _F32)],
        compiler_params=_cparams(1),
        name="outproj_norm_router",
    )(merged, w_out, x, g_ffn, w_router_t)


def _first_index_of_max(vals, n_rows):
    vmax = jnp.max(vals, axis=0, keepdims=True)
    rows = lax.broadcasted_iota(jnp.int32, vals.shape, 0).astype(_F32)
    idx = jnp.min(jnp.where(vals == vmax, rows, float(n_rows)), axis=0, keepdims=True)
    return vmax, idx


def _route_kernel(lt_ref, tri_ref, eid_ref, wts_ref, rank_ref, cnt_ref, carry):
    tl = lt_ref.shape[1]

    @pl.when(pl.program_id(0) == 0)
    def _():
        carry[...] = jnp.zeros(carry.shape, _F32)

    lt = lt_ref[...]
    g = lt[0:N_GROUPS]
    ge = jnp.exp(g - jnp.max(g, axis=0, keepdims=True))
    gp = ge / jnp.sum(ge, axis=0, keepdims=True)
    g_w, g_idx = _first_index_of_max(gp, N_GROUPS)
    el = jnp.zeros((EXPERTS_PER_GROUP, tl), _F32)
    for gg in range(N_GROUPS):
        el = jnp.where(g_idx == float(gg), lt[8 + gg * EXPERTS_PER_GROUP:8 + (gg + 1) * EXPERTS_PER_GROUP], el)
    ee = jnp.exp(el - jnp.max(el, axis=0, keepdims=True))
    ep = ee / jnp.sum(ee, axis=0, keepdims=True)
    v1, i1 = _first_index_of_max(ep, EXPERTS_PER_GROUP)
    rows8 = lax.broadcasted_iota(jnp.int32, ep.shape, 0).astype(_F32)
    v2, i2 = _first_index_of_max(jnp.where(rows8 == i1, -1.0, ep), EXPERTS_PER_GROUP)
    den = v1 + v2
    e1 = g_idx * float(EXPERTS_PER_GROUP) + i1
    e2 = g_idx * float(EXPERTS_PER_GROUP) + i2

    rows_e = lax.broadcasted_iota(jnp.int32, (N_EXPERTS, tl), 0).astype(_F32)
    ranks = []
    for e_sel in (e1, e2):
        onehot = rows_e == e_sel
        oh = jnp.where(onehot, 1.0, 0.0)
        before = jnp.dot(oh.astype(_BF16), tri_ref[...], preferred_element_type=_F32) + carry[...]
        ranks.append(jnp.sum(jnp.where(onehot, before, 0.0), axis=0, keepdims=True))
        carry[...] = carry[...] + jnp.sum(oh, axis=1, keepdims=True)

    zeros6 = jnp.zeros((6, tl), _F32)
    eid_ref[...] = jnp.concatenate([e1, e2, zeros6], axis=0).astype(jnp.int32)
    wts_ref[...] = jnp.concatenate([g_w * (v1 / den), g_w * (v2 / den), zeros6], axis=0)
    rank_ref[...] = jnp.concatenate(ranks + [zeros6], axis=0).astype(jnp.int32)
    cnt_ref[...] = jnp.broadcast_to(carry[...], cnt_ref.shape).astype(jnp.int32)


def _route(logits_t):
    n = logits_t.shape[1]
    tl = min(512, n)
    tri = jnp.asarray(np.triu(np.ones((tl, tl), np.float32), k=1), _BF16)
    row_block = pl.BlockSpec((8, tl), lambda i: (0, i))
    return pl.pallas_call(
        _route_kernel,
        grid=(n // tl,),
        in_specs=[pl.BlockSpec((ROUTER_ROWS, tl), lambda i: (0, i)),
                  pl.BlockSpec((tl, tl), lambda i: (0, 0))],
        out_specs=[row_block, row_block, row_block,
                   pl.BlockSpec((N_EXPERTS, 128), lambda i: (0, 0))],
        out_shape=[jax.ShapeDtypeStruct((8, n), jnp.int32),
                   jax.ShapeDtypeStruct((8, n), _F32),
                   jax.ShapeDtypeStruct((8, n), jnp.int32),
                   jax.ShapeDtypeStruct((N_EXPERTS, 128), jnp.int32)],
        scratch_shapes=[pltpu.VMEM((N_EXPERTS, 1), _F32)],
        compiler_params=_cparams(1),
        name="route_topk_rank",
    )(logits_t, tri)


def _row_copy(src_hbm, row, dst_vmem, r, sem):
    return pltpu.make_async_copy(src_hbm.at[pl.ds(row, 1)], dst_vmem.at[pl.ds(r, 1)], sem)


def _gather_kernel(idx_ref, h_hbm, o_ref, rows, sem):
    tb = rows.shape[0]
    base = pl.program_id(0) * tb

    def issue(r, c):
        _row_copy(h_hbm, idx_ref[base + r], rows, r, sem).start()
        return c

    lax.fori_loop(0, tb, issue, 0, unroll=8)

    def drain(r, c):
        _row_copy(h_hbm, 0, rows, r, sem).wait()
        return c

    lax.fori_loop(0, tb, drain, 0, unroll=8)
    o_ref[...] = rows[...].astype(o_ref.dtype)


def _gather_rows(slot_tok, h2, n_slots):
    d = h2.shape[1]
    tb = 512
    return pl.pallas_call(
        _gather_kernel,
        grid_spec=pltpu.PrefetchScalarGridSpec(
            num_scalar_prefetch=1,
            grid=(n_slots // tb,),
            in_specs=[pl.BlockSpec(memory_space=pl.ANY)],
            out_specs=pl.BlockSpec((tb, d), lambda i, idx: (i, 0)),
            scratch_shapes=[pltpu.VMEM((tb, d), _F32), pltpu.SemaphoreType.DMA]),
        out_shape=jax.ShapeDtypeStruct((n_slots, d), _BF16),
        compiler_params=_cparams(1),
        name="slot_gather",
    )(slot_tok, h2)


def _expert_kernel(be_ref, x_ref, wg_ref, wu_ref, wd_ref, o_ref):
    x = x_ref[...]
    g = jnp.dot(x, wg_ref[...], preferred_element_type=_F32)
    u = jnp.dot(x, wu_ref[...], preferred_element_type=_F32)
    mid = (jax.nn.silu(g) * u).astype(_BF16)
    o_ref[...] = jnp.dot(mid, wd_ref[...], preferred_element_type=_F32)


def _experts(blk_e, xs, w_gate, w_up, w_down):
    p, d = xs.shape
    return pl.pallas_call(
        _expert_kernel,
        grid_spec=pltpu.PrefetchScalarGridSpec(
            num_scalar_prefetch=1,
            grid=(p // SLOT_BLOCK,),
            in_specs=[pl.BlockSpec((SLOT_BLOCK, d), lambda i, be: (i, 0)),
                      pl.BlockSpec((None, d, D_EXPERT), lambda i, be: (be[i], 0, 0)),
                      pl.BlockSpec((None, d, D_EXPERT), lambda i, be: (be[i], 0, 0)),
                      pl.BlockSpec((None, D_EXPERT, d), lambda i, be: (be[i], 0, 0))],
            out_specs=pl.BlockSpec((SLOT_BLOCK, d), lambda i, be: (i, 0))),
        out_shape=jax.ShapeDtypeStruct((p, d), _F32),
        compiler_params=_cparams(1),
        name="expert_mlp",
    )(blk_e, xs, w_gate, w_up, w_down)


def _combine_kernel(dest_ref, x2_ref, w0_ref, w1_ref, g_ref, ys_hbm, o_ref, rows, sem):
    tm = x2_ref.shape[0]
    n = pl.num_programs(0) * tm
    base = pl.program_id(0) * tm

    def issue(r, c):
        for k in range(TOP_K):
            _row_copy(ys_hbm, dest_ref[k * n + base + r], rows.at[k], r, sem).start()
        return c

    lax.fori_loop(0, tm, issue, 0, unroll=8)

    def drain(r, c):
        for k in range(TOP_K):
            _row_copy(ys_hbm, 0, rows.at[k], r, sem).wait()
        return c

    lax.fori_loop(0, tm, drain, 0, unroll=8)
    x3 = x2_ref[...] + (w0_ref[...] * rows[0] + w1_ref[...] * rows[1])
    y = x3 * lax.rsqrt(jnp.mean(x3 * x3, axis=-1, keepdims=True) + NORM_EPS)
    o_ref[...] = y * g_ref[...]


def _combine(dest, x2, w0, w1, g_final, ys):
    n, d = x2.shape
    tm = min(256, n)
    return pl.pallas_call(
        _combine_kernel,
        grid_spec=pltpu.PrefetchScalarGridSpec(
            num_scalar_prefetch=1,
            grid=(n // tm,),
            in_specs=[pl.BlockSpec((tm, d), lambda i, dst: (i, 0)),
                      pl.BlockSpec((tm, 1), lambda i, dst: (i, 0)),
                      pl.BlockSpec((tm, 1), lambda i, dst: (i, 0)),
                      pl.BlockSpec((1, d), lambda i, dst: (0, 0)),
                      pl.BlockSpec(memory_space=pl.ANY)],
            out_specs=pl.BlockSpec((tm, d), lambda i, dst: (i, 0)),
            scratch_shapes=[pltpu.VMEM((TOP_K, tm, d), _F32), pltpu.SemaphoreType.DMA]),
        out_shape=jax.ShapeDtypeStruct((n, d), _F32),
        compiler_params=_cparams(1),
        name="combine_final_norm",
    )(dest, x2, w0, w1, g_final, ys)


def kernel(x, positions, g_mix, w_in, conv_w, conv_b, w_rg_a, b_rg_a, w_rg_x, b_rg_x, lru_param, lambda_q1, lambda_k1, lambda_q2, lambda_k2, subln_g, w_br_rnn, w_br_attn, w_out, g_ffn, w_grp_router, w_exp_router, w_gate, w_up, w_down, g_final):
    batch, seq, d = x.shape
    n = batch * seq
    xf = x.reshape(n, d)
    pos_col = positions.reshape(n, 1)
    row = lambda v: v.reshape(1, -1)

    w_in_b = w_in[0].astype(_BF16)
    w_rg = jnp.concatenate([w_rg_a[0], w_rg_x[0]], axis=-1).astype(_BF16)
    w_r_b = w_br_rnn[0].astype(_BF16)
    w_a_b = w_br_attn[0].astype(_BF16)
    w_out_b = w_out[0].astype(_BF16)
    w_gate_b = w_gate[0].astype(_BF16)
    w_up_b = w_up[0].astype(_BF16)
    w_down_b = w_down[0].astype(_BF16)
    w_router_t = jnp.concatenate(
        [w_grp_router[0].T, jnp.zeros((8 - N_GROUPS, d), _F32), w_exp_router[0].T], axis=0)

    h = _rmsnorm(xf, row(g_mix[0]), _BF16)
    cos_t, sin_t = _rope_tables(pos_col)

    c_qk = 2 * D_RNN
    c_v = c_qk + 2 * QK_W
    c_g = c_v + ATTN_W
    ug = _projection(_proj_plain_kernel, h, w_in_b, 0, 2 * D_RNN, _F32, name="proj_rnn")
    q_scale = (HEAD_DIM ** -0.5) * LOG2E
    qk = _projection(functools.partial(_proj_rope_kernel, q_tiles=QK_W // 1024, q_scale=q_scale),
                     h, w_in_b, c_qk, 2 * QK_W, _BF16, extra=(cos_t, sin_t), name="proj_qk_rope")
    v = _projection(_proj_plain_kernel, h, w_in_b, c_v, ATTN_W, _BF16, name="proj_v")
    gates = _projection(_proj_sigmoid_kernel, h, w_in_b, c_g, 2 * D_MODEL, _BF16, name="proj_gates")

    y_rnn = _lru(pos_col, ug, conv_w[0], row(conv_b[0]), w_rg, row(b_rg_a[0]), row(b_rg_x[0]),
                 row(lru_param[0]), batch, seq)
    y_attn = _attention(qk, v, row(lambda_q1[0]), row(lambda_k1[0]), row(lambda_q2[0]), row(lambda_k2[0]),
                        row(subln_g[0]), batch, seq)
    merged = _merge(y_rnn, y_attn, w_r_b, w_a_b, gates)
    x2, h2, logits_t = _outproj(merged, w_out_b, xf, row(g_ffn[0]), w_router_t)

    eid8, wts8, rank8, cnt = _route(logits_t)
    eid, rank, counts = eid8[:TOP_K], rank8[:TOP_K], cnt[:, 0]

    pcounts = ((counts + SLOT_BLOCK - 1) // SLOT_BLOCK) * SLOT_BLOCK
    pends = jnp.cumsum(pcounts)
    pstarts = pends - pcounts
    dest = pstarts[eid] + rank
    n_slots = n * TOP_K + N_EXPERTS * SLOT_BLOCK
    n_slots = ((n_slots + 511) // 512) * 512
    tok = jnp.broadcast_to(jnp.arange(n, dtype=jnp.int32)[None, :], (TOP_K, n))
    slot_tok = jnp.zeros((n_slots,), jnp.int32).at[dest.reshape(-1)].set(tok.reshape(-1))
    blk_e = jnp.clip(jnp.searchsorted(pends, jnp.arange(n_slots // SLOT_BLOCK, dtype=jnp.int32) * SLOT_BLOCK,
                                      side='right'), 0, N_EXPERTS - 1).astype(jnp.int32)

    xs = _gather_rows(slot_tok, h2, n_slots)
    ys = _experts(blk_e, xs, w_gate_b, w_up_b, w_down_b)
    out = _combine(dest.reshape(-1), x2, wts8[0].reshape(n, 1), wts8[1].reshape(n, 1), row(g_final), ys)
    return out.reshape(batch, seq, d)
```

```python
import functools
import math

import numpy as np
import jax
import jax.numpy as jnp
from jax import lax
from jax.experimental import pallas as pl
from jax.experimental.pallas import tpu as pltpu

D_MODEL = 2048
D_RNN = D_MODEL
LRU_BLOCK = 128
N_LRU_BLOCKS = D_RNN // LRU_BLOCK
CONV_W = 4
C_LRU = 8.0
HEAD_DIM = 128
N_HEADS = D_MODEL // (2 * HEAD_DIM)
V_DIM = 2 * HEAD_DIM
QK_W = N_HEADS * 2 * HEAD_DIM
ATTN_W = N_HEADS * V_DIM
ROPE_THETA = 10000.0
SUBLN_EPS = 1e-5
N_GROUPS = 4
EXPERTS_PER_GROUP = 8
N_EXPERTS = N_GROUPS * EXPERTS_PER_GROUP
TOP_K = 2
D_EXPERT = D_MODEL // 2
NORM_EPS = 1e-6
LAM_INIT = 0.8 - 0.6 * math.exp(-0.3 * 0)

SLOT_BLOCK = 128
ROUTER_ROWS = 8 + N_EXPERTS
MASK_VALUE = -1e30
LOG2E = 1.4426950408889634
VMEM_LIMIT = 56 * 1024 * 1024

_F32 = jnp.float32
_BF16 = jnp.bfloat16


def _cparams(n_axes):
    return pltpu.CompilerParams(dimension_semantics=("arbitrary",) * n_axes,
                                vmem_limit_bytes=VMEM_LIMIT)


def _rmsnorm_kernel(x_ref, g_ref, o_ref):
    x = x_ref[...]
    y = x * lax.rsqrt(jnp.mean(x * x, axis=-1, keepdims=True) + NORM_EPS)
    o_ref[...] = (y * g_ref[...]).astype(o_ref.dtype)


def _rmsnorm(x, g, out_dtype):
    n, d = x.shape
    tm = min(512, n)
    return pl.pallas_call(
        _rmsnorm_kernel,
        grid=(n // tm,),
        in_specs=[pl.BlockSpec((tm, d), lambda i: (i, 0)),
                  pl.BlockSpec((1, d), lambda i: (0, 0))],
        out_specs=pl.BlockSpec((tm, d), lambda i: (i, 0)),
        out_shape=jax.ShapeDtypeStruct((n, d), out_dtype),
        compiler_params=_cparams(1),
        name="rmsnorm",
    )(x, g)


def _rope_table_kernel(pos_ref, inv_ref, sign_ref, cos_ref, sin_ref):
    ang = pos_ref[...].astype(_F32) * inv_ref[...]
    cos_ref[...] = jnp.cos(ang)
    sin_ref[...] = jnp.sin(ang) * sign_ref[...]


def _rope_tables(pos_col):
    n = pos_col.shape[0]
    tm = min(1024, n)
    inv = 1.0 / (ROPE_THETA ** (np.arange(0, HEAD_DIM, 2, dtype=np.float32) / np.float32(HEAD_DIM)))
    inv = np.concatenate([inv, inv]).astype(np.float32)[None, :]
    sign = np.concatenate([-np.ones(HEAD_DIM // 2), np.ones(HEAD_DIM // 2)]).astype(np.float32)[None, :]
    return pl.pallas_call(
        _rope_table_kernel,
        grid=(n // tm,),
        in_specs=[pl.BlockSpec((tm, 1), lambda i: (i, 0)),
                  pl.BlockSpec((1, HEAD_DIM), lambda i: (0, 0)),
                  pl.BlockSpec((1, HEAD_DIM), lambda i: (0, 0))],
        out_specs=[pl.BlockSpec((tm, HEAD_DIM), lambda i: (i, 0))] * 2,
        out_shape=[jax.ShapeDtypeStruct((n, HEAD_DIM), _F32)] * 2,
        compiler_params=_cparams(1),
        name="rope_tables",
    )(pos_col, jnp.asarray(inv), jnp.asarray(sign))


def _proj_plain_kernel(h_ref, w_ref, o_ref):
    o_ref[...] = jnp.dot(h_ref[...], w_ref[...], preferred_element_type=_F32).astype(o_ref.dtype)


def _proj_sigmoid_kernel(h_ref, w_ref, o_ref):
    acc = jnp.dot(h_ref[...], w_ref[...], preferred_element_type=_F32)
    o_ref[...] = jax.nn.sigmoid(acc).astype(o_ref.dtype)


def _proj_t_kernel(h_ref, w_ref, o_ref):
    acc = jnp.dot(h_ref[...], w_ref[...], preferred_element_type=_F32)
    for g in range(acc.shape[1] // HEAD_DIM):
        o_ref[g * HEAD_DIM:(g + 1) * HEAD_DIM, :] = acc[:, g * HEAD_DIM:(g + 1) * HEAD_DIM].T.astype(o_ref.dtype)


def _proj_rope_kernel(h_ref, w_ref, cos_ref, sin_ref, o_ref, *, scale, transpose_out):
    acc = jnp.dot(h_ref[...], w_ref[...], preferred_element_type=_F32)
    cos = cos_ref[...]
    sin = sin_ref[...]
    for g in range(acc.shape[1] // HEAD_DIM):
        t = acc[:, g * HEAD_DIM:(g + 1) * HEAD_DIM]
        r = (t * cos + pltpu.roll(t, HEAD_DIM // 2, axis=1) * sin) * scale
        if transpose_out:
            o_ref[g * HEAD_DIM:(g + 1) * HEAD_DIM, :] = r.T.astype(o_ref.dtype)
        else:
            o_ref[:, g * HEAD_DIM:(g + 1) * HEAD_DIM] = r.astype(o_ref.dtype)


def _projection(kernel_fn, h, w, col_off, n_cols, out_dtype, extra=(), transpose_out=False, name="proj"):
    n, k = h.shape
    tm = min(1024, n)
    tn = 1024
    off = col_off // tn
    extra_specs = [pl.BlockSpec((tm, e.shape[1]), lambda j, i: (i, 0)) for e in extra]
    if transpose_out:
        out_spec = pl.BlockSpec((tn, tm), lambda j, i: (j, i))
        out_shape = jax.ShapeDtypeStruct((n_cols, n), out_dtype)
    else:
        out_spec = pl.BlockSpec((tm, tn), lambda j, i: (i, j))
        out_shape = jax.ShapeDtypeStruct((n, n_cols), out_dtype)
    return pl.pallas_call(
        kernel_fn,
        grid=(n_cols // tn, n // tm),
        in_specs=[pl.BlockSpec((tm, k), lambda j, i: (i, 0)),
                  pl.BlockSpec((k, tn), lambda j, i: (0, j + off))] + extra_specs,
        out_specs=out_spec,
        out_shape=out_shape,
        compiler_params=_cparams(2),
        name=name,
    )(h, w, *extra)


def _lru_kernel(pos_ref, u_ref, gb_ref, cw_ref, cb_ref, wrg_ref, ba_ref, bx_ref, lp_ref, o_ref,
                ubuf, hcar, a_s, b_s, h_s):
    t_rows, cb = u_ref.shape
    n_grp = t_rows // 8

    @pl.when(pl.program_id(2) == 0)
    def _():
        ubuf[0:8, :] = jnp.zeros((8, cb), _F32)
        hcar[...] = jnp.zeros((8, cb), _F32)

    ubuf[8:8 + t_rows, :] = u_ref[...]
    cw = cw_ref[...]
    uc = cb_ref[...] + cw[0:1] * ubuf[5:5 + t_rows, :]
    uc = uc + cw[1:2] * ubuf[6:6 + t_rows, :]
    uc = uc + cw[2:3] * ubuf[7:7 + t_rows, :]
    uc = uc + cw[3:4] * ubuf[8:8 + t_rows, :]
    ubuf[0:8, :] = ubuf[t_rows:t_rows + 8, :]

    ucb = uc.astype(_BF16)
    r_parts, i_parts = [], []
    for j in range(cb // LRU_BLOCK):
        g = jnp.dot(ucb[:, j * LRU_BLOCK:(j + 1) * LRU_BLOCK], wrg_ref[j], preferred_element_type=_F32)
        r_parts.append(g[:, :LRU_BLOCK])
        i_parts.append(g[:, LRU_BLOCK:])
    r = jax.nn.sigmoid(jnp.concatenate(r_parts, axis=1) + ba_ref[...])
    gi = jax.nn.sigmoid(jnp.concatenate(i_parts, axis=1) + bx_ref[...])

    log_a = (-C_LRU) * r * jax.nn.softplus(-lp_ref[...])
    a_raw = jnp.exp(log_a)
    mult = jnp.sqrt(1.0 - a_raw * a_raw)
    reset = pos_ref[...] == 0
    a = jnp.where(reset, 0.0, a_raw)
    mult = jnp.where(reset, 1.0, mult)
    bv = uc * gi * mult

    a3 = a.reshape(n_grp, 8, cb)
    b3 = bv.reshape(n_grp, 8, cb)
    row = lax.broadcasted_iota(jnp.int32, (n_grp, 8, cb), 1)
    for d in (1, 2, 4):
        a_sh = pltpu.roll(a3, d, axis=1)
        b_sh = pltpu.roll(b3, d, axis=1)
        take = row >= d
        b3 = jnp.where(take, a3 * b_sh + b3, b3)
        a3 = jnp.where(take, a3 * a_sh, a3)
    a_s[...] = a3
    b_s[...] = b3

    def carry_step(g, carry):
        h = b_s[g] + a_s[g] * carry
        h_s[g] = h
        return jnp.broadcast_to(h[7:8, :], (8, cb))

    hcar[...] = lax.fori_loop(0, n_grp, carry_step, hcar[...], unroll=8)

    hr = h_s[...].reshape(t_rows, cb)
    o_ref[...] = (hr * jax.nn.gelu(gb_ref[...], approximate=True)).astype(o_ref.dtype)


def _lru(pos_col, ug, conv_w, conv_b, w_rg, b_a, b_x, lru_param, batch, seq):
    cb = 512
    t_rows = min(512, seq)
    n_cb = D_RNN // cb
    n_t = seq // t_rows
    row_map = lambda b, c, s: (b * n_t + s, 0)
    return pl.pallas_call(
        _lru_kernel,
        grid=(batch, n_cb, n_t),
        in_specs=[pl.BlockSpec((t_rows, 1), row_map),
                  pl.BlockSpec((t_rows, cb), lambda b, c, s: (b * n_t + s, c)),
                  pl.BlockSpec((t_rows, cb), lambda b, c, s: (b * n_t + s, n_cb + c)),
                  pl.BlockSpec((CONV_W, cb), lambda b, c, s: (0, c)),
                  pl.BlockSpec((1, cb), lambda b, c, s: (0, c)),
                  pl.BlockSpec((cb // LRU_BLOCK, LRU_BLOCK, 2 * LRU_BLOCK), lambda b, c, s: (c, 0, 0)),
                  pl.BlockSpec((1, cb), lambda b, c, s: (0, c)),
                  pl.BlockSpec((1, cb), lambda b, c, s: (0, c)),
                  pl.BlockSpec((1, cb), lambda b, c, s: (0, c))],
        out_specs=pl.BlockSpec((t_rows, cb), lambda b, c, s: (b * n_t + s, c)),
        out_shape=jax.ShapeDtypeStruct((batch * seq, D_RNN), _BF16),
        scratch_shapes=[pltpu.VMEM((t_rows + 8, cb), _F32),
                        pltpu.VMEM((8, cb), _F32),
                        pltpu.VMEM((t_rows // 8, 8, cb), _F32),
                        pltpu.VMEM((t_rows // 8, 8, cb), _F32),
                        pltpu.VMEM((t_rows // 8, 8, cb), _F32)],
        compiler_params=_cparams(3),
        name="conv_rglru",
    )(pos_col, ug, ug, conv_w, conv_b, w_rg, b_a, b_x, lru_param)


def _attn_kernel(qt_ref, k_ref, vt_ref, lq1_ref, lk1_ref, lq2_ref, lk2_ref, g_ref, o_ref,
                 m_s, l_s, acc_s, *, tk, cq):
    tq = qt_ref.shape[1]
    q0 = pl.program_id(2) * tq

    m_s[...] = jnp.full(m_s.shape, MASK_VALUE, _F32)
    l_s[...] = jnp.zeros(l_s.shape, _F32)
    acc_s[...] = jnp.zeros(acc_s.shape, _F32)

    def block(col0, diag):
        k_blk = k_ref[pl.ds(col0, tk), :]
        vt_blk = vt_ref[:, pl.ds(col0, tk)]
        for c in range(tq // cq):
            if diag is not None and (c + 1) * cq <= diag:
                continue
            masked = diag is not None and c * cq < diag + tk - 1
            cols = slice(c * cq, (c + 1) * cq)
            for mp in range(2):
                dims = slice(mp * HEAD_DIM, (mp + 1) * HEAD_DIM)
                st = jnp.dot(k_blk[:, dims], qt_ref[dims, cols], preferred_element_type=_F32)
                if masked:
                    kv_pos = diag + lax.broadcasted_iota(jnp.int32, (tk, cq), 0)
                    q_pos = c * cq + lax.broadcasted_iota(jnp.int32, (tk, cq), 1)
                    st = jnp.where(kv_pos <= q_pos, st, MASK_VALUE)
                m_prev = m_s[mp, :, cols]
                m_new = jnp.maximum(m_prev, jnp.max(st, axis=0, keepdims=True))
                alpha = jnp.exp2(m_prev - m_new)
                pt = jnp.exp2(st - m_new)
                l_s[mp, :, cols] = alpha * l_s[mp, :, cols] + jnp.sum(pt, axis=0, keepdims=True)
                pv = jnp.dot(vt_blk, pt.astype(_BF16), preferred_element_type=_F32)
                acc_s[mp, :, cols] = alpha * acc_s[mp, :, cols] + pv
                m_s[mp, :, cols] = m_new

    n_full = q0 // tk

    def full_block(j, carry):
        block(pl.multiple_of(j * tk, tk), None)
        return carry

    lax.fori_loop(0, n_full, full_block, 0)
    for d in range(max(tq // tk, 1)):
        block(pl.multiple_of((n_full + d) * tk, tk), d * tk)

    lam = (jnp.exp(jnp.sum(lq1_ref[...] * lk1_ref[...], axis=1, keepdims=True))
           - jnp.exp(jnp.sum(lq2_ref[...] * lk2_ref[...], axis=1, keepdims=True)) + LAM_INIT)
    ot = acc_s[0] / l_s[0] - lam * (acc_s[1] / l_s[1])
    yt = ot * lax.rsqrt(jnp.mean(ot * ot, axis=0, keepdims=True) + SUBLN_EPS)
    o_ref[...] = ((yt.T * g_ref[...]) * (1.0 - LAM_INIT)).astype(o_ref.dtype)


def _attention(qt, k, vt, lq1, lk1, lq2, lk2, subln_g, batch, seq):
    tq = min(1024, seq)
    tk = min(512, seq)
    cq = min(256, tq)
    nq = seq // tq
    vec = pl.BlockSpec((1, HEAD_DIM), lambda b, h, i: (0, 0))
    return pl.pallas_call(
        functools.partial(_attn_kernel, tk=tk, cq=cq),
        grid=(batch, N_HEADS, nq),
        in_specs=[pl.BlockSpec((V_DIM, tq), lambda b, h, i: (h, b * nq + i)),
                  pl.BlockSpec((seq, V_DIM), lambda b, h, i: (b, h)),
                  pl.BlockSpec((V_DIM, seq), lambda b, h, i: (h, b)),
                  vec, vec, vec, vec,
                  pl.BlockSpec((1, V_DIM), lambda b, h, i: (0, 0))],
        out_specs=pl.BlockSpec((tq, V_DIM), lambda b, h, i: (b * nq + i, h)),
        out_shape=jax.ShapeDtypeStruct((batch * seq, ATTN_W), _BF16),
        scratch_shapes=[pltpu.VMEM((2, 1, tq), _F32),
                        pltpu.VMEM((2, 1, tq), _F32),
                        pltpu.VMEM((2, V_DIM, tq), _F32)],
        compiler_params=_cparams(3),
        name="diff_attention",
    )(qt, k, vt, lq1, lk1, lq2, lk2, subln_g)


def _merge_kernel(yr_ref, ya_ref, wr_ref, wa_ref, gr_ref, ga_ref, o_ref):
    pr = jnp.dot(yr_ref[...], wr_ref[...], preferred_element_type=_F32)
    pa = jnp.dot(ya_ref[...], wa_ref[...], preferred_element_type=_F32)
    o_ref[...] = (gr_ref[...].astype(_F32) * pr + ga_ref[...].astype(_F32) * pa).astype(o_ref.dtype)


def _merge(y_rnn, y_attn, w_r, w_a, gates):
    n, k = y_rnn.shape
    tm = min(512, n)
    tn = 1024
    n_tn = D_MODEL // tn
    return pl.pallas_call(
        _merge_kernel,
        grid=(n_tn, n // tm),
        in_specs=[pl.BlockSpec((tm, k), lambda j, i: (i, 0)),
                  pl.BlockSpec((tm, k), lambda j, i: (i, 0)),
                  pl.BlockSpec((k, tn), lambda j, i: (0, j)),
                  pl.BlockSpec((k, tn), lambda j, i: (0, j)),
                  pl.BlockSpec((tm, tn), lambda j, i: (i, j)),
                  pl.BlockSpec((tm, tn), lambda j, i: (i, n_tn + j))],
        out_specs=pl.BlockSpec((tm, tn), lambda j, i: (i, j)),
        out_shape=jax.ShapeDtypeStruct((n, D_MODEL), _BF16),
        compiler_params=_cparams(2),
        name="branch_merge",
    )(y_rnn, y_attn, w_r, w_a, gates, gates)


def _outproj_kernel(m_ref, w_ref, x_ref, g_ref, wr_ref, x2_ref, h2_ref, lt_ref):
    x2 = x_ref[...] + jnp.dot(m_ref[...], w_ref[...], preferred_element_type=_F32)
    x2_ref[...] = x2
    h2 = (x2 * lax.rsqrt(jnp.mean(x2 * x2, axis=-1, keepdims=True) + NORM_EPS)) * g_ref[...]
    h2_ref[...] = h2
    lt_ref[...] = lax.dot_general(wr_ref[...], h2, (((1,), (1,)), ((), ())),
                                  precision=lax.Precision.HIGHEST, preferred_element_type=_F32)


def _outproj(merged, w_out, x, g_ffn, w_router_t):
    n, d = x.shape
    tm = min(256, n)
    return pl.pallas_call(
        _outproj_kernel,
        grid=(n // tm,),
        in_specs=[pl.BlockSpec((tm, d), lambda i: (i, 0)),
                  pl.BlockSpec((d, d), lambda i: (0, 0)),
                  pl.BlockSpec((tm, d), lambda i: (i, 0)),
                  pl.BlockSpec((1, d), lambda i: (0, 0)),
                  pl.BlockSpec((ROUTER_ROWS, d), lambda i: (0, 0))],
        out_specs=[pl.BlockSpec((tm, d), lambda i: (i, 0)),
                   pl.BlockSpec((tm, d), lambda i: (i, 0)),
                   pl.BlockSpec((ROUTER_ROWS, tm), lambda i: (0, i))],
        out_shape=[jax.ShapeDtypeStruct((n, d), _F32),
                   jax.ShapeDtypeStruct((n, d), _F32),
                   jax.ShapeDtypeStruct((ROUTER_ROWS, n), _F32)],
        compiler_params=_cparams(1),
        name="outproj_norm_router",
    )(merged, w_out, x, g_ffn, w_router_t)


def _first_index_of_max(vals, n_rows):
    vmax = jnp.max(vals, axis=0, keepdims=True)
    rows = lax.broadcasted_iota(jnp.int32, vals.shape, 0).astype(_F32)
    idx = jnp.min(jnp.where(vals == vmax, rows, float(n_rows)), axis=0, keepdims=True)
    return vmax, idx


def _route_kernel(lt_ref, tri_ref, eid_ref, wts_ref, rank_ref, cnt_ref, carry):
    tl = lt_ref.shape[1]

    @pl.when(pl.program_id(0) == 0)
    def _():
        carry[...] = jnp.zeros(carry.shape, _F32)

    lt = lt_ref[...]
    g = lt[0:N_GROUPS]
    ge = jnp.exp(g - jnp.max(g, axis=0, keepdims=True))
    gp = ge / jnp.sum(ge, axis=0, keepdims=True)
    g_w, g_idx = _first_index_of_max(gp, N_GROUPS)
    el = jnp.zeros((EXPERTS_PER_GROUP, tl), _F32)
    for gg in range(N_GROUPS):
        el = jnp.where(g_idx == float(gg), lt[8 + gg * EXPERTS_PER_GROUP:8 + (gg + 1) * EXPERTS_PER_GROUP], el)
    ee = jnp.exp(el - jnp.max(el, axis=0, keepdims=True))
    ep = ee / jnp.sum(ee, axis=0, keepdims=True)
    v1, i1 = _first_index_of_max(ep, EXPERTS_PER_GROUP)
    rows8 = lax.broadcasted_iota(jnp.int32, ep.shape, 0).astype(_F32)
    v2, i2 = _first_index_of_max(jnp.where(rows8 == i1, -1.0, ep), EXPERTS_PER_GROUP)
    den = v1 + v2
    e1 = g_idx * float(EXPERTS_PER_GROUP) + i1
    e2 = g_idx * float(EXPERTS_PER_GROUP) + i2

    rows_e = lax.broadcasted_iota(jnp.int32, (N_EXPERTS, tl), 0).astype(_F32)
    ranks = []
    for e_sel in (e1, e2):
        onehot = rows_e == e_sel
        oh = jnp.where(onehot, 1.0, 0.0)
        before = jnp.dot(oh.astype(_BF16), tri_ref[...], preferred_element_type=_F32) + carry[...]
        ranks.append(jnp.sum(jnp.where(onehot, before, 0.0), axis=0, keepdims=True))
        carry[...] = carry[...] + jnp.sum(oh, axis=1, keepdims=True)

    zeros6 = jnp.zeros((6, tl), _F32)
    eid_ref[...] = jnp.concatenate([e1, e2, zeros6], axis=0).astype(jnp.int32)
    wts_ref[...] = jnp.concatenate([g_w * (v1 / den), g_w * (v2 / den), zeros6], axis=0)
    rank_ref[...] = jnp.concatenate(ranks + [zeros6], axis=0).astype(jnp.int32)
    cnt_ref[...] = jnp.broadcast_to(carry[...], cnt_ref.shape).astype(jnp.int32)


def _route(logits_t):
    n = logits_t.shape[1]
    tl = min(512, n)
    tri = jnp.asarray(np.triu(np.ones((tl, tl), np.float32), k=1), _BF16)
    row_block = pl.BlockSpec((8, tl), lambda i: (0, i))
    return pl.pallas_call(
        _route_kernel,
        grid=(n // tl,),
        in_specs=[pl.BlockSpec((ROUTER_ROWS, tl), lambda i: (0, i)),
                  pl.BlockSpec((tl, tl), lambda i: (0, 0))],
        out_specs=[row_block, row_block, row_block,
                   pl.BlockSpec((N_EXPERTS, 128), lambda i: (0, 0))],
        out_shape=[jax.ShapeDtypeStruct((8, n), jnp.int32),
                   jax.ShapeDtypeStruct((8, n), _F32),
                   jax.ShapeDtypeStruct((8, n), jnp.int32),
                   jax.ShapeDtypeStruct((N_EXPERTS, 128), jnp.int32)],
        scratch_shapes=[pltpu.VMEM((N_EXPERTS, 1), _F32)],
        compiler_params=_cparams(1),
        name="route_topk_rank",
    )(logits_t, tri)


def _row_copy(src_hbm, row, dst_vmem, r, sem):
    return pltpu.make_async_copy(src_hbm.at[pl.ds(row, 1)], dst_vmem.at[pl.ds(r, 1)], sem)


def _gather_kernel(idx_ref, h_hbm, o_ref, rows, sem):
    tb = rows.shape[0]
    base = pl.program_id(0) * tb

    def issue(r, c):
        _row_copy(h_hbm, idx_ref[base + r], rows, r, sem).start()
        return c

    lax.fori_loop(0, tb, issue, 0, unroll=8)

    def drain(r, c):
        _row_copy(h_hbm, 0, rows, r, sem).wait()
        return c

    lax.fori_loop(0, tb, drain, 0, unroll=8)
    o_ref[...] = rows[...].astype(o_ref.dtype)


def _gather_rows(slot_tok, h2, n_slots):
    d = h2.shape[1]
    tb = 512
    return pl.pallas_call(
        _gather_kernel,
        grid_spec=pltpu.PrefetchScalarGridSpec(
            num_scalar_prefetch=1,
            grid=(n_slots // tb,),
            in_specs=[pl.BlockSpec(memory_space=pl.ANY)],
            out_specs=pl.BlockSpec((tb, d), lambda i, idx: (i, 0)),
            scratch_shapes=[pltpu.VMEM((tb, d), _F32), pltpu.SemaphoreType.DMA]),
        out_shape=jax.ShapeDtypeStruct((n_slots, d), _BF16),
        compiler_params=_cparams(1),
        name="slot_gather",
    )(slot_tok, h2)


def _expert_kernel(be_ref, x_ref, wg_ref, wu_ref, wd_ref, o_ref):
    x = x_ref[...]
    g = jnp.dot(x, wg_ref[...], preferred_element_type=_F32)
    u = jnp.dot(x, wu_ref[...], preferred_element_type=_F32)
    mid = (jax.nn.silu(g) * u).astype(_BF16)
    o_ref[...] = jnp.dot(mid, wd_ref[...], preferred_element_type=_F32)


def _experts(blk_e, xs, w_gate, w_up, w_down):
    p, d = xs.shape
    return pl.pallas_call(
        _expert_kernel,
        grid_spec=pltpu.PrefetchScalarGridSpec(
            num_scalar_prefetch=1,
            grid=(p // SLOT_BLOCK,),
            in_specs=[pl.BlockSpec((SLOT_BLOCK, d), lambda i, be: (i, 0)),
                      pl.BlockSpec((None, d, D_EXPERT), lambda i, be: (be[i], 0, 0)),
                      pl.BlockSpec((None, d, D_EXPERT), lambda i, be: (be[i], 0, 0)),
                      pl.BlockSpec((None, D_EXPERT, d), lambda i, be: (be[i], 0, 0))],
            out_specs=pl.BlockSpec((SLOT_BLOCK, d), lambda i, be: (i, 0))),
        out_shape=jax.ShapeDtypeStruct((p, d), _F32),
        compiler_params=_cparams(1),
        name="expert_mlp",
    )(blk_e, xs, w_gate, w_up, w_down)


def _combine_kernel(dest_ref, x2_ref, w0_ref, w1_ref, g_ref, ys_hbm, o_ref, rows, sem):
    tm = x2_ref.shape[0]
    n = pl.num_programs(0) * tm
    base = pl.program_id(0) * tm

    def issue(r, c):
        for k in range(TOP_K):
            _row_copy(ys_hbm, dest_ref[k * n + base + r], rows.at[k], r, sem).start()
        return c

    lax.fori_loop(0, tm, issue, 0, unroll=8)

    def drain(r, c):
        for k in range(TOP_K):
            _row_copy(ys_hbm, 0, rows.at[k], r, sem).wait()
        return c

    lax.fori_loop(0, tm, drain, 0, unroll=8)
    x3 = x2_ref[...] + (w0_ref[...] * rows[0] + w1_ref[...] * rows[1])
    y = x3 * lax.rsqrt(jnp.mean(x3 * x3, axis=-1, keepdims=True) + NORM_EPS)
    o_ref[...] = y * g_ref[...]


def _combine(dest, x2, w0, w1, g_final, ys):
    n, d = x2.shape
    tm = min(256, n)
    return pl.pallas_call(
        _combine_kernel,
        grid_spec=pltpu.PrefetchScalarGridSpec(
            num_scalar_prefetch=1,
            grid=(n // tm,),
            in_specs=[pl.BlockSpec((tm, d), lambda i, dst: (i, 0)),
                      pl.BlockSpec((tm, 1), lambda i, dst: (i, 0)),
                      pl.BlockSpec((tm, 1), lambda i, dst: (i, 0)),
                      pl.BlockSpec((1, d), lambda i, dst: (0, 0)),
                      pl.BlockSpec(memory_space=pl.ANY)],
            out_specs=pl.BlockSpec((tm, d), lambda i, dst: (i, 0)),
            scratch_shapes=[pltpu.VMEM((TOP_K, tm, d), _F32), pltpu.SemaphoreType.DMA]),
        out_shape=jax.ShapeDtypeStruct((n, d), _F32),
        compiler_params=_cparams(1),
        name="combine_final_norm",
    )(dest, x2, w0, w1, g_final, ys)


def kernel(x, positions, g_mix, w_in, conv_w, conv_b, w_rg_a, b_rg_a, w_rg_x, b_rg_x, lru_param, lambda_q1, lambda_k1, lambda_q2, lambda_k2, subln_g, w_br_rnn, w_br_attn, w_out, g_ffn, w_grp_router, w_exp_router, w_gate, w_up, w_down, g_final):
    batch, seq, d = x.shape
    n = batch * seq
    xf = x.reshape(n, d)
    pos_col = positions.reshape(n, 1)
    row = lambda v: v.reshape(1, -1)

    w_in_b = w_in[0].astype(_BF16)
    w_rg = jnp.concatenate([w_rg_a[0], w_rg_x[0]], axis=-1).astype(_BF16)
    w_r_b = w_br_rnn[0].astype(_BF16)
    w_a_b = w_br_attn[0].astype(_BF16)
    w_out_b = w_out[0].astype(_BF16)
    w_gate_b = w_gate[0].astype(_BF16)
    w_up_b = w_up[0].astype(_BF16)
    w_down_b = w_down[0].astype(_BF16)
    w_router_t = jnp.concatenate(
        [w_grp_router[0].T, jnp.zeros((8 - N_GROUPS, d), _F32), w_exp_router[0].T], axis=0)

    h = _rmsnorm(xf, row(g_mix[0]), _BF16)
    cos_t, sin_t = _rope_tables(pos_col)

    c_qk = 2 * D_RNN
    c_v = c_qk + 2 * QK_W
    c_g = c_v + ATTN_W
    ug = _projection(_proj_plain_kernel, h, w_in_b, 0, 2 * D_RNN, _F32, name="proj_rnn")
    q_scale = (HEAD_DIM ** -0.5) * LOG2E
    qt = _projection(functools.partial(_proj_rope_kernel, scale=q_scale, transpose_out=True),
                     h, w_in_b, c_qk, QK_W, _BF16, extra=(cos_t, sin_t), transpose_out=True, name="proj_q_rope")
    k = _projection(functools.partial(_proj_rope_kernel, scale=1.0, transpose_out=False),
                    h, w_in_b, c_qk + QK_W, QK_W, _BF16, extra=(cos_t, sin_t), name="proj_k_rope")
    vt = _projection(_proj_t_kernel, h, w_in_b, c_v, ATTN_W, _BF16, transpose_out=True, name="proj_v")
    gates = _projection(_proj_sigmoid_kernel, h, w_in_b, c_g, 2 * D_MODEL, _BF16, name="proj_gates")

    y_rnn = _lru(pos_col, ug, conv_w[0], row(conv_b[0]), w_rg, row(b_rg_a[0]), row(b_rg_x[0]),
                 row(lru_param[0]), batch, seq)
    y_attn = _attention(qt, k, vt, row(lambda_q1[0]), row(lambda_k1[0]), row(lambda_q2[0]), row(lambda_k2[0]),
                        row(subln_g[0]), batch, seq)
    merged = _merge(y_rnn, y_attn, w_r_b, w_a_b, gates)
    x2, h2, logits_t = _outproj(merged, w_out_b, xf, row(g_ffn[0]), w_router_t)

    eid8, wts8, rank8, cnt = _route(logits_t)
    eid, rank, counts = eid8[:TOP_K], rank8[:TOP_K], cnt[:, 0]

    pcounts = ((counts + SLOT_BLOCK - 1) // SLOT_BLOCK) * SLOT_BLOCK
    pends = jnp.cumsum(pcounts)
    pstarts = pends - pcounts
    dest = pstarts[eid] + rank
    n_slots = n * TOP_K + N_EXPERTS * SLOT_BLOCK
    n_slots = ((n_slots + 511) // 512) * 512
    tok = jnp.broadcast_to(jnp.arange(n, dtype=jnp.int32)[None, :], (TOP_K, n))
    slot_tok = jnp.zeros((n_slots,), jnp.int32).at[dest.reshape(-1)].set(tok.reshape(-1))
    blk_e = jnp.clip(jnp.searchsorted(pends, jnp.arange(n_slots // SLOT_BLOCK, dtype=jnp.int32) * SLOT_BLOCK,
                                      side='right'), 0, N_EXPERTS - 1).astype(jnp.int32)

    xs = _gather_rows(slot_tok, h2, n_slots)
    ys = _experts(blk_e, xs, w_gate_b, w_up_b, w_down_b)
    out = _combine(dest.reshape(-1), x2, wts8[0].reshape(n, 1), wts8[1].reshape(n, 1), row(g_final), ys)
    return out.reshape(batch, seq, d)
```

```python
import functools
import math

import numpy as np
import jax
import jax.numpy as jnp
from jax import lax
from jax.experimental import pallas as pl
from jax.experimental.pallas import tpu as pltpu

D_MODEL = 2048
D_RNN = D_MODEL
LRU_BLOCK = 128
N_LRU_BLOCKS = D_RNN // LRU_BLOCK
CONV_W = 4
C_LRU = 8.0
HEAD_DIM = 128
N_HEADS = D_MODEL // (2 * HEAD_DIM)
V_DIM = 2 * HEAD_DIM
QK_W = N_HEADS * 2 * HEAD_DIM
ATTN_W = N_HEADS * V_DIM
ROPE_THETA = 10000.0
SUBLN_EPS = 1e-5
N_GROUPS = 4
EXPERTS_PER_GROUP = 8
N_EXPERTS = N_GROUPS * EXPERTS_PER_GROUP
TOP_K = 2
D_EXPERT = D_MODEL // 2
NORM_EPS = 1e-6
LAM_INIT = 0.8 - 0.6 * math.exp(-0.3 * 0)

SLOT_BLOCK = 128
ROUTER_ROWS = 8 + N_EXPERTS
MASK_VALUE = -1e30
LOG2E = 1.4426950408889634
VMEM_LIMIT = 56 * 1024 * 1024
EXPERT_VMEM_LIMIT = 60 * 1024 * 1024

_F32 = jnp.float32
_BF16 = jnp.bfloat16


def _dot(a, b):
    return lax.dot_general(a, b, (((1,), (0,)), ((), ())), preferred_element_type=_F32)


def _cparams(n_axes):
    return pltpu.CompilerParams(dimension_semantics=("arbitrary",) * n_axes,
                                vmem_limit_bytes=VMEM_LIMIT)


def _rmsnorm_kernel(x_ref, g_ref, o_ref):
    x = x_ref[...]
    y = x * lax.rsqrt(jnp.mean(x * x, axis=-1, keepdims=True) + NORM_EPS)
    o_ref[...] = (y * g_ref[...]).astype(o_ref.dtype)


def _rmsnorm(x, g, out_dtype):
    n, d = x.shape
    tm = min(512, n)
    return pl.pallas_call(
        _rmsnorm_kernel,
        grid=(n // tm,),
        in_specs=[pl.BlockSpec((tm, d), lambda i: (i, 0)),
                  pl.BlockSpec((1, d), lambda i: (0, 0))],
        out_specs=pl.BlockSpec((tm, d), lambda i: (i, 0)),
        out_shape=jax.ShapeDtypeStruct((n, d), out_dtype),
        compiler_params=_cparams(1),
        name="rmsnorm",
    )(x, g)


def _rope_table_kernel(pos_ref, inv_ref, sign_ref, cos_ref, sin_ref):
    ang = pos_ref[...].astype(_F32) * inv_ref[...]
    cos_ref[...] = jnp.cos(ang)
    sin_ref[...] = jnp.sin(ang) * sign_ref[...]


def _rope_tables(pos_col):
    n = pos_col.shape[0]
    tm = min(1024, n)
    inv = 1.0 / (ROPE_THETA ** (np.arange(0, HEAD_DIM, 2, dtype=np.float32) / np.float32(HEAD_DIM)))
    inv = np.concatenate([inv, inv]).astype(np.float32)[None, :]
    sign = np.concatenate([-np.ones(HEAD_DIM // 2), np.ones(HEAD_DIM // 2)]).astype(np.float32)[None, :]
    return pl.pallas_call(
        _rope_table_kernel,
        grid=(n // tm,),
        in_specs=[pl.BlockSpec((tm, 1), lambda i: (i, 0)),
                  pl.BlockSpec((1, HEAD_DIM), lambda i: (0, 0)),
                  pl.BlockSpec((1, HEAD_DIM), lambda i: (0, 0))],
        out_specs=[pl.BlockSpec((tm, HEAD_DIM), lambda i: (i, 0))] * 2,
        out_shape=[jax.ShapeDtypeStruct((n, HEAD_DIM), _F32)] * 2,
        compiler_params=_cparams(1),
        name="rope_tables",
    )(pos_col, jnp.asarray(inv), jnp.asarray(sign))


def _proj_plain_kernel(h_ref, w_ref, o_ref):
    o_ref[...] = _dot(h_ref[...], w_ref[...]).astype(o_ref.dtype)


def _proj_sigmoid_kernel(h_ref, w_ref, o_ref):
    acc = _dot(h_ref[...], w_ref[...])
    o_ref[...] = jax.nn.sigmoid(acc).astype(o_ref.dtype)


def _proj_t_kernel(h_ref, w_ref, o_ref):
    acc = _dot(h_ref[...], w_ref[...])
    for g in range(acc.shape[1] // HEAD_DIM):
        o_ref[g * HEAD_DIM:(g + 1) * HEAD_DIM, :] = acc[:, g * HEAD_DIM:(g + 1) * HEAD_DIM].T.astype(o_ref.dtype)


def _proj_rope_kernel(h_ref, w_ref, cos_ref, sin_ref, o_ref, *, scale, transpose_out):
    acc = _dot(h_ref[...], w_ref[...])
    cos = cos_ref[...]
    sin = sin_ref[...]
    for g in range(acc.shape[1] // HEAD_DIM):
        t = acc[:, g * HEAD_DIM:(g + 1) * HEAD_DIM]
        r = (t * cos + pltpu.roll(t, HEAD_DIM // 2, axis=1) * sin) * scale
        if transpose_out:
            o_ref[g * HEAD_DIM:(g + 1) * HEAD_DIM, :] = r.T.astype(o_ref.dtype)
        else:
            o_ref[:, g * HEAD_DIM:(g + 1) * HEAD_DIM] = r.astype(o_ref.dtype)


def _projection(kernel_fn, h, w, col_off, n_cols, out_dtype, extra=(), transpose_out=False, name="proj"):
    n, k = h.shape
    tm = min(1024, n)
    tn = 1024
    off = col_off // tn
    extra_specs = [pl.BlockSpec((tm, e.shape[1]), lambda j, i: (i, 0)) for e in extra]
    if transpose_out:
        out_spec = pl.BlockSpec((tn, tm), lambda j, i: (j, i))
        out_shape = jax.ShapeDtypeStruct((n_cols, n), out_dtype)
    else:
        out_spec = pl.BlockSpec((tm, tn), lambda j, i: (i, j))
        out_shape = jax.ShapeDtypeStruct((n, n_cols), out_dtype)
    return pl.pallas_call(
        kernel_fn,
        grid=(n_cols // tn, n // tm),
        in_specs=[pl.BlockSpec((tm, k), lambda j, i: (i, 0)),
                  pl.BlockSpec((k, tn), lambda j, i: (0, j + off))] + extra_specs,
        out_specs=out_spec,
        out_shape=out_shape,
        compiler_params=_cparams(2),
        name=name,
    )(h, w, *extra)


def _lru_kernel(pos_ref, u_ref, gb_ref, cw_ref, cb_ref, wrg_ref, ba_ref, bx_ref, lp_ref, o_ref,
                ubuf, hcar, a_s, b_s, h_s):
    t_rows, cb = u_ref.shape
    n_grp = t_rows // 8

    @pl.when(pl.program_id(2) == 0)
    def _():
        ubuf[0:8, :] = jnp.zeros((8, cb), _F32)
        hcar[...] = jnp.zeros((8, cb), _F32)

    ubuf[8:8 + t_rows, :] = u_ref[...]
    cw = cw_ref[...]
    uc = cb_ref[...] + cw[0:1] * ubuf[5:5 + t_rows, :]
    uc = uc + cw[1:2] * ubuf[6:6 + t_rows, :]
    uc = uc + cw[2:3] * ubuf[7:7 + t_rows, :]
    uc = uc + cw[3:4] * ubuf[8:8 + t_rows, :]
    ubuf[0:8, :] = ubuf[t_rows:t_rows + 8, :]

    ucb = uc.astype(_BF16)
    r_parts, i_parts = [], []
    for j in range(cb // LRU_BLOCK):
        g = jnp.dot(ucb[:, j * LRU_BLOCK:(j + 1) * LRU_BLOCK], wrg_ref[j], preferred_element_type=_F32)
        r_parts.append(g[:, :LRU_BLOCK])
        i_parts.append(g[:, LRU_BLOCK:])
    r = jax.nn.sigmoid(jnp.concatenate(r_parts, axis=1) + ba_ref[...])
    gi = jax.nn.sigmoid(jnp.concatenate(i_parts, axis=1) + bx_ref[...])

    log_a = (-C_LRU) * r * jax.nn.softplus(-lp_ref[...])
    a_raw = jnp.exp(log_a)
    m2 = 1.0 - a_raw * a_raw
    mult = jnp.where(m2 > 0.0, m2 * lax.rsqrt(m2), 0.0)
    reset = pos_ref[...] == 0
    a = jnp.where(reset, 0.0, a_raw)
    mult = jnp.where(reset, 1.0, mult)
    bv = uc * gi * mult

    a3 = a.reshape(n_grp, 8, cb)
    b3 = bv.reshape(n_grp, 8, cb)
    row = lax.broadcasted_iota(jnp.int32, (n_grp, 8, cb), 1)
    for d in (1, 2, 4):
        a_sh = pltpu.roll(a3, d, axis=1)
        b_sh = pltpu.roll(b3, d, axis=1)
        take = row >= d
        b3 = jnp.where(take, a3 * b_sh + b3, b3)
        a3 = jnp.where(take, a3 * a_sh, a3)
    a_s[...] = a3
    b_s[...] = b3

    def carry_step(g, carry):
        h = b_s[g] + a_s[g] * carry
        h_s[g] = h
        return jnp.broadcast_to(h[7:8, :], (8, cb))

    hcar[...] = lax.fori_loop(0, n_grp, carry_step, hcar[...], unroll=8)

    hr = h_s[...].reshape(t_rows, cb)
    o_ref[...] = (hr * jax.nn.gelu(gb_ref[...], approximate=True)).astype(o_ref.dtype)


def _lru(pos_col, ug, conv_w, conv_b, w_rg, b_a, b_x, lru_param, batch, seq):
    cb = 512
    t_rows = min(512, seq)
    n_cb = D_RNN // cb
    n_t = seq // t_rows
    row_map = lambda b, c, s: (b * n_t + s, 0)
    return pl.pallas_call(
        _lru_kernel,
        grid=(batch, n_cb, n_t),
        in_specs=[pl.BlockSpec((t_rows, 1), row_map),
                  pl.BlockSpec((t_rows, cb), lambda b, c, s: (b * n_t + s, c)),
                  pl.BlockSpec((t_rows, cb), lambda b, c, s: (b * n_t + s, n_cb + c)),
                  pl.BlockSpec((CONV_W, cb), lambda b, c, s: (0, c)),
                  pl.BlockSpec((1, cb), lambda b, c, s: (0, c)),
                  pl.BlockSpec((cb // LRU_BLOCK, LRU_BLOCK, 2 * LRU_BLOCK), lambda b, c, s: (c, 0, 0)),
                  pl.BlockSpec((1, cb), lambda b, c, s: (0, c)),
                  pl.BlockSpec((1, cb), lambda b, c, s: (0, c)),
                  pl.BlockSpec((1, cb), lambda b, c, s: (0, c))],
        out_specs=pl.BlockSpec((t_rows, cb), lambda b, c, s: (b * n_t + s, c)),
        out_shape=jax.ShapeDtypeStruct((batch * seq, D_RNN), _BF16),
        scratch_shapes=[pltpu.VMEM((t_rows + 8, cb), _F32),
                        pltpu.VMEM((8, cb), _F32),
                        pltpu.VMEM((t_rows // 8, 8, cb), _F32),
                        pltpu.VMEM((t_rows // 8, 8, cb), _F32),
                        pltpu.VMEM((t_rows // 8, 8, cb), _F32)],
        compiler_params=_cparams(3),
        name="conv_rglru",
    )(pos_col, ug, ug, conv_w, conv_b, w_rg, b_a, b_x, lru_param)


def _attn_kernel(qt_ref, k_ref, vt_ref, lq1_ref, lk1_ref, lq2_ref, lk2_ref, g_ref, o_ref,
                 m_s, l_s, acc_s, *, tk, cq):
    tq = qt_ref.shape[1]
    q0 = pl.program_id(2) * tq

    m_s[...] = jnp.full(m_s.shape, MASK_VALUE, _F32)
    l_s[...] = jnp.zeros(l_s.shape, _F32)
    acc_s[...] = jnp.zeros(acc_s.shape, _F32)

    def block(col0, diag):
        k_blk = k_ref[pl.ds(col0, tk), :]
        vt_blk = vt_ref[:, pl.ds(col0, tk)]
        for c in range(tq // cq):
            if diag is not None and (c + 1) * cq <= diag:
                continue
            masked = diag is not None and c * cq < diag + tk - 1
            cols = slice(c * cq, (c + 1) * cq)
            for mp in range(2):
                dims = slice(mp * HEAD_DIM, (mp + 1) * HEAD_DIM)
                st = jnp.dot(k_blk[:, dims], qt_ref[dims, cols], preferred_element_type=_F32)
                if masked:
                    kv_pos = diag + lax.broadcasted_iota(jnp.int32, (tk, cq), 0)
                    q_pos = c * cq + lax.broadcasted_iota(jnp.int32, (tk, cq), 1)
                    st = jnp.where(kv_pos <= q_pos, st, MASK_VALUE)
                m_prev = m_s[mp, :, cols]
                m_new = jnp.maximum(m_prev, jnp.max(st, axis=0, keepdims=True))
                alpha = jnp.exp2(m_prev - m_new)
                pt = jnp.exp2(st - m_new)
                l_s[mp, :, cols] = alpha * l_s[mp, :, cols] + jnp.sum(pt, axis=0, keepdims=True)
                pv = jnp.dot(vt_blk, pt.astype(_BF16), preferred_element_type=_F32)
                acc_s[mp, :, cols] = alpha * acc_s[mp, :, cols] + pv
                m_s[mp, :, cols] = m_new

    n_full = q0 // tk

    def full_block(j, carry):
        block(pl.multiple_of(j * tk, tk), None)
        return carry

    lax.fori_loop(0, n_full, full_block, 0)
    for d in range(max(tq // tk, 1)):
        block(pl.multiple_of((n_full + d) * tk, tk), d * tk)

    lam = (jnp.exp(jnp.sum(lq1_ref[...] * lk1_ref[...], axis=1, keepdims=True))
           - jnp.exp(jnp.sum(lq2_ref[...] * lk2_ref[...], axis=1, keepdims=True)) + LAM_INIT)
    ot = acc_s[0] / l_s[0] - lam * (acc_s[1] / l_s[1])
    yt = ot * lax.rsqrt(jnp.mean(ot * ot, axis=0, keepdims=True) + SUBLN_EPS)
    o_ref[...] = ((yt.T * g_ref[...]) * (1.0 - LAM_INIT)).astype(o_ref.dtype)


def _attention(qt, k, vt, lq1, lk1, lq2, lk2, subln_g, batch, seq):
    tq = min(1024, seq)
    tk = min(512, seq)
    cq = min(256, tq)
    nq = seq // tq
    vec = pl.BlockSpec((1, HEAD_DIM), lambda b, h, i: (0, 0))
    return pl.pallas_call(
        functools.partial(_attn_kernel, tk=tk, cq=cq),
        grid=(batch, N_HEADS, nq),
        in_specs=[pl.BlockSpec((V_DIM, tq), lambda b, h, i: (h, b * nq + i)),
                  pl.BlockSpec((seq, V_DIM), lambda b, h, i: (b, h)),
                  pl.BlockSpec((V_DIM, seq), lambda b, h, i: (h, b)),
                  vec, vec, vec, vec,
                  pl.BlockSpec((1, V_DIM), lambda b, h, i: (0, 0))],
        out_specs=pl.BlockSpec((tq, V_DIM), lambda b, h, i: (b * nq + i, h)),
        out_shape=jax.ShapeDtypeStruct((batch * seq, ATTN_W), _BF16),
        scratch_shapes=[pltpu.VMEM((2, 1, tq), _F32),
                        pltpu.VMEM((2, 1, tq), _F32),
                        pltpu.VMEM((2, V_DIM, tq), _F32)],
        compiler_params=_cparams(3),
        name="diff_attention",
    )(qt, k, vt, lq1, lk1, lq2, lk2, subln_g)


def _merge_kernel(yr_ref, ya_ref, wr_ref, wa_ref, gr_ref, ga_ref, o_ref):
    pr = jnp.dot(yr_ref[...], wr_ref[...], preferred_element_type=_F32)
    pa = jnp.dot(ya_ref[...], wa_ref[...], preferred_element_type=_F32)
    o_ref[...] = (gr_ref[...].astype(_F32) * pr + ga_ref[...].astype(_F32) * pa).astype(o_ref.dtype)


def _merge(y_rnn, y_attn, w_r, w_a, gates):
    n, k = y_rnn.shape
    tm = min(512, n)
    tn = 1024
    n_tn = D_MODEL // tn
    return pl.pallas_call(
        _merge_kernel,
        grid=(n_tn, n // tm),
        in_specs=[pl.BlockSpec((tm, k), lambda j, i: (i, 0)),
                  pl.BlockSpec((tm, k), lambda j, i: (i, 0)),
                  pl.BlockSpec((k, tn), lambda j, i: (0, j)),
                  pl.BlockSpec((k, tn), lambda j, i: (0, j)),
                  pl.BlockSpec((tm, tn), lambda j, i: (i, j)),
                  pl.BlockSpec((tm, tn), lambda j, i: (i, n_tn + j))],
        out_specs=pl.BlockSpec((tm, tn), lambda j, i: (i, j)),
        out_shape=jax.ShapeDtypeStruct((n, D_MODEL), _BF16),
        compiler_params=_cparams(2),
        name="branch_merge",
    )(y_rnn, y_attn, w_r, w_a, gates, gates)


def _outproj_kernel(m_ref, w_ref, x_ref, g_ref, rhi_ref, rlo_ref, x2_ref, h2_ref, lt_ref):
    x2 = x_ref[...] + _dot(m_ref[...], w_ref[...])
    x2_ref[...] = x2
    h2 = (x2 * lax.rsqrt(jnp.mean(x2 * x2, axis=-1, keepdims=True) + NORM_EPS)) * g_ref[...]
    h2_ref[...] = h2
    h_hi = h2.astype(_BF16)
    h_lo = (h2 - h_hi.astype(_F32)).astype(_BF16)
    logits = _dot(h_hi, rhi_ref[...]) + (_dot(h_lo, rhi_ref[...]) + _dot(h_hi, rlo_ref[...]))
    lt_ref[...] = logits.T[:ROUTER_ROWS]


def _outproj(merged, w_out, x, g_ffn, w_router):
    n, d = x.shape
    tm = min(256, n)
    r_hi = w_router.astype(_BF16)
    r_lo = (w_router - r_hi.astype(_F32)).astype(_BF16)
    return pl.pallas_call(
        _outproj_kernel,
        grid=(n // tm,),
        in_specs=[pl.BlockSpec((tm, d), lambda i: (i, 0)),
                  pl.BlockSpec((d, d), lambda i: (0, 0)),
                  pl.BlockSpec((tm, d), lambda i: (i, 0)),
                  pl.BlockSpec((1, d), lambda i: (0, 0)),
                  pl.BlockSpec((d, 128), lambda i: (0, 0)),
                  pl.BlockSpec((d, 128), lambda i: (0, 0))],
        out_specs=[pl.BlockSpec((tm, d), lambda i: (i, 0)),
                   pl.BlockSpec((tm, d), lambda i: (i, 0)),
                   pl.BlockSpec((ROUTER_ROWS, tm), lambda i: (0, i))],
        out_shape=[jax.ShapeDtypeStruct((n, d), _F32),
                   jax.ShapeDtypeStruct((n, d), _F32),
                   jax.ShapeDtypeStruct((ROUTER_ROWS, n), _F32)],
        compiler_params=_cparams(1),
        name="outproj_norm_router",
    )(merged, w_out, x, g_ffn, r_hi, r_lo)


def _first_index_of_max(vals, n_rows):
    vmax = jnp.max(vals, axis=0, keepdims=True)
    rows = lax.broadcasted_iota(jnp.int32, vals.shape, 0).astype(_F32)
    idx = jnp.min(jnp.where(vals == vmax, rows, float(n_rows)), axis=0, keepdims=True)
    return vmax, idx


def _route_kernel(lt_ref, tri_ref, eid_ref, wts_ref, rank_ref, cnt_ref, carry):
    tl = lt_ref.shape[1]

    @pl.when(pl.program_id(0) == 0)
    def _():
        carry[...] = jnp.zeros(carry.shape, _F32)

    lt = lt_ref[...]
    g = lt[0:N_GROUPS]
    ge = jnp.exp(g - jnp.max(g, axis=0, keepdims=True))
    gp = ge / jnp.sum(ge, axis=0, keepdims=True)
    g_w, g_idx = _first_index_of_max(gp, N_GROUPS)
    el = jnp.zeros((EXPERTS_PER_GROUP, tl), _F32)
    for gg in range(N_GROUPS):
        el = jnp.where(g_idx == float(gg), lt[8 + gg * EXPERTS_PER_GROUP:8 + (gg + 1) * EXPERTS_PER_GROUP], el)
    ee = jnp.exp(el - jnp.max(el, axis=0, keepdims=True))
    ep = ee / jnp.sum(ee, axis=0, keepdims=True)
    v1, i1 = _first_index_of_max(ep, EXPERTS_PER_GROUP)
    rows8 = lax.broadcasted_iota(jnp.int32, ep.shape, 0).astype(_F32)
    v2, i2 = _first_index_of_max(jnp.where(rows8 == i1, -1.0, ep), EXPERTS_PER_GROUP)
    den = v1 + v2
    e1 = g_idx * float(EXPERTS_PER_GROUP) + i1
    e2 = g_idx * float(EXPERTS_PER_GROUP) + i2

    rows_e = lax.broadcasted_iota(jnp.int32, (N_EXPERTS, tl), 0).astype(_F32)
    ranks = []
    for e_sel in (e1, e2):
        onehot = rows_e == e_sel
        oh = jnp.where(onehot, 1.0, 0.0)
        before = jnp.dot(oh.astype(_BF16), tri_ref[...], preferred_element_type=_F32) + carry[...]
        ranks.append(jnp.sum(jnp.where(onehot, before, 0.0), axis=0, keepdims=True))
        carry[...] = carry[...] + jnp.sum(oh, axis=1, keepdims=True)

    zeros6 = jnp.zeros((6, tl), _F32)
    eid_ref[...] = jnp.concatenate([e1, e2, zeros6], axis=0).astype(jnp.int32)
    wts_ref[...] = jnp.concatenate([g_w * (v1 / den), g_w * (v2 / den), zeros6], axis=0)
    rank_ref[...] = jnp.concatenate(ranks + [zeros6], axis=0).astype(jnp.int32)
    cnt_ref[...] = jnp.broadcast_to(carry[...], cnt_ref.shape).astype(jnp.int32)


def _route(logits_t):
    n = logits_t.shape[1]
    tl = min(512, n)
    tri = jnp.asarray(np.triu(np.ones((tl, tl), np.float32), k=1), _BF16)
    row_block = pl.BlockSpec((8, tl), lambda i: (0, i))
    return pl.pallas_call(
        _route_kernel,
        grid=(n // tl,),
        in_specs=[pl.BlockSpec((ROUTER_ROWS, tl), lambda i: (0, i)),
                  pl.BlockSpec((tl, tl), lambda i: (0, 0))],
        out_specs=[row_block, row_block, row_block,
                   pl.BlockSpec((N_EXPERTS, 128), lambda i: (0, 0))],
        out_shape=[jax.ShapeDtypeStruct((8, n), jnp.int32),
                   jax.ShapeDtypeStruct((8, n), _F32),
                   jax.ShapeDtypeStruct((8, n), jnp.int32),
                   jax.ShapeDtypeStruct((N_EXPERTS, 128), jnp.int32)],
        scratch_shapes=[pltpu.VMEM((N_EXPERTS, 1), _F32)],
        compiler_params=_cparams(1),
        name="route_topk_rank",
    )(logits_t, tri)


def _row_copy(src_hbm, row, dst_vmem, r, sem):
    return pltpu.make_async_copy(src_hbm.at[pl.ds(row, 1)], dst_vmem.at[pl.ds(r, 1)], sem)


def _gather_kernel(idx_ref, h_hbm, o_ref, rows, sem):
    tb = rows.shape[1]
    i = pl.program_id(0)

    def issue(blk, slot):
        def body(r, c):
            _row_copy(h_hbm, idx_ref[blk * tb + r], rows.at[slot], r, sem.at[slot]).start()
            return c
        lax.fori_loop(0, tb, body, 0, unroll=8)

    @pl.when(i == 0)
    def _():
        issue(0, 0)

    @pl.when(i + 1 < pl.num_programs(0))
    def _():
        issue(i + 1, (i + 1) % 2)

    slot = i % 2

    def drain(r, c):
        _row_copy(h_hbm, 0, rows.at[slot], r, sem.at[slot]).wait()
        return c

    lax.fori_loop(0, tb, drain, 0, unroll=8)
    o_ref[...] = rows[slot].astype(o_ref.dtype)


def _gather_rows(slot_tok, h2, n_slots):
    d = h2.shape[1]
    tb = 256
    return pl.pallas_call(
        _gather_kernel,
        grid_spec=pltpu.PrefetchScalarGridSpec(
            num_scalar_prefetch=1,
            grid=(n_slots // tb,),
            in_specs=[pl.BlockSpec(memory_space=pl.ANY)],
            out_specs=pl.BlockSpec((tb, d), lambda i, idx: (i, 0)),
            scratch_shapes=[pltpu.VMEM((2, tb, d), _F32), pltpu.SemaphoreType.DMA((2,))]),
        out_shape=jax.ShapeDtypeStruct((n_slots, d), _BF16),
        compiler_params=_cparams(1),
        name="slot_gather",
    )(slot_tok, h2)


def _expert_kernel(be_ref, first_ref, slot_ref, nxt_ref, x_ref, wg_hbm, wu_hbm, wd_hbm, o_ref,
                   wg_buf, wu_buf, wd_buf, sem):
    i = pl.program_id(0)

    def weight_copies(e, s):
        return (pltpu.make_async_copy(wg_hbm.at[e], wg_buf.at[s], sem.at[s, 0]),
                pltpu.make_async_copy(wu_hbm.at[e], wu_buf.at[s], sem.at[s, 1]),
                pltpu.make_async_copy(wd_hbm.at[e], wd_buf.at[s], sem.at[s, 2]))

    @pl.when(i == 0)
    def _():
        for cp in weight_copies(be_ref[0], 0):
            cp.start()

    s = slot_ref[i]

    @pl.when(first_ref[i] == 1)
    def _():
        for cp in weight_copies(be_ref[i], s):
            cp.wait()

        @pl.when(nxt_ref[i] >= 0)
        def _():
            for cp in weight_copies(nxt_ref[i], 1 - s):
                cp.start()

    x = x_ref[...]
    g = _dot(x, wg_buf[s])
    u = _dot(x, wu_buf[s])
    mid = (jax.nn.silu(g) * u).astype(_BF16)
    o_ref[...] = _dot(mid, wd_buf[s])


def _experts(blk_e, xs, w_gate, w_up, w_down):
    p, d = xs.shape
    nb = p // SLOT_BLOCK
    first = jnp.concatenate([jnp.ones((1,), jnp.int32), (blk_e[1:] != blk_e[:-1]).astype(jnp.int32)])
    slot = (jnp.cumsum(first) - 1) % 2
    later = blk_e[None, :] > blk_e[:, None]
    nxt = jnp.min(jnp.where(later, blk_e[None, :], N_EXPERTS), axis=1)
    nxt = jnp.where(nxt < N_EXPERTS, nxt, -1).astype(jnp.int32)
    any_spec = pl.BlockSpec(memory_space=pl.ANY)
    return pl.pallas_call(
        _expert_kernel,
        grid_spec=pltpu.PrefetchScalarGridSpec(
            num_scalar_prefetch=4,
            grid=(nb,),
            in_specs=[pl.BlockSpec((SLOT_BLOCK, d), lambda i, *_: (i, 0)), any_spec, any_spec, any_spec],
            out_specs=pl.BlockSpec((SLOT_BLOCK, d), lambda i, *_: (i, 0)),
            scratch_shapes=[pltpu.VMEM((2, d, D_EXPERT), _F32),
                            pltpu.VMEM((2, d, D_EXPERT), _F32),
                            pltpu.VMEM((2, D_EXPERT, d), _F32),
                            pltpu.SemaphoreType.DMA((2, 3))]),
        out_shape=jax.ShapeDtypeStruct((p, d), _F32),
        compiler_params=pltpu.CompilerParams(dimension_semantics=("arbitrary",),
                                             vmem_limit_bytes=EXPERT_VMEM_LIMIT),
        name="expert_mlp",
    )(blk_e, first, slot.astype(jnp.int32), nxt, xs, w_gate, w_up, w_down)


def _combine_kernel(dest_ref, x2_ref, w0_ref, w1_ref, g_ref, ys_hbm, o_ref, rows, sem):
    tm = x2_ref.shape[0]
    n = pl.num_programs(0) * tm
    i = pl.program_id(0)

    def issue(tile, slot):
        def body(r, c):
            for k in range(TOP_K):
                _row_copy(ys_hbm, dest_ref[k * n + tile * tm + r], rows.at[slot, k], r, sem.at[slot]).start()
            return c
        lax.fori_loop(0, tm, body, 0, unroll=8)

    @pl.when(i == 0)
    def _():
        issue(0, 0)

    @pl.when(i + 1 < pl.num_programs(0))
    def _():
        issue(i + 1, (i + 1) % 2)

    slot = i % 2

    def drain(r, c):
        for k in range(TOP_K):
            _row_copy(ys_hbm, 0, rows.at[slot, k], r, sem.at[slot]).wait()
        return c

    lax.fori_loop(0, tm, drain, 0, unroll=8)
    x3 = x2_ref[...] + (w0_ref[...] * rows[slot, 0] + w1_ref[...] * rows[slot, 1])
    y = x3 * lax.rsqrt(jnp.mean(x3 * x3, axis=-1, keepdims=True) + NORM_EPS)
    o_ref[...] = y * g_ref[...]


def _combine(dest, x2, w0, w1, g_final, ys):
    n, d = x2.shape
    tm = min(256, n)
    return pl.pallas_call(
        _combine_kernel,
        grid_spec=pltpu.PrefetchScalarGridSpec(
            num_scalar_prefetch=1,
            grid=(n // tm,),
            in_specs=[pl.BlockSpec((tm, d), lambda i, dst: (i, 0)),
                      pl.BlockSpec((tm, 1), lambda i, dst: (i, 0)),
                      pl.BlockSpec((tm, 1), lambda i, dst: (i, 0)),
                      pl.BlockSpec((1, d), lambda i, dst: (0, 0)),
                      pl.BlockSpec(memory_space=pl.ANY)],
            out_specs=pl.BlockSpec((tm, d), lambda i, dst: (i, 0)),
            scratch_shapes=[pltpu.VMEM((2, TOP_K, tm, d), _F32), pltpu.SemaphoreType.DMA((2,))]),
        out_shape=jax.ShapeDtypeStruct((n, d), _F32),
        compiler_params=_cparams(1),
        name="combine_final_norm",
    )(dest, x2, w0, w1, g_final, ys)


def kernel(x, positions, g_mix, w_in, conv_w, conv_b, w_rg_a, b_rg_a, w_rg_x, b_rg_x, lru_param, lambda_q1, lambda_k1, lambda_q2, lambda_k2, subln_g, w_br_rnn, w_br_attn, w_out, g_ffn, w_grp_router, w_exp_router, w_gate, w_up, w_down, g_final):
    batch, seq, d = x.shape
    n = batch * seq
    xf = x.reshape(n, d)
    pos_col = positions.reshape(n, 1)
    row = lambda v: v.reshape(1, -1)

    w_rg = jnp.concatenate([w_rg_a[0], w_rg_x[0]], axis=-1).astype(_BF16)
    w_r_b = w_br_rnn[0].astype(_BF16)
    w_a_b = w_br_attn[0].astype(_BF16)
    w_out_b = w_out[0].astype(_BF16)
    w_router = jnp.concatenate(
        [w_grp_router[0], jnp.zeros((d, 8 - N_GROUPS), _F32), w_exp_router[0],
         jnp.zeros((d, 128 - ROUTER_ROWS), _F32)], axis=1)

    h = _rmsnorm(xf, row(g_mix[0]), _BF16)
    cos_t, sin_t = _rope_tables(pos_col)

    c_qk = 2 * D_RNN
    c_v = c_qk + 2 * QK_W
    c_g = c_v + ATTN_W
    ug = _projection(_proj_plain_kernel, h, w_in[0], 0, 2 * D_RNN, _F32, name="proj_rnn")
    q_scale = (HEAD_DIM ** -0.5) * LOG2E
    qt = _projection(functools.partial(_proj_rope_kernel, scale=q_scale, transpose_out=True),
                     h, w_in[0], c_qk, QK_W, _BF16, extra=(cos_t, sin_t), transpose_out=True, name="proj_q_rope")
    k = _projection(functools.partial(_proj_rope_kernel, scale=1.0, transpose_out=False),
                    h, w_in[0], c_qk + QK_W, QK_W, _BF16, extra=(cos_t, sin_t), name="proj_k_rope")
    vt = _projection(_proj_t_kernel, h, w_in[0], c_v, ATTN_W, _BF16, transpose_out=True, name="proj_v")
    gates = _projection(_proj_sigmoid_kernel, h, w_in[0], c_g, 2 * D_MODEL, _BF16, name="proj_gates")

    y_rnn = _lru(pos_col, ug, conv_w[0], row(conv_b[0]), w_rg, row(b_rg_a[0]), row(b_rg_x[0]),
                 row(lru_param[0]), batch, seq)
    y_attn = _attention(qt, k, vt, row(lambda_q1[0]), row(lambda_k1[0]), row(lambda_q2[0]), row(lambda_k2[0]),
                        row(subln_g[0]), batch, seq)
    merged = _merge(y_rnn, y_attn, w_r_b, w_a_b, gates)
    x2, h2, logits_t = _outproj(merged, w_out_b, xf, row(g_ffn[0]), w_router)

    eid8, wts8, rank8, cnt = _route(logits_t)
    eid, rank, counts = eid8[:TOP_K], rank8[:TOP_K], cnt[:, 0]

    pcounts = ((counts + SLOT_BLOCK - 1) // SLOT_BLOCK) * SLOT_BLOCK
    pends = jnp.cumsum(pcounts)
    pstarts = pends - pcounts
    experts = jnp.arange(N_EXPERTS, dtype=jnp.int32)
    dest = jnp.sum(jnp.where(eid[:, :, None] == experts, pstarts, 0), axis=-1) + rank
    n_slots = n * TOP_K + N_EXPERTS * SLOT_BLOCK
    n_slots = ((n_slots + 255) // 256) * 256
    tok = jnp.broadcast_to(jnp.arange(n, dtype=jnp.int32)[None, :], (TOP_K, n))
    slot_tok = jnp.zeros((n_slots,), jnp.int32).at[dest.reshape(-1)].set(tok.reshape(-1))
    blk_start = jnp.arange(n_slots // SLOT_BLOCK, dtype=jnp.int32) * SLOT_BLOCK
    blk_e = jnp.minimum(jnp.sum((blk_start[:, None] >= pends[None, :]).astype(jnp.int32), axis=1), N_EXPERTS - 1)

    xs = _gather_rows(slot_tok, h2, n_slots)
    ys = _experts(blk_e, xs, w_gate[0], w_up[0], w_down[0])
    out = _combine(dest.reshape(-1), x2, wts8[0].reshape(n, 1), wts8[1].reshape(n, 1), row(g_final), ys)
    return out.reshape(batch, seq, d)
```

```python
import functools
import math

import numpy as np
import jax
import jax.numpy as jnp
from jax import lax
from jax.experimental import pallas as pl
from jax.experimental.pallas import tpu as pltpu

D_MODEL = 2048
D_RNN = D_MODEL
LRU_BLOCK = 128
N_LRU_BLOCKS = D_RNN // LRU_BLOCK
CONV_W = 4
C_LRU = 8.0
HEAD_DIM = 128
N_HEADS = D_MODEL // (2 * HEAD_DIM)
V_DIM = 2 * HEAD_DIM
QK_W = N_HEADS * 2 * HEAD_DIM
ATTN_W = N_HEADS * V_DIM
ROPE_THETA = 10000.0
SUBLN_EPS = 1e-5
N_GROUPS = 4
EXPERTS_PER_GROUP = 8
N_EXPERTS = N_GROUPS * EXPERTS_PER_GROUP
TOP_K = 2
D_EXPERT = D_MODEL // 2
NORM_EPS = 1e-6
LAM_INIT = 0.8 - 0.6 * math.exp(-0.3 * 0)

SLOT_BLOCK = 128
ROUTER_ROWS = 8 + N_EXPERTS
MASK_VALUE = -1e30
LOG2E = 1.4426950408889634
VMEM_LIMIT = 56 * 1024 * 1024
EXPERT_VMEM_LIMIT = 60 * 1024 * 1024

_F32 = jnp.float32
_BF16 = jnp.bfloat16


def _dot(a, b):
    return lax.dot_general(a, b, (((1,), (0,)), ((), ())), preferred_element_type=_F32)


def _cparams(n_axes):
    return pltpu.CompilerParams(dimension_semantics=("arbitrary",) * n_axes,
                                vmem_limit_bytes=VMEM_LIMIT)


def _rmsnorm_kernel(x_ref, g_ref, o_ref):
    x = x_ref[...]
    y = x * lax.rsqrt(jnp.mean(x * x, axis=-1, keepdims=True) + NORM_EPS)
    o_ref[...] = (y * g_ref[...]).astype(o_ref.dtype)


def _rmsnorm(x, g, out_dtype):
    n, d = x.shape
    tm = min(512, n)
    return pl.pallas_call(
        _rmsnorm_kernel,
        grid=(n // tm,),
        in_specs=[pl.BlockSpec((tm, d), lambda i: (i, 0)),
                  pl.BlockSpec((1, d), lambda i: (0, 0))],
        out_specs=pl.BlockSpec((tm, d), lambda i: (i, 0)),
        out_shape=jax.ShapeDtypeStruct((n, d), out_dtype),
        compiler_params=_cparams(1),
        name="rmsnorm",
    )(x, g)


def _rope_table_kernel(pos_ref, inv_ref, sign_ref, cos_ref, sin_ref):
    ang = pos_ref[...].astype(_F32) * inv_ref[...]
    cos_ref[...] = jnp.cos(ang)
    sin_ref[...] = jnp.sin(ang) * sign_ref[...]


def _rope_tables(pos_col):
    n = pos_col.shape[0]
    tm = min(1024, n)
    inv = 1.0 / (ROPE_THETA ** (np.arange(0, HEAD_DIM, 2, dtype=np.float32) / np.float32(HEAD_DIM)))
    inv = np.concatenate([inv, inv]).astype(np.float32)[None, :]
    sign = np.concatenate([-np.ones(HEAD_DIM // 2), np.ones(HEAD_DIM // 2)]).astype(np.float32)[None, :]
    return pl.pallas_call(
        _rope_table_kernel,
        grid=(n // tm,),
        in_specs=[pl.BlockSpec((tm, 1), lambda i: (i, 0)),
                  pl.BlockSpec((1, HEAD_DIM), lambda i: (0, 0)),
                  pl.BlockSpec((1, HEAD_DIM), lambda i: (0, 0))],
        out_specs=[pl.BlockSpec((tm, HEAD_DIM), lambda i: (i, 0))] * 2,
        out_shape=[jax.ShapeDtypeStruct((n, HEAD_DIM), _F32)] * 2,
        compiler_params=_cparams(1),
        name="rope_tables",
    )(pos_col, jnp.asarray(inv), jnp.asarray(sign))


def _proj_plain_kernel(h_ref, w_ref, o_ref):
    o_ref[...] = _dot(h_ref[...], w_ref[...]).astype(o_ref.dtype)


def _proj_sigmoid_kernel(h_ref, w_ref, o_ref):
    acc = _dot(h_ref[...], w_ref[...])
    o_ref[...] = jax.nn.sigmoid(acc).astype(o_ref.dtype)


def _proj_t_kernel(h_ref, w_ref, o_ref):
    acc = _dot(h_ref[...], w_ref[...])
    for g in range(acc.shape[1] // HEAD_DIM):
        o_ref[g * HEAD_DIM:(g + 1) * HEAD_DIM, :] = acc[:, g * HEAD_DIM:(g + 1) * HEAD_DIM].T.astype(o_ref.dtype)


def _proj_rope_kernel(h_ref, w_ref, cos_ref, sin_ref, o_ref, *, scale, transpose_out):
    acc = _dot(h_ref[...], w_ref[...])
    cos = cos_ref[...]
    sin = sin_ref[...]
    for g in range(acc.shape[1] // HEAD_DIM):
        t = acc[:, g * HEAD_DIM:(g + 1) * HEAD_DIM]
        r = (t * cos + pltpu.roll(t, HEAD_DIM // 2, axis=1) * sin) * scale
        if transpose_out:
            o_ref[g * HEAD_DIM:(g + 1) * HEAD_DIM, :] = r.T.astype(o_ref.dtype)
        else:
            o_ref[:, g * HEAD_DIM:(g + 1) * HEAD_DIM] = r.astype(o_ref.dtype)


def _projection(kernel_fn, h, w, col_off, n_cols, out_dtype, extra=(), transpose_out=False, name="proj"):
    n, k = h.shape
    tm = min(1024, n)
    tn = 1024
    off = col_off // tn
    extra_specs = [pl.BlockSpec((tm, e.shape[1]), lambda j, i: (i, 0)) for e in extra]
    if transpose_out:
        out_spec = pl.BlockSpec((tn, tm), lambda j, i: (j, i))
        out_shape = jax.ShapeDtypeStruct((n_cols, n), out_dtype)
    else:
        out_spec = pl.BlockSpec((tm, tn), lambda j, i: (i, j))
        out_shape = jax.ShapeDtypeStruct((n, n_cols), out_dtype)
    return pl.pallas_call(
        kernel_fn,
        grid=(n_cols // tn, n // tm),
        in_specs=[pl.BlockSpec((tm, k), lambda j, i: (i, 0)),
                  pl.BlockSpec((k, tn), lambda j, i: (0, j + off))] + extra_specs,
        out_specs=out_spec,
        out_shape=out_shape,
        compiler_params=_cparams(2),
        name=name,
    )(h, w, *extra)


def _lru_kernel(pos_ref, u_ref, gb_ref, cw_ref, cb_ref, wrg_ref, ba_ref, bx_ref, lp_ref, o_ref,
                ubuf, hcar, a_s, b_s, h_s):
    t_rows, cb = u_ref.shape
    n_grp = t_rows // 8

    @pl.when(pl.program_id(2) == 0)
    def _():
        ubuf[0:8, :] = jnp.zeros((8, cb), _F32)
        hcar[...] = jnp.zeros((8, cb), _F32)

    ubuf[8:8 + t_rows, :] = u_ref[...]
    cw = cw_ref[...]
    uc = cb_ref[...] + cw[0:1] * ubuf[5:5 + t_rows, :]
    uc = uc + cw[1:2] * ubuf[6:6 + t_rows, :]
    uc = uc + cw[2:3] * ubuf[7:7 + t_rows, :]
    uc = uc + cw[3:4] * ubuf[8:8 + t_rows, :]
    ubuf[0:8, :] = ubuf[t_rows:t_rows + 8, :]

    ucb = uc.astype(_BF16)
    r_parts, i_parts = [], []
    for j in range(cb // LRU_BLOCK):
        g = jnp.dot(ucb[:, j * LRU_BLOCK:(j + 1) * LRU_BLOCK], wrg_ref[j], preferred_element_type=_F32)
        r_parts.append(g[:, :LRU_BLOCK])
        i_parts.append(g[:, LRU_BLOCK:])
    r = jax.nn.sigmoid(jnp.concatenate(r_parts, axis=1) + ba_ref[...])
    gi = jax.nn.sigmoid(jnp.concatenate(i_parts, axis=1) + bx_ref[...])

    log_a = (-C_LRU) * r * jax.nn.softplus(-lp_ref[...])
    a_raw = jnp.exp(log_a)
    m2 = 1.0 - a_raw * a_raw
    mult = jnp.where(m2 > 0.0, m2 * lax.rsqrt(m2), 0.0)
    reset = pos_ref[...] == 0
    a = jnp.where(reset, 0.0, a_raw)
    mult = jnp.where(reset, 1.0, mult)
    bv = uc * gi * mult

    a3 = a.reshape(n_grp, 8, cb)
    b3 = bv.reshape(n_grp, 8, cb)
    row = lax.broadcasted_iota(jnp.int32, (n_grp, 8, cb), 1)
    for d in (1, 2, 4):
        a_sh = pltpu.roll(a3, d, axis=1)
        b_sh = pltpu.roll(b3, d, axis=1)
        take = row >= d
        b3 = jnp.where(take, a3 * b_sh + b3, b3)
        a3 = jnp.where(take, a3 * a_sh, a3)
    a_s[...] = a3
    b_s[...] = b3

    def carry_step(g, carry):
        h = b_s[g] + a_s[g] * carry
        h_s[g] = h
        return jnp.broadcast_to(h[7:8, :], (8, cb))

    hcar[...] = lax.fori_loop(0, n_grp, carry_step, hcar[...], unroll=8)

    hr = h_s[...].reshape(t_rows, cb)
    o_ref[...] = (hr * jax.nn.gelu(gb_ref[...], approximate=True)).astype(o_ref.dtype)


def _lru(pos_col, ug, conv_w, conv_b, w_rg, b_a, b_x, lru_param, batch, seq):
    cb = 512
    t_rows = min(512, seq)
    n_cb = D_RNN // cb
    n_t = seq // t_rows
    row_map = lambda b, c, s: (b * n_t + s, 0)
    return pl.pallas_call(
        _lru_kernel,
        grid=(batch, n_cb, n_t),
        in_specs=[pl.BlockSpec((t_rows, 1), row_map),
                  pl.BlockSpec((t_rows, cb), lambda b, c, s: (b * n_t + s, c)),
                  pl.BlockSpec((t_rows, cb), lambda b, c, s: (b * n_t + s, n_cb + c)),
                  pl.BlockSpec((CONV_W, cb), lambda b, c, s: (0, c)),
                  pl.BlockSpec((1, cb), lambda b, c, s: (0, c)),
                  pl.BlockSpec((cb // LRU_BLOCK, LRU_BLOCK, 2 * LRU_BLOCK), lambda b, c, s: (c, 0, 0)),
                  pl.BlockSpec((1, cb), lambda b, c, s: (0, c)),
                  pl.BlockSpec((1, cb), lambda b, c, s: (0, c)),
                  pl.BlockSpec((1, cb), lambda b, c, s: (0, c))],
        out_specs=pl.BlockSpec((t_rows, cb), lambda b, c, s: (b * n_t + s, c)),
        out_shape=jax.ShapeDtypeStruct((batch * seq, D_RNN), _BF16),
        scratch_shapes=[pltpu.VMEM((t_rows + 8, cb), _F32),
                        pltpu.VMEM((8, cb), _F32),
                        pltpu.VMEM((t_rows // 8, 8, cb), _F32),
                        pltpu.VMEM((t_rows // 8, 8, cb), _F32),
                        pltpu.VMEM((t_rows // 8, 8, cb), _F32)],
        compiler_params=_cparams(3),
        name="conv_rglru",
    )(pos_col, ug, ug, conv_w, conv_b, w_rg, b_a, b_x, lru_param)


def _attn_kernel(qt_ref, k_ref, vt_ref, lq1_ref, lk1_ref, lq2_ref, lk2_ref, g_ref, o_ref,
                 m_s, l_s, acc_s, st_s, *, tk, cq):
    tq = qt_ref.shape[1]
    q0 = pl.program_id(2) * tq
    n_chunks = tq // cq
    n_diag = tq // tk

    m_s[...] = jnp.full(m_s.shape, MASK_VALUE, _F32)
    l_s[...] = jnp.zeros(l_s.shape, _F32)
    acc_s[...] = jnp.zeros(acc_s.shape, _F32)

    def chunks(diag):
        return [c for c in range(n_chunks) if diag is None or (c + 1) * cq > diag]

    def scores(col0, slot, diag):
        k_blk = k_ref[pl.ds(col0, tk), :]
        for c in chunks(diag):
            cols = slice(c * cq, (c + 1) * cq)
            for mp in range(2):
                dims = slice(mp * HEAD_DIM, (mp + 1) * HEAD_DIM)
                st_s[slot, mp, :, cols] = jnp.dot(k_blk[:, dims], qt_ref[dims, cols], preferred_element_type=_F32)

    def update(col0, slot, diag):
        vt_blk = vt_ref[:, pl.ds(col0, tk)]
        for c in chunks(diag):
            masked = diag is not None and c * cq < diag + tk - 1
            cols = slice(c * cq, (c + 1) * cq)
            for mp in range(2):
                st = st_s[slot, mp, :, cols]
                if masked:
                    kv_pos = diag + lax.broadcasted_iota(jnp.int32, (tk, cq), 0)
                    q_pos = c * cq + lax.broadcasted_iota(jnp.int32, (tk, cq), 1)
                    st = jnp.where(kv_pos <= q_pos, st, MASK_VALUE)
                m_prev = m_s[mp, :, cols]
                m_new = jnp.maximum(m_prev, jnp.max(st, axis=0, keepdims=True))
                alpha = jnp.exp2(m_prev - m_new)
                pt = jnp.exp2(st - m_new)
                l_s[mp, :, cols] = alpha * l_s[mp, :, cols] + jnp.sum(pt, axis=0, keepdims=True)
                pv = jnp.dot(vt_blk, pt.astype(_BF16), preferred_element_type=_F32)
                acc_s[mp, :, cols] = alpha * acc_s[mp, :, cols] + pv
                m_s[mp, :, cols] = m_new

    n_pairs = q0 // (2 * tk)
    scores(0, 0, None)

    def pair(t, carry):
        c0 = pl.multiple_of(2 * t * tk, tk)
        scores(c0 + tk, 1, None)
        update(c0, 0, None)
        scores(c0 + 2 * tk, 0, None)
        update(c0 + tk, 1, None)
        return carry

    lax.fori_loop(0, n_pairs, pair, 0)
    for d in range(n_diag):
        col0 = pl.multiple_of(q0 + d * tk, tk)
        if d + 1 < n_diag:
            scores(col0 + tk, (d + 1) % 2, (d + 1) * tk)
        update(col0, d % 2, d * tk)

    lam = (jnp.exp(jnp.sum(lq1_ref[...] * lk1_ref[...], axis=1, keepdims=True))
           - jnp.exp(jnp.sum(lq2_ref[...] * lk2_ref[...], axis=1, keepdims=True)) + LAM_INIT)
    ot = acc_s[0] / l_s[0] - lam * (acc_s[1] / l_s[1])
    yt = ot * lax.rsqrt(jnp.mean(ot * ot, axis=0, keepdims=True) + SUBLN_EPS)
    o_ref[...] = ((yt.T * g_ref[...]) * (1.0 - LAM_INIT)).astype(o_ref.dtype)


def _attention(qt, k, vt, lq1, lk1, lq2, lk2, subln_g, batch, seq):
    tq = min(1024, seq)
    tk = min(512, seq)
    cq = min(256, tq)
    nq = seq // tq
    assert tq % (2 * tk) == 0 and seq % tq == 0
    vec = pl.BlockSpec((1, HEAD_DIM), lambda b, h, i: (0, 0))
    return pl.pallas_call(
        functools.partial(_attn_kernel, tk=tk, cq=cq),
        grid=(batch, N_HEADS, nq),
        in_specs=[pl.BlockSpec((V_DIM, tq), lambda b, h, i: (h, b * nq + i)),
                  pl.BlockSpec((seq, V_DIM), lambda b, h, i: (b, h)),
                  pl.BlockSpec((V_DIM, seq), lambda b, h, i: (h, b)),
                  vec, vec, vec, vec,
                  pl.BlockSpec((1, V_DIM), lambda b, h, i: (0, 0))],
        out_specs=pl.BlockSpec((tq, V_DIM), lambda b, h, i: (b * nq + i, h)),
        out_shape=jax.ShapeDtypeStruct((batch * seq, ATTN_W), _BF16),
        scratch_shapes=[pltpu.VMEM((2, 1, tq), _F32),
                        pltpu.VMEM((2, 1, tq), _F32),
                        pltpu.VMEM((2, V_DIM, tq), _F32),
                        pltpu.VMEM((2, 2, tk, tq), _F32)],
        compiler_params=_cparams(3),
        name="diff_attention",
    )(qt, k, vt, lq1, lk1, lq2, lk2, subln_g)


def _merge_kernel(yr_ref, ya_ref, wr_ref, wa_ref, gr_ref, ga_ref, o_ref):
    pr = jnp.dot(yr_ref[...], wr_ref[...], preferred_element_type=_F32)
    pa = jnp.dot(ya_ref[...], wa_ref[...], preferred_element_type=_F32)
    o_ref[...] = (gr_ref[...].astype(_F32) * pr + ga_ref[...].astype(_F32) * pa).astype(o_ref.dtype)


def _merge(y_rnn, y_attn, w_r, w_a, gates):
    n, k = y_rnn.shape
    tm = min(512, n)
    tn = 1024
    n_tn = D_MODEL // tn
    return pl.pallas_call(
        _merge_kernel,
        grid=(n_tn, n // tm),
        in_specs=[pl.BlockSpec((tm, k), lambda j, i: (i, 0)),
                  pl.BlockSpec((tm, k), lambda j, i: (i, 0)),
                  pl.BlockSpec((k, tn), lambda j, i: (0, j)),
                  pl.BlockSpec((k, tn), lambda j, i: (0, j)),
                  pl.BlockSpec((tm, tn), lambda j, i: (i, j)),
                  pl.BlockSpec((tm, tn), lambda j, i: (i, n_tn + j))],
        out_specs=pl.BlockSpec((tm, tn), lambda j, i: (i, j)),
        out_shape=jax.ShapeDtypeStruct((n, D_MODEL), _BF16),
        compiler_params=_cparams(2),
        name="branch_merge",
    )(y_rnn, y_attn, w_r, w_a, gates, gates)


def _outproj_kernel(m_ref, w_ref, x_ref, g_ref, rhi_ref, rlo_ref, x2_ref, h2_ref, lt_ref):
    x2 = x_ref[...] + _dot(m_ref[...], w_ref[...])
    x2_ref[...] = x2
    h2 = (x2 * lax.rsqrt(jnp.mean(x2 * x2, axis=-1, keepdims=True) + NORM_EPS)) * g_ref[...]
    h2_ref[...] = h2
    h_hi = h2.astype(_BF16)
    h_lo = (h2 - h_hi.astype(_F32)).astype(_BF16)
    logits = _dot(h_hi, rhi_ref[...]) + (_dot(h_lo, rhi_ref[...]) + _dot(h_hi, rlo_ref[...]))
    lt_ref[...] = logits.T[:ROUTER_ROWS]


def _outproj(merged, w_out, x, g_ffn, w_router):
    n, d = x.shape
    tm = min(256, n)
    r_hi = w_router.astype(_BF16)
    r_lo = (w_router - r_hi.astype(_F32)).astype(_BF16)
    return pl.pallas_call(
        _outproj_kernel,
        grid=(n // tm,),
        in_specs=[pl.BlockSpec((tm, d), lambda i: (i, 0)),
                  pl.BlockSpec((d, d), lambda i: (0, 0)),
                  pl.BlockSpec((tm, d), lambda i: (i, 0)),
                  pl.BlockSpec((1, d), lambda i: (0, 0)),
                  pl.BlockSpec((d, 128), lambda i: (0, 0)),
                  pl.BlockSpec((d, 128), lambda i: (0, 0))],
        out_specs=[pl.BlockSpec((tm, d), lambda i: (i, 0)),
                   pl.BlockSpec((tm, d), lambda i: (i, 0)),
                   pl.BlockSpec((ROUTER_ROWS, tm), lambda i: (0, i))],
        out_shape=[jax.ShapeDtypeStruct((n, d), _F32),
                   jax.ShapeDtypeStruct((n, d), _F32),
                   jax.ShapeDtypeStruct((ROUTER_ROWS, n), _F32)],
        compiler_params=_cparams(1),
        name="outproj_norm_router",
    )(merged, w_out, x, g_ffn, r_hi, r_lo)


def _first_index_of_max(vals, n_rows):
    vmax = jnp.max(vals, axis=0, keepdims=True)
    rows = lax.broadcasted_iota(jnp.int32, vals.shape, 0).astype(_F32)
    idx = jnp.min(jnp.where(vals == vmax, rows, float(n_rows)), axis=0, keepdims=True)
    return vmax, idx


def _route_kernel(lt_ref, tri_ref, eid_ref, wts_ref, rank_ref, cnt_ref, carry):
    tl = lt_ref.shape[1]

    @pl.when(pl.program_id(0) == 0)
    def _():
        carry[...] = jnp.zeros(carry.shape, _F32)

    lt = lt_ref[...]
    g = lt[0:N_GROUPS]
    ge = jnp.exp(g - jnp.max(g, axis=0, keepdims=True))
    gp = ge / jnp.sum(ge, axis=0, keepdims=True)
    g_w, g_idx = _first_index_of_max(gp, N_GROUPS)
    el = jnp.zeros((EXPERTS_PER_GROUP, tl), _F32)
    for gg in range(N_GROUPS):
        el = jnp.where(g_idx == float(gg), lt[8 + gg * EXPERTS_PER_GROUP:8 + (gg + 1) * EXPERTS_PER_GROUP], el)
    ee = jnp.exp(el - jnp.max(el, axis=0, keepdims=True))
    ep = ee / jnp.sum(ee, axis=0, keepdims=True)
    v1, i1 = _first_index_of_max(ep, EXPERTS_PER_GROUP)
    rows8 = lax.broadcasted_iota(jnp.int32, ep.shape, 0).astype(_F32)
    v2, i2 = _first_index_of_max(jnp.where(rows8 == i1, -1.0, ep), EXPERTS_PER_GROUP)
    den = v1 + v2
    e1 = g_idx * float(EXPERTS_PER_GROUP) + i1
    e2 = g_idx * float(EXPERTS_PER_GROUP) + i2

    rows_e = lax.broadcasted_iota(jnp.int32, (N_EXPERTS, tl), 0).astype(_F32)
    ranks = []
    for e_sel in (e1, e2):
        onehot = rows_e == e_sel
        oh = jnp.where(onehot, 1.0, 0.0)
        before = jnp.dot(oh.astype(_BF16), tri_ref[...], preferred_element_type=_F32) + carry[...]
        ranks.append(jnp.sum(jnp.where(onehot, before, 0.0), axis=0, keepdims=True))
        carry[...] = carry[...] + jnp.sum(oh, axis=1, keepdims=True)

    zeros6 = jnp.zeros((6, tl), _F32)
    eid_ref[...] = jnp.concatenate([e1, e2, zeros6], axis=0).astype(jnp.int32)
    wts_ref[...] = jnp.concatenate([g_w * (v1 / den), g_w * (v2 / den), zeros6], axis=0)
    rank_ref[...] = jnp.concatenate(ranks + [zeros6], axis=0).astype(jnp.int32)
    cnt_ref[...] = jnp.broadcast_to(carry[...], cnt_ref.shape).astype(jnp.int32)


def _route(logits_t):
    n = logits_t.shape[1]
    tl = min(512, n)
    tri = jnp.asarray(np.triu(np.ones((tl, tl), np.float32), k=1), _BF16)
    row_block = pl.BlockSpec((8, tl), lambda i: (0, i))
    return pl.pallas_call(
        _route_kernel,
        grid=(n // tl,),
        in_specs=[pl.BlockSpec((ROUTER_ROWS, tl), lambda i: (0, i)),
                  pl.BlockSpec((tl, tl), lambda i: (0, 0))],
        out_specs=[row_block, row_block, row_block,
                   pl.BlockSpec((N_EXPERTS, 128), lambda i: (0, 0))],
        out_shape=[jax.ShapeDtypeStruct((8, n), jnp.int32),
                   jax.ShapeDtypeStruct((8, n), _F32),
                   jax.ShapeDtypeStruct((8, n), jnp.int32),
                   jax.ShapeDtypeStruct((N_EXPERTS, 128), jnp.int32)],
        scratch_shapes=[pltpu.VMEM((N_EXPERTS, 1), _F32)],
        compiler_params=_cparams(1),
        name="route_topk_rank",
    )(logits_t, tri)


def _row_copy(src_hbm, row, dst_vmem, r, sem):
    return pltpu.make_async_copy(src_hbm.at[pl.ds(row, 1)], dst_vmem.at[pl.ds(r, 1)], sem)


def _dispatch_kernel(dest_ref, cnt_ref, pstart_ref, pend_ref, h_ref, xs_hbm, rows, zero_rows, sem, zsem, *, n_steps):
    tm = h_ref.shape[0]
    i = pl.program_id(0)
    n = n_steps * tm
    n_slots = xs_hbm.shape[0]
    slot = i % 2
    n_jobs = 2 * N_EXPERTS
    jobs_per_step = -(-n_jobs // n_steps)

    def pad_job(j, wait):
        e = jnp.minimum(j, N_EXPERTS - 1)
        tail_lo = pend_ref[N_EXPERTS - 1] + (j - N_EXPERTS) * SLOT_BLOCK
        lo = jnp.where(j < N_EXPERTS, pstart_ref[e] + cnt_ref[e], tail_lo)
        hi = jnp.where(j < N_EXPERTS, pend_ref[e], jnp.minimum(tail_lo + SLOT_BLOCK, n_slots))

        def body(r, c):
            cp = pltpu.make_async_copy(zero_rows.at[pl.ds(0, 1)], xs_hbm.at[pl.ds(r, 1)], zsem)
            cp.wait() if wait else cp.start()
            return c
        lax.fori_loop(lo, jnp.maximum(lo, hi), body, 0)

    def pad_jobs(wait):
        for jj in range(jobs_per_step):
            j = i * jobs_per_step + jj

            @pl.when(j < n_jobs)
            def _():
                pad_job(j, wait)

    @pl.when(i == 0)
    def _():
        zero_rows[...] = jnp.zeros(zero_rows.shape, _F32)

    pad_jobs(False)

    def row_copy(tile_slot, r, row):
        return pltpu.make_async_copy(rows.at[tile_slot, pl.ds(r, 1)], xs_hbm.at[pl.ds(row, 1)], sem.at[tile_slot])

    def drain(tile_slot):
        def body(r, c):
            for _ in range(TOP_K):
                row_copy(tile_slot, r, 0).wait()
            return c
        lax.fori_loop(0, tm, body, 0, unroll=8)

    @pl.when(i >= 2)
    def _():
        drain(slot)

    rows[slot] = h_ref[...]

    def issue(r, c):
        for k in range(TOP_K):
            row_copy(slot, r, dest_ref[k * n + i * tm + r]).start()
        return c

    lax.fori_loop(0, tm, issue, 0, unroll=8)
    pad_jobs(True)

    @pl.when(i == n_steps - 1)
    def _():
        if n_steps >= 2:
            drain(1 - slot)
        drain(slot)


def _dispatch(dest, counts, pstarts, pends, h2, n_slots):
    n, d = h2.shape
    tm = min(256, n)
    return pl.pallas_call(
        functools.partial(_dispatch_kernel, n_steps=n // tm),
        grid_spec=pltpu.PrefetchScalarGridSpec(
            num_scalar_prefetch=4,
            grid=(n // tm,),
            in_specs=[pl.BlockSpec((tm, d), lambda i, *_: (i, 0))],
            out_specs=pl.BlockSpec(memory_space=pl.ANY),
            scratch_shapes=[pltpu.VMEM((2, tm, d), _F32), pltpu.VMEM((8, d), _F32),
                            pltpu.SemaphoreType.DMA((2,)), pltpu.SemaphoreType.DMA]),
        out_shape=jax.ShapeDtypeStruct((n_slots, d), _F32),
        compiler_params=_cparams(1),
        name="slot_dispatch",
    )(dest, counts, pstarts, pends, h2)


def _expert_kernel(be_ref, first_ref, slot_ref, nxt_ref, x_ref, wg_hbm, wu_hbm, wd_hbm, o_ref,
                   wg_buf, wu_buf, wd_buf, sem):
    i = pl.program_id(0)

    def weight_copies(e, s):
        return (pltpu.make_async_copy(wg_hbm.at[e], wg_buf.at[s], sem.at[s, 0]),
                pltpu.make_async_copy(wu_hbm.at[e], wu_buf.at[s], sem.at[s, 1]),
                pltpu.make_async_copy(wd_hbm.at[e], wd_buf.at[s], sem.at[s, 2]))

    @pl.when(i == 0)
    def _():
        for cp in weight_copies(be_ref[0], 0):
            cp.start()

    s = slot_ref[i]

    @pl.when(first_ref[i] == 1)
    def _():
        for cp in weight_copies(be_ref[i], s):
            cp.wait()

        @pl.when(nxt_ref[i] >= 0)
        def _():
            for cp in weight_copies(nxt_ref[i], 1 - s):
                cp.start()

    x = x_ref[...]
    g = _dot(x, wg_buf[s])
    u = _dot(x, wu_buf[s])
    mid = (jax.nn.silu(g) * u).astype(_BF16)
    o_ref[...] = _dot(mid, wd_buf[s])


def _experts(blk_e, xs, w_gate, w_up, w_down):
    p, d = xs.shape
    nb = p // SLOT_BLOCK
    first = jnp.concatenate([jnp.ones((1,), jnp.int32), (blk_e[1:] != blk_e[:-1]).astype(jnp.int32)])
    slot = (jnp.cumsum(first) - 1) % 2
    later = blk_e[None, :] > blk_e[:, None]
    nxt = jnp.min(jnp.where(later, blk_e[None, :], N_EXPERTS), axis=1)
    nxt = jnp.where(nxt < N_EXPERTS, nxt, -1).astype(jnp.int32)
    any_spec = pl.BlockSpec(memory_space=pl.ANY)
    return pl.pallas_call(
        _expert_kernel,
        grid_spec=pltpu.PrefetchScalarGridSpec(
            num_scalar_prefetch=4,
            grid=(nb,),
            in_specs=[pl.BlockSpec((SLOT_BLOCK, d), lambda i, *_: (i, 0)), any_spec, any_spec, any_spec],
            out_specs=pl.BlockSpec((SLOT_BLOCK, d), lambda i, *_: (i, 0)),
            scratch_shapes=[pltpu.VMEM((2, d, D_EXPERT), _F32),
                            pltpu.VMEM((2, d, D_EXPERT), _F32),
                            pltpu.VMEM((2, D_EXPERT, d), _F32),
                            pltpu.SemaphoreType.DMA((2, 3))]),
        out_shape=jax.ShapeDtypeStruct((p, d), _F32),
        compiler_params=pltpu.CompilerParams(dimension_semantics=("arbitrary",),
                                             vmem_limit_bytes=EXPERT_VMEM_LIMIT),
        name="expert_mlp",
    )(blk_e, first, slot.astype(jnp.int32), nxt, xs, w_gate, w_up, w_down)


def _combine_kernel(dest_ref, x2_ref, w0_ref, w1_ref, g_ref, ys_hbm, o_ref, rows, sem):
    tm = x2_ref.shape[0]
    n = pl.num_programs(0) * tm
    i = pl.program_id(0)

    def issue(tile, slot):
        def body(r, c):
            for k in range(TOP_K):
                _row_copy(ys_hbm, dest_ref[k * n + tile * tm + r], rows.at[slot, k], r, sem.at[slot]).start()
            return c
        lax.fori_loop(0, tm, body, 0, unroll=8)

    @pl.when(i == 0)
    def _():
        issue(0, 0)

    @pl.when(i + 1 < pl.num_programs(0))
    def _():
        issue(i + 1, (i + 1) % 2)

    slot = i % 2

    def drain(r, c):
        for k in range(TOP_K):
            _row_copy(ys_hbm, 0, rows.at[slot, k], r, sem.at[slot]).wait()
        return c

    lax.fori_loop(0, tm, drain, 0, unroll=8)
    x3 = x2_ref[...] + (w0_ref[...] * rows[slot, 0] + w1_ref[...] * rows[slot, 1])
    y = x3 * lax.rsqrt(jnp.mean(x3 * x3, axis=-1, keepdims=True) + NORM_EPS)
    o_ref[...] = y * g_ref[...]


def _combine(dest, x2, w0, w1, g_final, ys):
    n, d = x2.shape
    tm = min(256, n)
    return pl.pallas_call(
        _combine_kernel,
        grid_spec=pltpu.PrefetchScalarGridSpec(
            num_scalar_prefetch=1,
            grid=(n // tm,),
            in_specs=[pl.BlockSpec((tm, d), lambda i, dst: (i, 0)),
                      pl.BlockSpec((tm, 1), lambda i, dst: (i, 0)),
                      pl.BlockSpec((tm, 1), lambda i, dst: (i, 0)),
                      pl.BlockSpec((1, d), lambda i, dst: (0, 0)),
                      pl.BlockSpec(memory_space=pl.ANY)],
            out_specs=pl.BlockSpec((tm, d), lambda i, dst: (i, 0)),
            scratch_shapes=[pltpu.VMEM((2, TOP_K, tm, d), _F32), pltpu.SemaphoreType.DMA((2,))]),
        out_shape=jax.ShapeDtypeStruct((n, d), _F32),
        compiler_params=_cparams(1),
        name="combine_final_norm",
    )(dest, x2, w0, w1, g_final, ys)


def kernel(x, positions, g_mix, w_in, conv_w, conv_b, w_rg_a, b_rg_a, w_rg_x, b_rg_x, lru_param, lambda_q1, lambda_k1, lambda_q2, lambda_k2, subln_g, w_br_rnn, w_br_attn, w_out, g_ffn, w_grp_router, w_exp_router, w_gate, w_up, w_down, g_final):
    batch, seq, d = x.shape
    n = batch * seq
    xf = x.reshape(n, d)
    pos_col = positions.reshape(n, 1)
    row = lambda v: v.reshape(1, -1)

    w_rg = jnp.concatenate([w_rg_a[0], w_rg_x[0]], axis=-1).astype(_BF16)
    w_r_b = w_br_rnn[0].astype(_BF16)
    w_a_b = w_br_attn[0].astype(_BF16)
    w_out_b = w_out[0].astype(_BF16)
    w_router = jnp.concatenate(
        [w_grp_router[0], jnp.zeros((d, 8 - N_GROUPS), _F32), w_exp_router[0],
         jnp.zeros((d, 128 - ROUTER_ROWS), _F32)], axis=1)

    h = _rmsnorm(xf, row(g_mix[0]), _BF16)
    cos_t, sin_t = _rope_tables(pos_col)

    c_qk = 2 * D_RNN
    c_v = c_qk + 2 * QK_W
    c_g = c_v + ATTN_W
    ug = _projection(_proj_plain_kernel, h, w_in[0], 0, 2 * D_RNN, _F32, name="proj_rnn")
    q_scale = (HEAD_DIM ** -0.5) * LOG2E
    qt = _projection(functools.partial(_proj_rope_kernel, scale=q_scale, transpose_out=True),
                     h, w_in[0], c_qk, QK_W, _BF16, extra=(cos_t, sin_t), transpose_out=True, name="proj_q_rope")
    k = _projection(functools.partial(_proj_rope_kernel, scale=1.0, transpose_out=False),
                    h, w_in[0], c_qk + QK_W, QK_W, _BF16, extra=(cos_t, sin_t), name="proj_k_rope")
    vt = _projection(_proj_t_kernel, h, w_in[0], c_v, ATTN_W, _BF16, transpose_out=True, name="proj_v")
    gates = _projection(_proj_sigmoid_kernel, h, w_in[0], c_g, 2 * D_MODEL, _BF16, name="proj_gates")

    y_rnn = _lru(pos_col, ug, conv_w[0], row(conv_b[0]), w_rg, row(b_rg_a[0]), row(b_rg_x[0]),
                 row(lru_param[0]), batch, seq)
    y_attn = _attention(qt, k, vt, row(lambda_q1[0]), row(lambda_k1[0]), row(lambda_q2[0]), row(lambda_k2[0]),
                        row(subln_g[0]), batch, seq)
    merged = _merge(y_rnn, y_attn, w_r_b, w_a_b, gates)
    x2, h2, logits_t = _outproj(merged, w_out_b, xf, row(g_ffn[0]), w_router)

    eid8, wts8, rank8, cnt = _route(logits_t)
    eid, rank, counts = eid8[:TOP_K], rank8[:TOP_K], cnt[:, 0]

    pcounts = ((counts + SLOT_BLOCK - 1) // SLOT_BLOCK) * SLOT_BLOCK
    pends = jnp.cumsum(pcounts)
    pstarts = pends - pcounts
    experts = jnp.arange(N_EXPERTS, dtype=jnp.int32)
    dest = jnp.sum(jnp.where(eid[:, :, None] == experts, pstarts, 0), axis=-1) + rank
    n_slots = n * TOP_K + N_EXPERTS * SLOT_BLOCK
    blk_start = jnp.arange(n_slots // SLOT_BLOCK, dtype=jnp.int32) * SLOT_BLOCK
    blk_e = jnp.minimum(jnp.sum((blk_start[:, None] >= pends[None, :]).astype(jnp.int32), axis=1), N_EXPERTS - 1)

    dest_flat = dest.reshape(-1)
    xs = _dispatch(dest_flat, counts, pstarts, pends, h2, n_slots)
    ys = _experts(blk_e, xs, w_gate[0], w_up[0], w_down[0])
    out = _combine(dest_flat, x2, wts8[0].reshape(n, 1), wts8[1].reshape(n, 1), row(g_final), ys)
    return out.reshape(batch, seq, d)
```

```python
import functools
import math

import numpy as np
import jax
import jax.numpy as jnp
from jax import lax
from jax.experimental import pallas as pl
from jax.experimental.pallas import tpu as pltpu

D_MODEL = 2048
D_RNN = D_MODEL
LRU_BLOCK = 128
N_LRU_BLOCKS = D_RNN // LRU_BLOCK
CONV_W = 4
C_LRU = 8.0
HEAD_DIM = 128
N_HEADS = D_MODEL // (2 * HEAD_DIM)
V_DIM = 2 * HEAD_DIM
QK_W = N_HEADS * 2 * HEAD_DIM
ATTN_W = N_HEADS * V_DIM
ROPE_THETA = 10000.0
SUBLN_EPS = 1e-5
N_GROUPS = 4
EXPERTS_PER_GROUP = 8
N_EXPERTS = N_GROUPS * EXPERTS_PER_GROUP
TOP_K = 2
D_EXPERT = D_MODEL // 2
NORM_EPS = 1e-6
LAM_INIT = 0.8 - 0.6 * math.exp(-0.3 * 0)

SLOT_BLOCK = 256
HALF = D_MODEL // 2
ROUTER_ROWS = 8 + N_EXPERTS
MASK_VALUE = -1e30
LOG2E = 1.4426950408889634
VMEM_LIMIT = 56 * 1024 * 1024
EXPERT_VMEM_LIMIT = 60 * 1024 * 1024

_F32 = jnp.float32
_BF16 = jnp.bfloat16


def _dot(a, b):
    return lax.dot_general(a, b, (((1,), (0,)), ((), ())), preferred_element_type=_F32)


def _pack_halves(y):
    bits = lax.bitcast_convert_type(y.astype(_BF16).astype(_F32), jnp.uint32)
    return (bits[:, :HALF] >> 16) | bits[:, HALF:]


def _unpack_halves(p):
    lo = lax.bitcast_convert_type(p << 16, _F32)
    hi = lax.bitcast_convert_type(p & jnp.uint32(0xFFFF0000), _F32)
    return lo, hi


def _cparams(n_axes):
    return pltpu.CompilerParams(dimension_semantics=("arbitrary",) * n_axes,
                                vmem_limit_bytes=VMEM_LIMIT)


def _rmsnorm_kernel(x_ref, g_ref, o_ref):
    x = x_ref[...]
    y = x * lax.rsqrt(jnp.mean(x * x, axis=-1, keepdims=True) + NORM_EPS)
    o_ref[...] = (y * g_ref[...]).astype(o_ref.dtype)


def _rmsnorm(x, g, out_dtype):
    n, d = x.shape
    tm = min(512, n)
    return pl.pallas_call(
        _rmsnorm_kernel,
        grid=(n // tm,),
        in_specs=[pl.BlockSpec((tm, d), lambda i: (i, 0)),
                  pl.BlockSpec((1, d), lambda i: (0, 0))],
        out_specs=pl.BlockSpec((tm, d), lambda i: (i, 0)),
        out_shape=jax.ShapeDtypeStruct((n, d), out_dtype),
        compiler_params=_cparams(1),
        name="rmsnorm",
    )(x, g)


def _rope_table_kernel(pos_ref, inv_ref, sign_ref, cos_ref, sin_ref):
    ang = pos_ref[...].astype(_F32) * inv_ref[...]
    cos_ref[...] = jnp.cos(ang)
    sin_ref[...] = jnp.sin(ang) * sign_ref[...]


def _rope_tables(pos_col):
    n = pos_col.shape[0]
    tm = min(1024, n)
    inv = 1.0 / (ROPE_THETA ** (np.arange(0, HEAD_DIM, 2, dtype=np.float32) / np.float32(HEAD_DIM)))
    inv = np.concatenate([inv, inv]).astype(np.float32)[None, :]
    sign = np.concatenate([-np.ones(HEAD_DIM // 2), np.ones(HEAD_DIM // 2)]).astype(np.float32)[None, :]
    return pl.pallas_call(
        _rope_table_kernel,
        grid=(n // tm,),
        in_specs=[pl.BlockSpec((tm, 1), lambda i: (i, 0)),
                  pl.BlockSpec((1, HEAD_DIM), lambda i: (0, 0)),
                  pl.BlockSpec((1, HEAD_DIM), lambda i: (0, 0))],
        out_specs=[pl.BlockSpec((tm, HEAD_DIM), lambda i: (i, 0))] * 2,
        out_shape=[jax.ShapeDtypeStruct((n, HEAD_DIM), _F32)] * 2,
        compiler_params=_cparams(1),
        name="rope_tables",
    )(pos_col, jnp.asarray(inv), jnp.asarray(sign))


def _proj_plain_kernel(h_ref, w_ref, o_ref):
    o_ref[...] = _dot(h_ref[...], w_ref[...]).astype(o_ref.dtype)


def _proj_gelu_kernel(h_ref, w_ref, o_ref):
    o_ref[...] = jax.nn.gelu(_dot(h_ref[...], w_ref[...]), approximate=True).astype(o_ref.dtype)


def _proj_sigmoid_kernel(h_ref, w_ref, o_ref):
    acc = _dot(h_ref[...], w_ref[...])
    o_ref[...] = jax.nn.sigmoid(acc).astype(o_ref.dtype)


def _proj_t_kernel(h_ref, w_ref, o_ref):
    acc = _dot(h_ref[...], w_ref[...])
    for g in range(acc.shape[1] // HEAD_DIM):
        o_ref[g * HEAD_DIM:(g + 1) * HEAD_DIM, :] = acc[:, g * HEAD_DIM:(g + 1) * HEAD_DIM].T.astype(o_ref.dtype)


def _proj_rope_kernel(h_ref, w_ref, cos_ref, sin_ref, o_ref, *, scale, transpose_out):
    acc = _dot(h_ref[...], w_ref[...])
    cos = cos_ref[...]
    sin = sin_ref[...]
    for g in range(acc.shape[1] // HEAD_DIM):
        t = acc[:, g * HEAD_DIM:(g + 1) * HEAD_DIM]
        r = (t * cos + pltpu.roll(t, HEAD_DIM // 2, axis=1) * sin) * scale
        if transpose_out:
            o_ref[g * HEAD_DIM:(g + 1) * HEAD_DIM, :] = r.T.astype(o_ref.dtype)
        else:
            o_ref[:, g * HEAD_DIM:(g + 1) * HEAD_DIM] = r.astype(o_ref.dtype)


def _projection(kernel_fn, h, w, col_off, n_cols, out_dtype, extra=(), transpose_out=False, name="proj"):
    n, k = h.shape
    tm = min(1024, n)
    tn = 1024
    off = col_off // tn
    extra_specs = [pl.BlockSpec((tm, e.shape[1]), lambda j, i: (i, 0)) for e in extra]
    if transpose_out:
        out_spec = pl.BlockSpec((tn, tm), lambda j, i: (j, i))
        out_shape = jax.ShapeDtypeStruct((n_cols, n), out_dtype)
    else:
        out_spec = pl.BlockSpec((tm, tn), lambda j, i: (i, j))
        out_shape = jax.ShapeDtypeStruct((n, n_cols), out_dtype)
    return pl.pallas_call(
        kernel_fn,
        grid=(n_cols // tn, n // tm),
        in_specs=[pl.BlockSpec((tm, k), lambda j, i: (i, 0)),
                  pl.BlockSpec((k, tn), lambda j, i: (0, j + off))] + extra_specs,
        out_specs=out_spec,
        out_shape=out_shape,
        compiler_params=_cparams(2),
        name=name,
    )(h, w, *extra)


def _lru_kernel(pos_ref, u_ref, gg_ref, cw_ref, cb_ref, wrg_ref, ba_ref, bx_ref, lp_ref, o_ref,
                ubuf, hcar, a_s, b_s, h_s):
    t_rows, cb = u_ref.shape
    n_grp = t_rows // 8

    @pl.when(pl.program_id(2) == 0)
    def _():
        ubuf[0:8, :] = jnp.zeros((8, cb), _F32)
        hcar[...] = jnp.zeros((8, cb), _F32)

    ubuf[8:8 + t_rows, :] = u_ref[...]
    cw = cw_ref[...]
    uc = cb_ref[...] + cw[0:1] * ubuf[5:5 + t_rows, :]
    uc = uc + cw[1:2] * ubuf[6:6 + t_rows, :]
    uc = uc + cw[2:3] * ubuf[7:7 + t_rows, :]
    uc = uc + cw[3:4] * ubuf[8:8 + t_rows, :]
    ubuf[0:8, :] = ubuf[t_rows:t_rows + 8, :]

    ucb = uc.astype(_BF16)
    r_parts, i_parts = [], []
    for j in range(cb // LRU_BLOCK):
        g = jnp.dot(ucb[:, j * LRU_BLOCK:(j + 1) * LRU_BLOCK], wrg_ref[j], preferred_element_type=_F32)
        r_parts.append(g[:, :LRU_BLOCK])
        i_parts.append(g[:, LRU_BLOCK:])
    r = jax.nn.sigmoid(jnp.concatenate(r_parts, axis=1) + ba_ref[...])
    gi = jax.nn.sigmoid(jnp.concatenate(i_parts, axis=1) + bx_ref[...])

    log_a = (-C_LRU) * r * jax.nn.softplus(-lp_ref[...])
    a_raw = jnp.exp(log_a)
    m2 = 1.0 - a_raw * a_raw
    mult = jnp.where(m2 > 0.0, m2 * lax.rsqrt(m2), 0.0)
    reset = pos_ref[...] == 0
    a = jnp.where(reset, 0.0, a_raw)
    mult = jnp.where(reset, 1.0, mult)
    bv = uc * gi * mult

    a3 = a.reshape(n_grp, 8, cb)
    b3 = bv.reshape(n_grp, 8, cb)
    row = lax.broadcasted_iota(jnp.int32, (n_grp, 8, cb), 1)
    for d in (1, 2, 4):
        a_sh = pltpu.roll(a3, d, axis=1)
        b_sh = pltpu.roll(b3, d, axis=1)
        take = row >= d
        b3 = jnp.where(take, a3 * b_sh + b3, b3)
        a3 = jnp.where(take, a3 * a_sh, a3)
    a_s[...] = a3
    b_s[...] = b3

    def carry_step(g, carry):
        h = b_s[g] + a_s[g] * carry
        h_s[g] = h
        return jnp.broadcast_to(h[7:8, :], (8, cb))

    hcar[...] = lax.fori_loop(0, n_grp, carry_step, hcar[...], unroll=8)

    hr = h_s[...].reshape(t_rows, cb)
    o_ref[...] = (hr * gg_ref[...]).astype(o_ref.dtype)


def _lru(pos_col, u, gg, conv_w, conv_b, w_rg, b_a, b_x, lru_param, batch, seq):
    cb = 512
    t_rows = min(512, seq)
    n_cb = D_RNN // cb
    n_t = seq // t_rows
    row_map = lambda b, c, s: (b * n_t + s, 0)
    return pl.pallas_call(
        _lru_kernel,
        grid=(batch, n_cb, n_t),
        in_specs=[pl.BlockSpec((t_rows, 1), row_map),
                  pl.BlockSpec((t_rows, cb), lambda b, c, s: (b * n_t + s, c)),
                  pl.BlockSpec((t_rows, cb), lambda b, c, s: (b * n_t + s, c)),
                  pl.BlockSpec((CONV_W, cb), lambda b, c, s: (0, c)),
                  pl.BlockSpec((1, cb), lambda b, c, s: (0, c)),
                  pl.BlockSpec((cb // LRU_BLOCK, LRU_BLOCK, 2 * LRU_BLOCK), lambda b, c, s: (c, 0, 0)),
                  pl.BlockSpec((1, cb), lambda b, c, s: (0, c)),
                  pl.BlockSpec((1, cb), lambda b, c, s: (0, c)),
                  pl.BlockSpec((1, cb), lambda b, c, s: (0, c))],
        out_specs=pl.BlockSpec((t_rows, cb), lambda b, c, s: (b * n_t + s, c)),
        out_shape=jax.ShapeDtypeStruct((batch * seq, D_RNN), _BF16),
        scratch_shapes=[pltpu.VMEM((t_rows + 8, cb), _F32),
                        pltpu.VMEM((8, cb), _F32),
                        pltpu.VMEM((t_rows // 8, 8, cb), _F32),
                        pltpu.VMEM((t_rows // 8, 8, cb), _F32),
                        pltpu.VMEM((t_rows // 8, 8, cb), _F32)],
        compiler_params=_cparams(3),
        name="conv_rglru",
    )(pos_col, u, gg, conv_w, conv_b, w_rg, b_a, b_x, lru_param)


def _attn_kernel(qt_ref, k_ref, vt_ref, lq1_ref, lk1_ref, lq2_ref, lk2_ref, g_ref, o_ref,
                 m_s, l_s, acc_s, st_s, *, tk, cq):
    tq = qt_ref.shape[1]
    q0 = pl.program_id(2) * tq
    n_chunks = tq // cq
    n_diag = tq // tk

    m_s[...] = jnp.full(m_s.shape, MASK_VALUE, _F32)
    l_s[...] = jnp.zeros(l_s.shape, _F32)
    acc_s[...] = jnp.zeros(acc_s.shape, _F32)

    def chunks(diag):
        return [c for c in range(n_chunks) if diag is None or (c + 1) * cq > diag]

    def scores(col0, slot, diag):
        k_blk = k_ref[pl.ds(col0, tk), :]
        for c in chunks(diag):
            cols = slice(c * cq, (c + 1) * cq)
            for mp in range(2):
                dims = slice(mp * HEAD_DIM, (mp + 1) * HEAD_DIM)
                st_s[slot, mp, :, cols] = jnp.dot(k_blk[:, dims], qt_ref[dims, cols], preferred_element_type=_F32)

    def update(col0, slot, diag):
        vt_blk = vt_ref[:, pl.ds(col0, tk)]
        for c in chunks(diag):
            masked = diag is not None and c * cq < diag + tk - 1
            cols = slice(c * cq, (c + 1) * cq)
            for mp in range(2):
                st = st_s[slot, mp, :, cols]
                if masked:
                    kv_pos = diag + lax.broadcasted_iota(jnp.int32, (tk, cq), 0)
                    q_pos = c * cq + lax.broadcasted_iota(jnp.int32, (tk, cq), 1)
                    st = jnp.where(kv_pos <= q_pos, st, MASK_VALUE)
                m_prev = m_s[mp, :, cols]
                m_new = jnp.maximum(m_prev, jnp.max(st, axis=0, keepdims=True))
                alpha = jnp.exp2(m_prev - m_new)
                pt = jnp.exp2(st - m_new)
                l_s[mp, :, cols] = alpha * l_s[mp, :, cols] + jnp.sum(pt, axis=0, keepdims=True)
                pv = jnp.dot(vt_blk, pt.astype(_BF16), preferred_element_type=_F32)
                acc_s[mp, :, cols] = alpha * acc_s[mp, :, cols] + pv
                m_s[mp, :, cols] = m_new

    n_pairs = q0 // (2 * tk)
    scores(0, 0, None)

    def pair(t, carry):
        c0 = pl.multiple_of(2 * t * tk, tk)
        scores(c0 + tk, 1, None)
        update(c0, 0, None)
        scores(c0 + 2 * tk, 0, None)
        update(c0 + tk, 1, None)
        return carry

    lax.fori_loop(0, n_pairs, pair, 0)
    for d in range(n_diag):
        col0 = pl.multiple_of(q0 + d * tk, tk)
        if d + 1 < n_diag:
            scores(col0 + tk, (d + 1) % 2, (d + 1) * tk)
        update(col0, d % 2, d * tk)

    lam = (jnp.exp(jnp.sum(lq1_ref[...] * lk1_ref[...], axis=1, keepdims=True))
           - jnp.exp(jnp.sum(lq2_ref[...] * lk2_ref[...], axis=1, keepdims=True)) + LAM_INIT)
    ot = acc_s[0] / l_s[0] - lam * (acc_s[1] / l_s[1])
    yt = ot * lax.rsqrt(jnp.mean(ot * ot, axis=0, keepdims=True) + SUBLN_EPS)
    o_ref[...] = ((yt.T * g_ref[...]) * (1.0 - LAM_INIT)).astype(o_ref.dtype)


def _attention(qt, k, vt, lq1, lk1, lq2, lk2, subln_g, batch, seq):
    tq = min(1024, seq)
    tk = min(512, seq)
    cq = min(256, tq)
    nq = seq // tq
    assert tq % (2 * tk) == 0 and seq % tq == 0
    vec = pl.BlockSpec((1, HEAD_DIM), lambda b, h, i: (0, 0))
    return pl.pallas_call(
        functools.partial(_attn_kernel, tk=tk, cq=cq),
        grid=(batch, N_HEADS, nq),
        in_specs=[pl.BlockSpec((V_DIM, tq), lambda b, h, i: (h, b * nq + i)),
                  pl.BlockSpec((seq, V_DIM), lambda b, h, i: (b, h)),
                  pl.BlockSpec((V_DIM, seq), lambda b, h, i: (h, b)),
                  vec, vec, vec, vec,
                  pl.BlockSpec((1, V_DIM), lambda b, h, i: (0, 0))],
        out_specs=pl.BlockSpec((tq, V_DIM), lambda b, h, i: (b * nq + i, h)),
        out_shape=jax.ShapeDtypeStruct((batch * seq, ATTN_W), _BF16),
        scratch_shapes=[pltpu.VMEM((2, 1, tq), _F32),
                        pltpu.VMEM((2, 1, tq), _F32),
                        pltpu.VMEM((2, V_DIM, tq), _F32),
                        pltpu.VMEM((2, 2, tk, tq), _F32)],
        compiler_params=_cparams(3),
        name="diff_attention",
    )(qt, k, vt, lq1, lk1, lq2, lk2, subln_g)


def _merge_kernel(yr_ref, ya_ref, wr_ref, wa_ref, gr_ref, ga_ref, o_ref):
    pr = jnp.dot(yr_ref[...], wr_ref[...], preferred_element_type=_F32)
    pa = jnp.dot(ya_ref[...], wa_ref[...], preferred_element_type=_F32)
    o_ref[...] = (gr_ref[...].astype(_F32) * pr + ga_ref[...].astype(_F32) * pa).astype(o_ref.dtype)


def _merge(y_rnn, y_attn, w_r, w_a, gates):
    n, k = y_rnn.shape
    tm = min(512, n)
    tn = 1024
    n_tn = D_MODEL // tn
    return pl.pallas_call(
        _merge_kernel,
        grid=(n_tn, n // tm),
        in_specs=[pl.BlockSpec((tm, k), lambda j, i: (i, 0)),
                  pl.BlockSpec((tm, k), lambda j, i: (i, 0)),
                  pl.BlockSpec((k, tn), lambda j, i: (0, j)),
                  pl.BlockSpec((k, tn), lambda j, i: (0, j)),
                  pl.BlockSpec((tm, tn), lambda j, i: (i, j)),
                  pl.BlockSpec((tm, tn), lambda j, i: (i, n_tn + j))],
        out_specs=pl.BlockSpec((tm, tn), lambda j, i: (i, j)),
        out_shape=jax.ShapeDtypeStruct((n, D_MODEL), _BF16),
        compiler_params=_cparams(2),
        name="branch_merge",
    )(y_rnn, y_attn, w_r, w_a, gates, gates)


def _outproj_kernel(m_ref, w_ref, x_ref, g_ref, rhi_ref, rlo_ref, x2_ref, h2_ref, lt_ref):
    x2 = x_ref[...] + _dot(m_ref[...], w_ref[...])
    x2_ref[...] = x2
    h2 = (x2 * lax.rsqrt(jnp.mean(x2 * x2, axis=-1, keepdims=True) + NORM_EPS)) * g_ref[...]
    h2_ref[...] = _pack_halves(h2)
    h_hi = h2.astype(_BF16)
    h_lo = (h2 - h_hi.astype(_F32)).astype(_BF16)
    logits = _dot(h_hi, rhi_ref[...]) + (_dot(h_lo, rhi_ref[...]) + _dot(h_hi, rlo_ref[...]))
    lt_ref[...] = logits.T[:ROUTER_ROWS]


def _outproj(merged, w_out, x, g_ffn, w_router):
    n, d = x.shape
    tm = min(256, n)
    r_hi = w_router.astype(_BF16)
    r_lo = (w_router - r_hi.astype(_F32)).astype(_BF16)
    return pl.pallas_call(
        _outproj_kernel,
        grid=(n // tm,),
        in_specs=[pl.BlockSpec((tm, d), lambda i: (i, 0)),
                  pl.BlockSpec((d, d), lambda i: (0, 0)),
                  pl.BlockSpec((tm, d), lambda i: (i, 0)),
                  pl.BlockSpec((1, d), lambda i: (0, 0)),
                  pl.BlockSpec((d, 128), lambda i: (0, 0)),
                  pl.BlockSpec((d, 128), lambda i: (0, 0))],
        out_specs=[pl.BlockSpec((tm, d), lambda i: (i, 0)),
                   pl.BlockSpec((tm, HALF), lambda i: (i, 0)),
                   pl.BlockSpec((ROUTER_ROWS, tm), lambda i: (0, i))],
        out_shape=[jax.ShapeDtypeStruct((n, d), _F32),
                   jax.ShapeDtypeStruct((n, HALF), jnp.uint32),
                   jax.ShapeDtypeStruct((ROUTER_ROWS, n), _F32)],
        compiler_params=_cparams(1),
        name="outproj_norm_router",
    )(merged, w_out, x, g_ffn, r_hi, r_lo)


def _first_index_of_max(vals, n_rows):
    vmax = jnp.max(vals, axis=0, keepdims=True)
    rows = lax.broadcasted_iota(jnp.int32, vals.shape, 0).astype(_F32)
    idx = jnp.min(jnp.where(vals == vmax, rows, float(n_rows)), axis=0, keepdims=True)
    return vmax, idx


def _route_kernel(lt_ref, tri_ref, eid_ref, wts_ref, rank_ref, cnt_ref, carry):
    tl = lt_ref.shape[1]

    @pl.when(pl.program_id(0) == 0)
    def _():
        carry[...] = jnp.zeros(carry.shape, _F32)

    lt = lt_ref[...]
    g = lt[0:N_GROUPS]
    ge = jnp.exp(g - jnp.max(g, axis=0, keepdims=True))
    gp = ge / jnp.sum(ge, axis=0, keepdims=True)
    g_w, g_idx = _first_index_of_max(gp, N_GROUPS)
    el = jnp.zeros((EXPERTS_PER_GROUP, tl), _F32)
    for gg in range(N_GROUPS):
        el = jnp.where(g_idx == float(gg), lt[8 + gg * EXPERTS_PER_GROUP:8 + (gg + 1) * EXPERTS_PER_GROUP], el)
    ee = jnp.exp(el - jnp.max(el, axis=0, keepdims=True))
    ep = ee / jnp.sum(ee, axis=0, keepdims=True)
    v1, i1 = _first_index_of_max(ep, EXPERTS_PER_GROUP)
    rows8 = lax.broadcasted_iota(jnp.int32, ep.shape, 0).astype(_F32)
    v2, i2 = _first_index_of_max(jnp.where(rows8 == i1, -1.0, ep), EXPERTS_PER_GROUP)
    den = v1 + v2
    e1 = g_idx * float(EXPERTS_PER_GROUP) + i1
    e2 = g_idx * float(EXPERTS_PER_GROUP) + i2

    rows_e = lax.broadcasted_iota(jnp.int32, (N_EXPERTS, tl), 0).astype(_F32)
    ranks = []
    for e_sel in (e1, e2):
        onehot = rows_e == e_sel
        oh = jnp.where(onehot, 1.0, 0.0)
        before = jnp.dot(oh.astype(_BF16), tri_ref[...], preferred_element_type=_F32) + carry[...]
        ranks.append(jnp.sum(jnp.where(onehot, before, 0.0), axis=0, keepdims=True))
        carry[...] = carry[...] + jnp.sum(oh, axis=1, keepdims=True)

    zeros6 = jnp.zeros((6, tl), _F32)
    eid_ref[...] = jnp.concatenate([e1, e2, zeros6], axis=0).astype(jnp.int32)
    wts_ref[...] = jnp.concatenate([g_w * (v1 / den), g_w * (v2 / den), zeros6], axis=0)
    rank_ref[...] = jnp.concatenate(ranks + [zeros6], axis=0).astype(jnp.int32)
    cnt_ref[...] = jnp.broadcast_to(carry[...], cnt_ref.shape).astype(jnp.int32)


def _route(logits_t):
    n = logits_t.shape[1]
    tl = min(512, n)
    tri = jnp.asarray(np.triu(np.ones((tl, tl), np.float32), k=1), _BF16)
    row_block = pl.BlockSpec((8, tl), lambda i: (0, i))
    return pl.pallas_call(
        _route_kernel,
        grid=(n // tl,),
        in_specs=[pl.BlockSpec((ROUTER_ROWS, tl), lambda i: (0, i)),
                  pl.BlockSpec((tl, tl), lambda i: (0, 0))],
        out_specs=[row_block, row_block, row_block,
                   pl.BlockSpec((N_EXPERTS, 128), lambda i: (0, 0))],
        out_shape=[jax.ShapeDtypeStruct((8, n), jnp.int32),
                   jax.ShapeDtypeStruct((8, n), _F32),
                   jax.ShapeDtypeStruct((8, n), jnp.int32),
                   jax.ShapeDtypeStruct((N_EXPERTS, 128), jnp.int32)],
        scratch_shapes=[pltpu.VMEM((N_EXPERTS, 1), _F32)],
        compiler_params=_cparams(1),
        name="route_topk_rank",
    )(logits_t, tri)


def _row_copy(src_hbm, row, dst_vmem, r, sem):
    return pltpu.make_async_copy(src_hbm.at[pl.ds(row, 1)], dst_vmem.at[pl.ds(r, 1)], sem)


def _dispatch_kernel(dest_ref, cnt_ref, pstart_ref, pend_ref, h_ref, xs_hbm, rows, zero_rows, sem, zsem, *, n_steps):
    tm = h_ref.shape[0]
    i = pl.program_id(0)
    n = n_steps * tm
    n_slots = xs_hbm.shape[0]
    slot = i % 2
    n_jobs = 2 * N_EXPERTS
    jobs_per_step = -(-n_jobs // n_steps)

    def pad_job(j, wait):
        e = jnp.minimum(j, N_EXPERTS - 1)
        tail_lo = pend_ref[N_EXPERTS - 1] + (j - N_EXPERTS) * SLOT_BLOCK
        lo = jnp.where(j < N_EXPERTS, pstart_ref[e] + cnt_ref[e], tail_lo)
        hi = jnp.where(j < N_EXPERTS, pend_ref[e], jnp.minimum(tail_lo + SLOT_BLOCK, n_slots))

        def body(r, c):
            cp = pltpu.make_async_copy(zero_rows.at[pl.ds(0, 1)], xs_hbm.at[pl.ds(r, 1)], zsem)
            cp.wait() if wait else cp.start()
            return c
        lax.fori_loop(lo, jnp.maximum(lo, hi), body, 0)

    def pad_jobs(wait):
        for jj in range(jobs_per_step):
            j = i * jobs_per_step + jj

            @pl.when(j < n_jobs)
            def _():
                pad_job(j, wait)

    @pl.when(i == 0)
    def _():
        zero_rows[...] = jnp.zeros(zero_rows.shape, zero_rows.dtype)

    pad_jobs(False)

    def row_copy(tile_slot, r, row):
        return pltpu.make_async_copy(rows.at[tile_slot, pl.ds(r, 1)], xs_hbm.at[pl.ds(row, 1)], sem.at[tile_slot])

    def drain(tile_slot):
        def body(r, c):
            for _ in range(TOP_K):
                row_copy(tile_slot, r, 0).wait()
            return c
        lax.fori_loop(0, tm, body, 0, unroll=8)

    @pl.when(i >= 2)
    def _():
        drain(slot)

    rows[slot] = h_ref[...]

    def issue(r, c):
        for k in range(TOP_K):
            row_copy(slot, r, dest_ref[k * n + i * tm + r]).start()
        return c

    lax.fori_loop(0, tm, issue, 0, unroll=8)
    pad_jobs(True)

    @pl.when(i == n_steps - 1)
    def _():
        if n_steps >= 2:
            drain(1 - slot)
        drain(slot)


def _dispatch(dest, counts, pstarts, pends, h2, n_slots):
    n, d = h2.shape
    tm = min(256, n)
    return pl.pallas_call(
        functools.partial(_dispatch_kernel, n_steps=n // tm),
        grid_spec=pltpu.PrefetchScalarGridSpec(
            num_scalar_prefetch=4,
            grid=(n // tm,),
            in_specs=[pl.BlockSpec((tm, d), lambda i, *_: (i, 0))],
            out_specs=pl.BlockSpec(memory_space=pl.ANY),
            scratch_shapes=[pltpu.VMEM((2, tm, d), h2.dtype), pltpu.VMEM((8, d), h2.dtype),
                            pltpu.SemaphoreType.DMA((2,)), pltpu.SemaphoreType.DMA]),
        out_shape=jax.ShapeDtypeStruct((n_slots, d), h2.dtype),
        compiler_params=_cparams(1),
        name="slot_dispatch",
    )(dest, counts, pstarts, pends, h2)


def _expert_kernel(be_ref, first_ref, slot_ref, nxt_ref, x_ref, wg_hbm, wu_hbm, wd_hbm, o_ref,
                   wg_buf, wu_buf, wd_buf, sem):
    i = pl.program_id(0)

    def weight_copies(e, s):
        return (pltpu.make_async_copy(wg_hbm.at[e], wg_buf.at[s], sem.at[s, 0]),
                pltpu.make_async_copy(wu_hbm.at[e], wu_buf.at[s], sem.at[s, 1]),
                pltpu.make_async_copy(wd_hbm.at[e], wd_buf.at[s], sem.at[s, 2]))

    @pl.when(i == 0)
    def _():
        for cp in weight_copies(be_ref[0], 0):
            cp.start()

    s = slot_ref[i]

    @pl.when(first_ref[i] == 1)
    def _():
        for cp in weight_copies(be_ref[i], s):
            cp.wait()

        @pl.when(nxt_ref[i] >= 0)
        def _():
            for cp in weight_copies(nxt_ref[i], 1 - s):
                cp.start()

    x_lo, x_hi = _unpack_halves(x_ref[...])
    g = _dot(x_lo, wg_buf[s, :HALF]) + _dot(x_hi, wg_buf[s, HALF:])
    u = _dot(x_lo, wu_buf[s, :HALF]) + _dot(x_hi, wu_buf[s, HALF:])
    mid = (jax.nn.silu(g) * u).astype(_BF16)
    o_ref[...] = _pack_halves(_dot(mid, wd_buf[s]))


def _experts(blk_e, xs, w_gate, w_up, w_down):
    p = xs.shape[0]
    d = w_gate.shape[1]
    nb = p // SLOT_BLOCK
    first = jnp.concatenate([jnp.ones((1,), jnp.int32), (blk_e[1:] != blk_e[:-1]).astype(jnp.int32)])
    slot = (jnp.cumsum(first) - 1) % 2
    later = blk_e[None, :] > blk_e[:, None]
    nxt = jnp.min(jnp.where(later, blk_e[None, :], N_EXPERTS), axis=1)
    nxt = jnp.where(nxt < N_EXPERTS, nxt, -1).astype(jnp.int32)
    any_spec = pl.BlockSpec(memory_space=pl.ANY)
    return pl.pallas_call(
        _expert_kernel,
        grid_spec=pltpu.PrefetchScalarGridSpec(
            num_scalar_prefetch=4,
            grid=(nb,),
            in_specs=[pl.BlockSpec((SLOT_BLOCK, HALF), lambda i, *_: (i, 0)), any_spec, any_spec, any_spec],
            out_specs=pl.BlockSpec((SLOT_BLOCK, HALF), lambda i, *_: (i, 0)),
            scratch_shapes=[pltpu.VMEM((2, d, D_EXPERT), _F32),
                            pltpu.VMEM((2, d, D_EXPERT), _F32),
                            pltpu.VMEM((2, D_EXPERT, d), _F32),
                            pltpu.SemaphoreType.DMA((2, 3))]),
        out_shape=jax.ShapeDtypeStruct((p, HALF), jnp.uint32),
        compiler_params=pltpu.CompilerParams(dimension_semantics=("arbitrary",),
                                             vmem_limit_bytes=EXPERT_VMEM_LIMIT),
        name="expert_mlp",
    )(blk_e, first, slot.astype(jnp.int32), nxt, xs, w_gate, w_up, w_down)


def _combine_kernel(dest_ref, x2_ref, w0_ref, w1_ref, g_ref, ys_hbm, o_ref, rows, sem):
    tm = x2_ref.shape[0]
    n = pl.num_programs(0) * tm
    i = pl.program_id(0)

    def issue(tile, slot):
        def body(r, c):
            for k in range(TOP_K):
                _row_copy(ys_hbm, dest_ref[k * n + tile * tm + r], rows.at[slot, k], r, sem.at[slot]).start()
            return c
        lax.fori_loop(0, tm, body, 0, unroll=8)

    @pl.when(i == 0)
    def _():
        issue(0, 0)

    @pl.when(i + 1 < pl.num_programs(0))
    def _():
        issue(i + 1, (i + 1) % 2)

    slot = i % 2

    def drain(r, c):
        for k in range(TOP_K):
            _row_copy(ys_hbm, 0, rows.at[slot, k], r, sem.at[slot]).wait()
        return c

    lax.fori_loop(0, tm, drain, 0, unroll=8)
    y0 = _unpack_halves(rows[slot, 0])
    y1 = _unpack_halves(rows[slot, 1])
    w0 = w0_ref[...]
    w1 = w1_ref[...]
    halves = (slice(0, HALF), slice(HALF, 2 * HALF))
    x3 = [x2_ref[:, cols] + (w0 * a + w1 * b) for cols, a, b in zip(halves, y0, y1)]
    ms = (jnp.sum(x3[0] * x3[0], axis=-1, keepdims=True) + jnp.sum(x3[1] * x3[1], axis=-1, keepdims=True)) / (2 * HALF)
    inv = lax.rsqrt(ms + NORM_EPS)
    for cols, v in zip(halves, x3):
        o_ref[:, cols] = (v * inv) * g_ref[:, cols]


def _combine(dest, x2, w0, w1, g_final, ys):
    n, d = x2.shape
    tm = min(256, n)
    return pl.pallas_call(
        _combine_kernel,
        grid_spec=pltpu.PrefetchScalarGridSpec(
            num_scalar_prefetch=1,
            grid=(n // tm,),
            in_specs=[pl.BlockSpec((tm, d), lambda i, dst: (i, 0)),
                      pl.BlockSpec((tm, 1), lambda i, dst: (i, 0)),
                      pl.BlockSpec((tm, 1), lambda i, dst: (i, 0)),
                      pl.BlockSpec((1, d), lambda i, dst: (0, 0)),
                      pl.BlockSpec(memory_space=pl.ANY)],
            out_specs=pl.BlockSpec((tm, d), lambda i, dst: (i, 0)),
            scratch_shapes=[pltpu.VMEM((2, TOP_K, tm, HALF), jnp.uint32), pltpu.SemaphoreType.DMA((2,))]),
        out_shape=jax.ShapeDtypeStruct((n, d), _F32),
        compiler_params=_cparams(1),
        name="combine_final_norm",
    )(dest, x2, w0, w1, g_final, ys)


def kernel(x, positions, g_mix, w_in, conv_w, conv_b, w_rg_a, b_rg_a, w_rg_x, b_rg_x, lru_param, lambda_q1, lambda_k1, lambda_q2, lambda_k2, subln_g, w_br_rnn, w_br_attn, w_out, g_ffn, w_grp_router, w_exp_router, w_gate, w_up, w_down, g_final):
    batch, seq, d = x.shape
    n = batch * seq
    xf = x.reshape(n, d)
    pos_col = positions.reshape(n, 1)
    row = lambda v: v.reshape(1, -1)

    w_rg = jnp.concatenate([w_rg_a[0], w_rg_x[0]], axis=-1).astype(_BF16)
    w_r_b = w_br_rnn[0].astype(_BF16)
    w_a_b = w_br_attn[0].astype(_BF16)
    w_out_b = w_out[0].astype(_BF16)
    w_router = jnp.concatenate(
        [w_grp_router[0], jnp.zeros((d, 8 - N_GROUPS), _F32), w_exp_router[0],
         jnp.zeros((d, 128 - ROUTER_ROWS), _F32)], axis=1)

    h = _rmsnorm(xf, row(g_mix[0]), _BF16)
    cos_t, sin_t = _rope_tables(pos_col)

    c_qk = 2 * D_RNN
    c_v = c_qk + 2 * QK_W
    c_g = c_v + ATTN_W
    u = _projection(_proj_plain_kernel, h, w_in[0], 0, D_RNN, _F32, name="proj_rnn")
    gg = _projection(_proj_gelu_kernel, h, w_in[0], D_RNN, D_RNN, _F32, name="proj_rnn_gate")
    q_scale = (HEAD_DIM ** -0.5) * LOG2E
    qt = _projection(functools.partial(_proj_rope_kernel, scale=q_scale, transpose_out=True),
                     h, w_in[0], c_qk, QK_W, _BF16, extra=(cos_t, sin_t), transpose_out=True, name="proj_q_rope")
    k = _projection(functools.partial(_proj_rope_kernel, scale=1.0, transpose_out=False),
                    h, w_in[0], c_qk + QK_W, QK_W, _BF16, extra=(cos_t, sin_t), name="proj_k_rope")
    vt = _projection(_proj_t_kernel, h, w_in[0], c_v, ATTN_W, _BF16, transpose_out=True, name="proj_v")
    gates = _projection(_proj_sigmoid_kernel, h, w_in[0], c_g, 2 * D_MODEL, _BF16, name="proj_gates")

    y_rnn = _lru(pos_col, u, gg, conv_w[0], row(conv_b[0]), w_rg, row(b_rg_a[0]), row(b_rg_x[0]),
                 row(lru_param[0]), batch, seq)
    y_attn = _attention(qt, k, vt, row(lambda_q1[0]), row(lambda_k1[0]), row(lambda_q2[0]), row(lambda_k2[0]),
                        row(subln_g[0]), batch, seq)
    merged = _merge(y_rnn, y_attn, w_r_b, w_a_b, gates)
    x2, h2, logits_t = _outproj(merged, w_out_b, xf, row(g_ffn[0]), w_router)

    eid8, wts8, rank8, cnt = _route(logits_t)
    eid, rank, counts = eid8[:TOP_K], rank8[:TOP_K], cnt[:, 0]

    pcounts = ((counts + SLOT_BLOCK - 1) // SLOT_BLOCK) * SLOT_BLOCK
    pends = jnp.cumsum(pcounts)
    pstarts = pends - pcounts
    experts = jnp.arange(N_EXPERTS, dtype=jnp.int32)
    dest = jnp.sum(jnp.where(eid[:, :, None] == experts, pstarts, 0), axis=-1) + rank
    n_slots = n * TOP_K + N_EXPERTS * SLOT_BLOCK
    blk_start = jnp.arange(n_slots // SLOT_BLOCK, dtype=jnp.int32) * SLOT_BLOCK
    blk_e = jnp.minimum(jnp.sum((blk_start[:, None] >= pends[None, :]).astype(jnp.int32), axis=1), N_EXPERTS - 1)

    dest_flat = dest.reshape(-1)
    xs = _dispatch(dest_flat, counts, pstarts, pends, h2, n_slots)
    ys = _experts(blk_e, xs, w_gate[0], w_up[0], w_down[0])
    out = _combine(dest_flat, x2, wts8[0].reshape(n, 1), wts8[1].reshape(n, 1), row(g_final), ys)
    return out.reshape(batch, seq, d)
```

```python
import functools
import math

import numpy as np
import jax
import jax.numpy as jnp
from jax import lax
from jax.experimental import pallas as pl
from jax.experimental.pallas import tpu as pltpu

D_MODEL = 2048
D_RNN = D_MODEL
LRU_BLOCK = 128
N_LRU_BLOCKS = D_RNN // LRU_BLOCK
CONV_W = 4
C_LRU = 8.0
HEAD_DIM = 128
N_HEADS = D_MODEL // (2 * HEAD_DIM)
V_DIM = 2 * HEAD_DIM
QK_W = N_HEADS * 2 * HEAD_DIM
ATTN_W = N_HEADS * V_DIM
ROPE_THETA = 10000.0
SUBLN_EPS = 1e-5
N_GROUPS = 4
EXPERTS_PER_GROUP = 8
N_EXPERTS = N_GROUPS * EXPERTS_PER_GROUP
TOP_K = 2
D_EXPERT = D_MODEL // 2
NORM_EPS = 1e-6
LAM_INIT = 0.8 - 0.6 * math.exp(-0.3 * 0)

SLOT_BLOCK = 256
HALF = D_MODEL // 2
TILE_ROWS = HALF // 128
ROUTER_ROWS = 8 + N_EXPERTS
MASK_VALUE = -1e30
LOG2E = 1.4426950408889634
VMEM_LIMIT = 56 * 1024 * 1024
EXPERT_VMEM_LIMIT = 60 * 1024 * 1024

_F32 = jnp.float32
_BF16 = jnp.bfloat16


def _dot(a, b):
    return lax.dot_general(a, b, (((1,), (0,)), ((), ())), preferred_element_type=_F32)


def _pack_halves(y):
    bits = lax.bitcast_convert_type(y.astype(_BF16).astype(_F32), jnp.uint32)
    return (bits[:, :HALF] >> 16) | bits[:, HALF:]


def _unpack_halves(p):
    lo = lax.bitcast_convert_type(p << 16, _F32)
    hi = lax.bitcast_convert_type(p & jnp.uint32(0xFFFF0000), _F32)
    return lo, hi


def _token_rows(t):
    return pl.ds(pl.multiple_of(t * TILE_ROWS, TILE_ROWS), TILE_ROWS)


def _store_token_tiles(ref, packed):
    tokens = packed.shape[0]
    for j in range(TILE_ROWS):
        ref[pl.ds(j, tokens, stride=TILE_ROWS), :] = packed[:, j * 128:(j + 1) * 128]


def _load_token_tiles(ref):
    tokens = ref.shape[0] // TILE_ROWS
    return jnp.concatenate([ref[pl.ds(j, tokens, stride=TILE_ROWS), :] for j in range(TILE_ROWS)], axis=1)


def _cparams(n_axes):
    return pltpu.CompilerParams(dimension_semantics=("arbitrary",) * n_axes,
                                vmem_limit_bytes=VMEM_LIMIT)


def _rmsnorm_kernel(x_ref, g_ref, o_ref):
    x = x_ref[...]
    y = x * lax.rsqrt(jnp.mean(x * x, axis=-1, keepdims=True) + NORM_EPS)
    o_ref[...] = (y * g_ref[...]).astype(o_ref.dtype)


def _rmsnorm(x, g, out_dtype):
    n, d = x.shape
    tm = min(512, n)
    return pl.pallas_call(
        _rmsnorm_kernel,
        grid=(n // tm,),
        in_specs=[pl.BlockSpec((tm, d), lambda i: (i, 0)),
                  pl.BlockSpec((1, d), lambda i: (0, 0))],
        out_specs=pl.BlockSpec((tm, d), lambda i: (i, 0)),
        out_shape=jax.ShapeDtypeStruct((n, d), out_dtype),
        compiler_params=_cparams(1),
        name="rmsnorm",
    )(x, g)


def _rope_table_kernel(pos_ref, inv_ref, sign_ref, cos_ref, sin_ref):
    ang = pos_ref[...].astype(_F32) * inv_ref[...]
    cos_ref[...] = jnp.cos(ang)
    sin_ref[...] = jnp.sin(ang) * sign_ref[...]


def _rope_tables(pos_col):
    n = pos_col.shape[0]
    tm = min(1024, n)
    inv = 1.0 / (ROPE_THETA ** (np.arange(0, HEAD_DIM, 2, dtype=np.float32) / np.float32(HEAD_DIM)))
    inv = np.concatenate([inv, inv]).astype(np.float32)[None, :]
    sign = np.concatenate([-np.ones(HEAD_DIM // 2), np.ones(HEAD_DIM // 2)]).astype(np.float32)[None, :]
    return pl.pallas_call(
        _rope_table_kernel,
        grid=(n // tm,),
        in_specs=[pl.BlockSpec((tm, 1), lambda i: (i, 0)),
                  pl.BlockSpec((1, HEAD_DIM), lambda i: (0, 0)),
                  pl.BlockSpec((1, HEAD_DIM), lambda i: (0, 0))],
        out_specs=[pl.BlockSpec((tm, HEAD_DIM), lambda i: (i, 0))] * 2,
        out_shape=[jax.ShapeDtypeStruct((n, HEAD_DIM), _F32)] * 2,
        compiler_params=_cparams(1),
        name="rope_tables",
    )(pos_col, jnp.asarray(inv), jnp.asarray(sign))


def _proj_plain_kernel(h_ref, w_ref, o_ref):
    o_ref[...] = _dot(h_ref[...], w_ref[...]).astype(o_ref.dtype)


def _proj_gelu_kernel(h_ref, w_ref, o_ref):
    o_ref[...] = jax.nn.gelu(_dot(h_ref[...], w_ref[...]), approximate=True).astype(o_ref.dtype)


def _proj_sigmoid_kernel(h_ref, w_ref, o_ref):
    acc = _dot(h_ref[...], w_ref[...])
    o_ref[...] = jax.nn.sigmoid(acc).astype(o_ref.dtype)


def _proj_t_kernel(h_ref, w_ref, o_ref):
    acc = _dot(h_ref[...], w_ref[...])
    for g in range(acc.shape[1] // HEAD_DIM):
        o_ref[g * HEAD_DIM:(g + 1) * HEAD_DIM, :] = acc[:, g * HEAD_DIM:(g + 1) * HEAD_DIM].T.astype(o_ref.dtype)


def _proj_rope_kernel(h_ref, w_ref, cos_ref, sin_ref, o_ref, *, scale, transpose_out):
    acc = _dot(h_ref[...], w_ref[...])
    cos = cos_ref[...]
    sin = sin_ref[...]
    for g in range(acc.shape[1] // HEAD_DIM):
        t = acc[:, g * HEAD_DIM:(g + 1) * HEAD_DIM]
        r = (t * cos + pltpu.roll(t, HEAD_DIM // 2, axis=1) * sin) * scale
        if transpose_out:
            o_ref[g * HEAD_DIM:(g + 1) * HEAD_DIM, :] = r.T.astype(o_ref.dtype)
        else:
            o_ref[:, g * HEAD_DIM:(g + 1) * HEAD_DIM] = r.astype(o_ref.dtype)


def _projection(kernel_fn, h, w, col_off, n_cols, out_dtype, extra=(), transpose_out=False, name="proj"):
    n, k = h.shape
    tm = min(1024, n)
    tn = 1024
    off = col_off // tn
    extra_specs = [pl.BlockSpec((tm, e.shape[1]), lambda j, i: (i, 0)) for e in extra]
    if transpose_out:
        out_spec = pl.BlockSpec((tn, tm), lambda j, i: (j, i))
        out_shape = jax.ShapeDtypeStruct((n_cols, n), out_dtype)
    else:
        out_spec = pl.BlockSpec((tm, tn), lambda j, i: (i, j))
        out_shape = jax.ShapeDtypeStruct((n, n_cols), out_dtype)
    return pl.pallas_call(
        kernel_fn,
        grid=(n_cols // tn, n // tm),
        in_specs=[pl.BlockSpec((tm, k), lambda j, i: (i, 0)),
                  pl.BlockSpec((k, tn), lambda j, i: (0, j + off))] + extra_specs,
        out_specs=out_spec,
        out_shape=out_shape,
        compiler_params=_cparams(2),
        name=name,
    )(h, w, *extra)


def _lru_kernel(pos_ref, u_ref, gg_ref, cw_ref, cb_ref, wrg_ref, ba_ref, bx_ref, lp_ref, o_ref,
                ubuf, hcar, a_s, b_s, h_s):
    t_rows, cb = u_ref.shape
    n_grp = t_rows // 8

    @pl.when(pl.program_id(2) == 0)
    def _():
        ubuf[0:8, :] = jnp.zeros((8, cb), _F32)
        hcar[...] = jnp.zeros((8, cb), _F32)

    ubuf[8:8 + t_rows, :] = u_ref[...]
    cw = cw_ref[...]
    uc = cb_ref[...] + cw[0:1] * ubuf[5:5 + t_rows, :]
    uc = uc + cw[1:2] * ubuf[6:6 + t_rows, :]
    uc = uc + cw[2:3] * ubuf[7:7 + t_rows, :]
    uc = uc + cw[3:4] * ubuf[8:8 + t_rows, :]
    ubuf[0:8, :] = ubuf[t_rows:t_rows + 8, :]

    ucb = uc.astype(_BF16)
    r_parts, i_parts = [], []
    for j in range(cb // LRU_BLOCK):
        g = jnp.dot(ucb[:, j * LRU_BLOCK:(j + 1) * LRU_BLOCK], wrg_ref[j], preferred_element_type=_F32)
        r_parts.append(g[:, :LRU_BLOCK])
        i_parts.append(g[:, LRU_BLOCK:])
    r = jax.nn.sigmoid(jnp.concatenate(r_parts, axis=1) + ba_ref[...])
    gi = jax.nn.sigmoid(jnp.concatenate(i_parts, axis=1) + bx_ref[...])

    log_a = (-C_LRU) * r * jax.nn.softplus(-lp_ref[...])
    a_raw = jnp.exp(log_a)
    m2 = 1.0 - a_raw * a_raw
    mult = jnp.where(m2 > 0.0, m2 * lax.rsqrt(m2), 0.0)
    reset = pos_ref[...] == 0
    a = jnp.where(reset, 0.0, a_raw)
    mult = jnp.where(reset, 1.0, mult)
    bv = uc * gi * mult

    a3 = a.reshape(n_grp, 8, cb)
    b3 = bv.reshape(n_grp, 8, cb)
    row = lax.broadcasted_iota(jnp.int32, (n_grp, 8, cb), 1)
    for d in (1, 2, 4):
        a_sh = pltpu.roll(a3, d, axis=1)
        b_sh = pltpu.roll(b3, d, axis=1)
        take = row >= d
        b3 = jnp.where(take, a3 * b_sh + b3, b3)
        a3 = jnp.where(take, a3 * a_sh, a3)
    a_s[...] = a3
    b_s[...] = b3

    def carry_step(g, carry):
        h = b_s[g] + a_s[g] * carry
        h_s[g] = h
        return jnp.broadcast_to(h[7:8, :], (8, cb))

    hcar[...] = lax.fori_loop(0, n_grp, carry_step, hcar[...], unroll=8)

    hr = h_s[...].reshape(t_rows, cb)
    o_ref[...] = (hr * gg_ref[...]).astype(o_ref.dtype)


def _lru(pos_col, u, gg, conv_w, conv_b, w_rg, b_a, b_x, lru_param, batch, seq):
    cb = 512
    t_rows = min(512, seq)
    n_cb = D_RNN // cb
    n_t = seq // t_rows
    row_map = lambda b, c, s: (b * n_t + s, 0)
    return pl.pallas_call(
        _lru_kernel,
        grid=(batch, n_cb, n_t),
        in_specs=[pl.BlockSpec((t_rows, 1), row_map),
                  pl.BlockSpec((t_rows, cb), lambda b, c, s: (b * n_t + s, c)),
                  pl.BlockSpec((t_rows, cb), lambda b, c, s: (b * n_t + s, c)),
                  pl.BlockSpec((CONV_W, cb), lambda b, c, s: (0, c)),
                  pl.BlockSpec((1, cb), lambda b, c, s: (0, c)),
                  pl.BlockSpec((cb // LRU_BLOCK, LRU_BLOCK, 2 * LRU_BLOCK), lambda b, c, s: (c, 0, 0)),
                  pl.BlockSpec((1, cb), lambda b, c, s: (0, c)),
                  pl.BlockSpec((1, cb), lambda b, c, s: (0, c)),
                  pl.BlockSpec((1, cb), lambda b, c, s: (0, c))],
        out_specs=pl.BlockSpec((t_rows, cb), lambda b, c, s: (b * n_t + s, c)),
        out_shape=jax.ShapeDtypeStruct((batch * seq, D_RNN), _BF16),
        scratch_shapes=[pltpu.VMEM((t_rows + 8, cb), _F32),
                        pltpu.VMEM((8, cb), _F32),
                        pltpu.VMEM((t_rows // 8, 8, cb), _F32),
                        pltpu.VMEM((t_rows // 8, 8, cb), _F32),
                        pltpu.VMEM((t_rows // 8, 8, cb), _F32)],
        compiler_params=_cparams(3),
        name="conv_rglru",
    )(pos_col, u, gg, conv_w, conv_b, w_rg, b_a, b_x, lru_param)


def _attn_kernel(qt_ref, k_ref, vt_ref, lq1_ref, lk1_ref, lq2_ref, lk2_ref, g_ref, o_ref,
                 m_s, l_s, acc_s, st_s, *, tk, cq):
    tq = qt_ref.shape[1]
    q0 = pl.program_id(2) * tq
    n_chunks = tq // cq
    n_diag = tq // tk

    m_s[...] = jnp.full(m_s.shape, MASK_VALUE, _F32)
    l_s[...] = jnp.zeros(l_s.shape, _F32)
    acc_s[...] = jnp.zeros(acc_s.shape, _F32)

    def chunks(diag):
        return [c for c in range(n_chunks) if diag is None or (c + 1) * cq > diag]

    def scores(col0, slot, diag):
        k_blk = k_ref[pl.ds(col0, tk), :]
        for c in chunks(diag):
            cols = slice(c * cq, (c + 1) * cq)
            for mp in range(2):
                dims = slice(mp * HEAD_DIM, (mp + 1) * HEAD_DIM)
                st_s[slot, mp, :, cols] = jnp.dot(k_blk[:, dims], qt_ref[dims, cols], preferred_element_type=_F32)

    def update(col0, slot, diag):
        vt_blk = vt_ref[:, pl.ds(col0, tk)]
        for c in chunks(diag):
            masked = diag is not None and c * cq < diag + tk - 1
            cols = slice(c * cq, (c + 1) * cq)
            for mp in range(2):
                st = st_s[slot, mp, :, cols]
                if masked:
                    kv_pos = diag + lax.broadcasted_iota(jnp.int32, (tk, cq), 0)
                    q_pos = c * cq + lax.broadcasted_iota(jnp.int32, (tk, cq), 1)
                    st = jnp.where(kv_pos <= q_pos, st, MASK_VALUE)
                m_prev = m_s[mp, :, cols]
                m_new = jnp.maximum(m_prev, jnp.max(st, axis=0, keepdims=True))
                alpha = jnp.exp2(m_prev - m_new)
                pt = jnp.exp2(st - m_new)
                l_s[mp, :, cols] = alpha * l_s[mp, :, cols] + jnp.sum(pt, axis=0, keepdims=True)
                pv = jnp.dot(vt_blk, pt.astype(_BF16), preferred_element_type=_F32)
                acc_s[mp, :, cols] = alpha * acc_s[mp, :, cols] + pv
                m_s[mp, :, cols] = m_new

    n_pairs = q0 // (2 * tk)
    scores(0, 0, None)

    def pair(t, carry):
        c0 = pl.multiple_of(2 * t * tk, tk)
        scores(c0 + tk, 1, None)
        update(c0, 0, None)
        scores(c0 + 2 * tk, 0, None)
        update(c0 + tk, 1, None)
        return carry

    lax.fori_loop(0, n_pairs, pair, 0)
    for d in range(n_diag):
        col0 = pl.multiple_of(q0 + d * tk, tk)
        if d + 1 < n_diag:
            scores(col0 + tk, (d + 1) % 2, (d + 1) * tk)
        update(col0, d % 2, d * tk)

    lam = (jnp.exp(jnp.sum(lq1_ref[...] * lk1_ref[...], axis=1, keepdims=True))
           - jnp.exp(jnp.sum(lq2_ref[...] * lk2_ref[...], axis=1, keepdims=True)) + LAM_INIT)
    ot = acc_s[0] / l_s[0] - lam * (acc_s[1] / l_s[1])
    yt = ot * lax.rsqrt(jnp.mean(ot * ot, axis=0, keepdims=True) + SUBLN_EPS)
    o_ref[...] = ((yt.T * g_ref[...]) * (1.0 - LAM_INIT)).astype(o_ref.dtype)


def _attention(qt, k, vt, lq1, lk1, lq2, lk2, subln_g, batch, seq):
    tq = min(1024, seq)
    tk = min(512, seq)
    cq = min(256, tq)
    nq = seq // tq
    assert tq % (2 * tk) == 0 and seq % tq == 0
    vec = pl.BlockSpec((1, HEAD_DIM), lambda b, h, i: (0, 0))
    return pl.pallas_call(
        functools.partial(_attn_kernel, tk=tk, cq=cq),
        grid=(batch, N_HEADS, nq),
        in_specs=[pl.BlockSpec((V_DIM, tq), lambda b, h, i: (h, b * nq + i)),
                  pl.BlockSpec((seq, V_DIM), lambda b, h, i: (b, h)),
                  pl.BlockSpec((V_DIM, seq), lambda b, h, i: (h, b)),
                  vec, vec, vec, vec,
                  pl.BlockSpec((1, V_DIM), lambda b, h, i: (0, 0))],
        out_specs=pl.BlockSpec((tq, V_DIM), lambda b, h, i: (b * nq + i, h)),
        out_shape=jax.ShapeDtypeStruct((batch * seq, ATTN_W), _BF16),
        scratch_shapes=[pltpu.VMEM((2, 1, tq), _F32),
                        pltpu.VMEM((2, 1, tq), _F32),
                        pltpu.VMEM((2, V_DIM, tq), _F32),
                        pltpu.VMEM((2, 2, tk, tq), _F32)],
        compiler_params=_cparams(3),
        name="diff_attention",
    )(qt, k, vt, lq1, lk1, lq2, lk2, subln_g)


def _merge_kernel(yr_ref, ya_ref, wr_ref, wa_ref, gr_ref, ga_ref, o_ref):
    pr = jnp.dot(yr_ref[...], wr_ref[...], preferred_element_type=_F32)
    pa = jnp.dot(ya_ref[...], wa_ref[...], preferred_element_type=_F32)
    o_ref[...] = (gr_ref[...].astype(_F32) * pr + ga_ref[...].astype(_F32) * pa).astype(o_ref.dtype)


def _merge(y_rnn, y_attn, w_r, w_a, gates):
    n, k = y_rnn.shape
    tm = min(512, n)
    tn = 1024
    n_tn = D_MODEL // tn
    return pl.pallas_call(
        _merge_kernel,
        grid=(n_tn, n // tm),
        in_specs=[pl.BlockSpec((tm, k), lambda j, i: (i, 0)),
                  pl.BlockSpec((tm, k), lambda j, i: (i, 0)),
                  pl.BlockSpec((k, tn), lambda j, i: (0, j)),
                  pl.BlockSpec((k, tn), lambda j, i: (0, j)),
                  pl.BlockSpec((tm, tn), lambda j, i: (i, j)),
                  pl.BlockSpec((tm, tn), lambda j, i: (i, n_tn + j))],
        out_specs=pl.BlockSpec((tm, tn), lambda j, i: (i, j)),
        out_shape=jax.ShapeDtypeStruct((n, D_MODEL), _BF16),
        compiler_params=_cparams(2),
        name="branch_merge",
    )(y_rnn, y_attn, w_r, w_a, gates, gates)


def _outproj_kernel(m_ref, w_ref, x_ref, g_ref, rhi_ref, rlo_ref, x2_ref, h2_ref, lt_ref):
    x2 = x_ref[...] + _dot(m_ref[...], w_ref[...])
    x2_ref[...] = x2
    h2 = (x2 * lax.rsqrt(jnp.mean(x2 * x2, axis=-1, keepdims=True) + NORM_EPS)) * g_ref[...]
    _store_token_tiles(h2_ref, _pack_halves(h2))
    h_hi = h2.astype(_BF16)
    h_lo = (h2 - h_hi.astype(_F32)).astype(_BF16)
    logits = _dot(h_hi, rhi_ref[...]) + (_dot(h_lo, rhi_ref[...]) + _dot(h_hi, rlo_ref[...]))
    lt_ref[...] = logits.T[:ROUTER_ROWS]


def _outproj(merged, w_out, x, g_ffn, w_router):
    n, d = x.shape
    tm = min(256, n)
    r_hi = w_router.astype(_BF16)
    r_lo = (w_router - r_hi.astype(_F32)).astype(_BF16)
    return pl.pallas_call(
        _outproj_kernel,
        grid=(n // tm,),
        in_specs=[pl.BlockSpec((tm, d), lambda i: (i, 0)),
                  pl.BlockSpec((d, d), lambda i: (0, 0)),
                  pl.BlockSpec((tm, d), lambda i: (i, 0)),
                  pl.BlockSpec((1, d), lambda i: (0, 0)),
                  pl.BlockSpec((d, 128), lambda i: (0, 0)),
                  pl.BlockSpec((d, 128), lambda i: (0, 0))],
        out_specs=[pl.BlockSpec((tm, d), lambda i: (i, 0)),
                   pl.BlockSpec((tm * TILE_ROWS, 128), lambda i: (i, 0)),
                   pl.BlockSpec((ROUTER_ROWS, tm), lambda i: (0, i))],
        out_shape=[jax.ShapeDtypeStruct((n, d), _F32),
                   jax.ShapeDtypeStruct((n * TILE_ROWS, 128), jnp.uint32),
                   jax.ShapeDtypeStruct((ROUTER_ROWS, n), _F32)],
        compiler_params=_cparams(1),
        name="outproj_norm_router",
    )(merged, w_out, x, g_ffn, r_hi, r_lo)


def _first_index_of_max(vals, n_rows):
    vmax = jnp.max(vals, axis=0, keepdims=True)
    rows = lax.broadcasted_iota(jnp.int32, vals.shape, 0).astype(_F32)
    idx = jnp.min(jnp.where(vals == vmax, rows, float(n_rows)), axis=0, keepdims=True)
    return vmax, idx


def _route_kernel(lt_ref, tri_ref, eid_ref, wts_ref, rank_ref, cnt_ref, carry):
    tl = lt_ref.shape[1]

    @pl.when(pl.program_id(0) == 0)
    def _():
        carry[...] = jnp.zeros(carry.shape, _F32)

    lt = lt_ref[...]
    g = lt[0:N_GROUPS]
    ge = jnp.exp(g - jnp.max(g, axis=0, keepdims=True))
    gp = ge / jnp.sum(ge, axis=0, keepdims=True)
    g_w, g_idx = _first_index_of_max(gp, N_GROUPS)
    el = jnp.zeros((EXPERTS_PER_GROUP, tl), _F32)
    for gg in range(N_GROUPS):
        el = jnp.where(g_idx == float(gg), lt[8 + gg * EXPERTS_PER_GROUP:8 + (gg + 1) * EXPERTS_PER_GROUP], el)
    ee = jnp.exp(el - jnp.max(el, axis=0, keepdims=True))
    ep = ee / jnp.sum(ee, axis=0, keepdims=True)
    v1, i1 = _first_index_of_max(ep, EXPERTS_PER_GROUP)
    rows8 = lax.broadcasted_iota(jnp.int32, ep.shape, 0).astype(_F32)
    v2, i2 = _first_index_of_max(jnp.where(rows8 == i1, -1.0, ep), EXPERTS_PER_GROUP)
    den = v1 + v2
    e1 = g_idx * float(EXPERTS_PER_GROUP) + i1
    e2 = g_idx * float(EXPERTS_PER_GROUP) + i2

    rows_e = lax.broadcasted_iota(jnp.int32, (N_EXPERTS, tl), 0).astype(_F32)
    ranks = []
    for e_sel in (e1, e2):
        onehot = rows_e == e_sel
        oh = jnp.where(onehot, 1.0, 0.0)
        before = jnp.dot(oh.astype(_BF16), tri_ref[...], preferred_element_type=_F32) + carry[...]
        ranks.append(jnp.sum(jnp.where(onehot, before, 0.0), axis=0, keepdims=True))
        carry[...] = carry[...] + jnp.sum(oh, axis=1, keepdims=True)

    zeros6 = jnp.zeros((6, tl), _F32)
    eid_ref[...] = jnp.concatenate([e1, e2, zeros6], axis=0).astype(jnp.int32)
    wts_ref[...] = jnp.concatenate([g_w * (v1 / den), g_w * (v2 / den), zeros6], axis=0)
    rank_ref[...] = jnp.concatenate(ranks + [zeros6], axis=0).astype(jnp.int32)
    cnt_ref[...] = jnp.broadcast_to(carry[...], cnt_ref.shape).astype(jnp.int32)


def _route(logits_t):
    n = logits_t.shape[1]
    tl = min(512, n)
    tri = jnp.asarray(np.triu(np.ones((tl, tl), np.float32), k=1), _BF16)
    row_block = pl.BlockSpec((8, tl), lambda i: (0, i))
    return pl.pallas_call(
        _route_kernel,
        grid=(n // tl,),
        in_specs=[pl.BlockSpec((ROUTER_ROWS, tl), lambda i: (0, i)),
                  pl.BlockSpec((tl, tl), lambda i: (0, 0))],
        out_specs=[row_block, row_block, row_block,
                   pl.BlockSpec((N_EXPERTS, 128), lambda i: (0, 0))],
        out_shape=[jax.ShapeDtypeStruct((8, n), jnp.int32),
                   jax.ShapeDtypeStruct((8, n), _F32),
                   jax.ShapeDtypeStruct((8, n), jnp.int32),
                   jax.ShapeDtypeStruct((N_EXPERTS, 128), jnp.int32)],
        scratch_shapes=[pltpu.VMEM((N_EXPERTS, 1), _F32)],
        compiler_params=_cparams(1),
        name="route_topk_rank",
    )(logits_t, tri)


def _row_copy(src_hbm, row, dst_vmem, r, sem):
    return pltpu.make_async_copy(src_hbm.at[_token_rows(row)], dst_vmem.at[_token_rows(r)], sem)


def _dispatch_kernel(dest_ref, cnt_ref, pstart_ref, pend_ref, h_ref, xs_hbm, rows, zero_rows, sem, zsem, *, n_steps):
    tm = h_ref.shape[0] // TILE_ROWS
    i = pl.program_id(0)
    n = n_steps * tm
    n_slots = xs_hbm.shape[0] // TILE_ROWS
    slot = i % 2
    n_jobs = 2 * N_EXPERTS
    jobs_per_step = -(-n_jobs // n_steps)

    def pad_job(j, wait):
        e = jnp.minimum(j, N_EXPERTS - 1)
        tail_lo = pend_ref[N_EXPERTS - 1] + (j - N_EXPERTS) * SLOT_BLOCK
        lo = jnp.where(j < N_EXPERTS, pstart_ref[e] + cnt_ref[e], tail_lo)
        hi = jnp.where(j < N_EXPERTS, pend_ref[e], jnp.minimum(tail_lo + SLOT_BLOCK, n_slots))

        def body(r, c):
            cp = pltpu.make_async_copy(zero_rows, xs_hbm.at[_token_rows(r)], zsem)
            cp.wait() if wait else cp.start()
            return c
        lax.fori_loop(lo, jnp.maximum(lo, hi), body, 0)

    def pad_jobs(wait):
        for jj in range(jobs_per_step):
            j = i * jobs_per_step + jj

            @pl.when(j < n_jobs)
            def _():
                pad_job(j, wait)

    @pl.when(i == 0)
    def _():
        zero_rows[...] = jnp.zeros(zero_rows.shape, zero_rows.dtype)

    pad_jobs(False)

    def row_copy(tile_slot, r, row):
        return pltpu.make_async_copy(rows.at[tile_slot, _token_rows(r)], xs_hbm.at[_token_rows(row)], sem.at[tile_slot])

    def drain(tile_slot):
        def body(r, c):
            for _ in range(TOP_K):
                row_copy(tile_slot, r, 0).wait()
            return c
        lax.fori_loop(0, tm, body, 0, unroll=8)

    @pl.when(i >= 2)
    def _():
        drain(slot)

    rows[slot] = h_ref[...]

    def issue(r, c):
        for k in range(TOP_K):
            row_copy(slot, r, dest_ref[k * n + i * tm + r]).start()
        return c

    lax.fori_loop(0, tm, issue, 0, unroll=8)
    pad_jobs(True)

    @pl.when(i == n_steps - 1)
    def _():
        if n_steps >= 2:
            drain(1 - slot)
        drain(slot)


def _dispatch(dest, counts, pstarts, pends, h2, n_slots):
    n = h2.shape[0] // TILE_ROWS
    tm = min(256, n)
    return pl.pallas_call(
        functools.partial(_dispatch_kernel, n_steps=n // tm),
        grid_spec=pltpu.PrefetchScalarGridSpec(
            num_scalar_prefetch=4,
            grid=(n // tm,),
            in_specs=[pl.BlockSpec((tm * TILE_ROWS, 128), lambda i, *_: (i, 0))],
            out_specs=pl.BlockSpec(memory_space=pl.ANY),
            scratch_shapes=[pltpu.VMEM((2, tm * TILE_ROWS, 128), h2.dtype), pltpu.VMEM((TILE_ROWS, 128), h2.dtype),
                            pltpu.SemaphoreType.DMA((2,)), pltpu.SemaphoreType.DMA]),
        out_shape=jax.ShapeDtypeStruct((n_slots * TILE_ROWS, 128), h2.dtype),
        compiler_params=_cparams(1),
        name="slot_dispatch",
    )(dest, counts, pstarts, pends, h2)


def _expert_kernel(be_ref, first_ref, slot_ref, nxt_ref, x_ref, wg_hbm, wu_hbm, wd_hbm, o_ref,
                   wg_buf, wu_buf, wd_buf, sem):
    i = pl.program_id(0)

    def weight_copies(e, s):
        return (pltpu.make_async_copy(wg_hbm.at[e], wg_buf.at[s], sem.at[s, 0]),
                pltpu.make_async_copy(wu_hbm.at[e], wu_buf.at[s], sem.at[s, 1]),
                pltpu.make_async_copy(wd_hbm.at[e], wd_buf.at[s], sem.at[s, 2]))

    @pl.when(i == 0)
    def _():
        for cp in weight_copies(be_ref[0], 0):
            cp.start()

    s = slot_ref[i]

    @pl.when(first_ref[i] == 1)
    def _():
        for cp in weight_copies(be_ref[i], s):
            cp.wait()

        @pl.when(nxt_ref[i] >= 0)
        def _():
            for cp in weight_copies(nxt_ref[i], 1 - s):
                cp.start()

    x_lo, x_hi = _unpack_halves(_load_token_tiles(x_ref))
    g = _dot(x_lo, wg_buf[s, :HALF]) + _dot(x_hi, wg_buf[s, HALF:])
    u = _dot(x_lo, wu_buf[s, :HALF]) + _dot(x_hi, wu_buf[s, HALF:])
    mid = (jax.nn.silu(g) * u).astype(_BF16)
    _store_token_tiles(o_ref, _pack_halves(_dot(mid, wd_buf[s])))


def _experts(blk_e, xs, w_gate, w_up, w_down):
    p = xs.shape[0] // TILE_ROWS
    d = w_gate.shape[1]
    nb = p // SLOT_BLOCK
    first = jnp.concatenate([jnp.ones((1,), jnp.int32), (blk_e[1:] != blk_e[:-1]).astype(jnp.int32)])
    slot = (jnp.cumsum(first) - 1) % 2
    later = blk_e[None, :] > blk_e[:, None]
    nxt = jnp.min(jnp.where(later, blk_e[None, :], N_EXPERTS), axis=1)
    nxt = jnp.where(nxt < N_EXPERTS, nxt, -1).astype(jnp.int32)
    any_spec = pl.BlockSpec(memory_space=pl.ANY)
    return pl.pallas_call(
        _expert_kernel,
        grid_spec=pltpu.PrefetchScalarGridSpec(
            num_scalar_prefetch=4,
            grid=(nb,),
            in_specs=[pl.BlockSpec((SLOT_BLOCK * TILE_ROWS, 128), lambda i, *_: (i, 0)), any_spec, any_spec, any_spec],
            out_specs=pl.BlockSpec((SLOT_BLOCK * TILE_ROWS, 128), lambda i, *_: (i, 0)),
            scratch_shapes=[pltpu.VMEM((2, d, D_EXPERT), _F32),
                            pltpu.VMEM((2, d, D_EXPERT), _F32),
                            pltpu.VMEM((2, D_EXPERT, d), _F32),
                            pltpu.SemaphoreType.DMA((2, 3))]),
        out_shape=jax.ShapeDtypeStruct((p * TILE_ROWS, 128), jnp.uint32),
        compiler_params=pltpu.CompilerParams(dimension_semantics=("arbitrary",),
                                             vmem_limit_bytes=EXPERT_VMEM_LIMIT),
        name="expert_mlp",
    )(blk_e, first, slot.astype(jnp.int32), nxt, xs, w_gate, w_up, w_down)


def _combine_kernel(dest_ref, x2_ref, w0_ref, w1_ref, g_ref, ys_hbm, o_ref, rows, sem):
    tm = x2_ref.shape[0]
    n = pl.num_programs(0) * tm
    i = pl.program_id(0)

    def issue(tile, slot):
        def body(r, c):
            for k in range(TOP_K):
                _row_copy(ys_hbm, dest_ref[k * n + tile * tm + r], rows.at[slot, k], r, sem.at[slot]).start()
            return c
        lax.fori_loop(0, tm, body, 0, unroll=8)

    @pl.when(i == 0)
    def _():
        issue(0, 0)

    @pl.when(i + 1 < pl.num_programs(0))
    def _():
        issue(i + 1, (i + 1) % 2)

    slot = i % 2

    def drain(r, c):
        for k in range(TOP_K):
            _row_copy(ys_hbm, 0, rows.at[slot, k], r, sem.at[slot]).wait()
        return c

    lax.fori_loop(0, tm, drain, 0, unroll=8)
    y0 = _unpack_halves(_load_token_tiles(rows.at[slot, 0]))
    y1 = _unpack_halves(_load_token_tiles(rows.at[slot, 1]))
    w0 = w0_ref[...]
    w1 = w1_ref[...]
    halves = (slice(0, HALF), slice(HALF, 2 * HALF))
    x3 = [x2_ref[:, cols] + (w0 * a + w1 * b) for cols, a, b in zip(halves, y0, y1)]
    ms = (jnp.sum(x3[0] * x3[0], axis=-1, keepdims=True) + jnp.sum(x3[1] * x3[1], axis=-1, keepdims=True)) / (2 * HALF)
    inv = lax.rsqrt(ms + NORM_EPS)
    for cols, v in zip(halves, x3):
        o_ref[:, cols] = (v * inv) * g_ref[:, cols]


def _combine(dest, x2, w0, w1, g_final, ys):
    n, d = x2.shape
    tm = min(256, n)
    return pl.pallas_call(
        _combine_kernel,
        grid_spec=pltpu.PrefetchScalarGridSpec(
            num_scalar_prefetch=1,
            grid=(n // tm,),
            in_specs=[pl.BlockSpec((tm, d), lambda i, dst: (i, 0)),
                      pl.BlockSpec((tm, 1), lambda i, dst: (i, 0)),
                      pl.BlockSpec((tm, 1), lambda i, dst: (i, 0)),
                      pl.BlockSpec((1, d), lambda i, dst: (0, 0)),
                      pl.BlockSpec(memory_space=pl.ANY)],
            out_specs=pl.BlockSpec((tm, d), lambda i, dst: (i, 0)),
            scratch_shapes=[pltpu.VMEM((2, TOP_K, tm * TILE_ROWS, 128), jnp.uint32), pltpu.SemaphoreType.DMA((2,))]),
        out_shape=jax.ShapeDtypeStruct((n, d), _F32),
        compiler_params=_cparams(1),
        name="combine_final_norm",
    )(dest, x2, w0, w1, g_final, ys)


def kernel(x, positions, g_mix, w_in, conv_w, conv_b, w_rg_a, b_rg_a, w_rg_x, b_rg_x, lru_param, lambda_q1, lambda_k1, lambda_q2, lambda_k2, subln_g, w_br_rnn, w_br_attn, w_out, g_ffn, w_grp_router, w_exp_router, w_gate, w_up, w_down, g_final):
    batch, seq, d = x.shape
    n = batch * seq
    xf = x.reshape(n, d)
    pos_col = positions.reshape(n, 1)
    row = lambda v: v.reshape(1, -1)

    w_rg = jnp.concatenate([w_rg_a[0], w_rg_x[0]], axis=-1).astype(_BF16)
    w_r_b = w_br_rnn[0].astype(_BF16)
    w_a_b = w_br_attn[0].astype(_BF16)
    w_out_b = w_out[0].astype(_BF16)
    w_router = jnp.concatenate(
        [w_grp_router[0], jnp.zeros((d, 8 - N_GROUPS), _F32), w_exp_router[0],
         jnp.zeros((d, 128 - ROUTER_ROWS), _F32)], axis=1)

    h = _rmsnorm(xf, row(g_mix[0]), _BF16)
    cos_t, sin_t = _rope_tables(pos_col)

    c_qk = 2 * D_RNN
    c_v = c_qk + 2 * QK_W
    c_g = c_v + ATTN_W
    u = _projection(_proj_plain_kernel, h, w_in[0], 0, D_RNN, _F32, name="proj_rnn")
    gg = _projection(_proj_gelu_kernel, h, w_in[0], D_RNN, D_RNN, _F32, name="proj_rnn_gate")
    q_scale = (HEAD_DIM ** -0.5) * LOG2E
    qt = _projection(functools.partial(_proj_rope_kernel, scale=q_scale, transpose_out=True),
                     h, w_in[0], c_qk, QK_W, _BF16, extra=(cos_t, sin_t), transpose_out=True, name="proj_q_rope")
    k = _projection(functools.partial(_proj_rope_kernel, scale=1.0, transpose_out=False),
                    h, w_in[0], c_qk + QK_W, QK_W, _BF16, extra=(cos_t, sin_t), name="proj_k_rope")
    vt = _projection(_proj_t_kernel, h, w_in[0], c_v, ATTN_W, _BF16, transpose_out=True, name="proj_v")
    gates = _projection(_proj_sigmoid_kernel, h, w_in[0], c_g, 2 * D_MODEL, _BF16, name="proj_gates")

    y_rnn = _lru(pos_col, u, gg, conv_w[0], row(conv_b[0]), w_rg, row(b_rg_a[0]), row(b_rg_x[0]),
                 row(lru_param[0]), batch, seq)
    y_attn = _attention(qt, k, vt, row(lambda_q1[0]), row(lambda_k1[0]), row(lambda_q2[0]), row(lambda_k2[0]),
                        row(subln_g[0]), batch, seq)
    merged = _merge(y_rnn, y_attn, w_r_b, w_a_b, gates)
    x2, h2, logits_t = _outproj(merged, w_out_b, xf, row(g_ffn[0]), w_router)

    eid8, wts8, rank8, cnt = _route(logits_t)
    eid, rank, counts = eid8[:TOP_K], rank8[:TOP_K], cnt[:, 0]

    pcounts = ((counts + SLOT_BLOCK - 1) // SLOT_BLOCK) * SLOT_BLOCK
    pends = jnp.cumsum(pcounts)
    pstarts = pends - pcounts
    experts = jnp.arange(N_EXPERTS, dtype=jnp.int32)
    dest = jnp.sum(jnp.where(eid[:, :, None] == experts, pstarts, 0), axis=-1) + rank
    n_slots = n * TOP_K + N_EXPERTS * SLOT_BLOCK
    blk_start = jnp.arange(n_slots // SLOT_BLOCK, dtype=jnp.int32) * SLOT_BLOCK
    blk_e = jnp.minimum(jnp.sum((blk_start[:, None] >= pends[None, :]).astype(jnp.int32), axis=1), N_EXPERTS - 1)

    dest_flat = dest.reshape(-1)
    xs = _dispatch(dest_flat, counts, pstarts, pends, h2, n_slots)
    ys = _experts(blk_e, xs, w_gate[0], w_up[0], w_down[0])
    out = _combine(dest_flat, x2, wts8[0].reshape(n, 1), wts8[1].reshape(n, 1), row(g_final), ys)
    return out.reshape(batch, seq, d)
```

```python
import functools
import math

import numpy as np
import jax
import jax.numpy as jnp
from jax import lax
from jax.experimental import pallas as pl
from jax.experimental.pallas import tpu as pltpu

D_MODEL = 2048
D_RNN = D_MODEL
LRU_BLOCK = 128
N_LRU_BLOCKS = D_RNN // LRU_BLOCK
CONV_W = 4
C_LRU = 8.0
HEAD_DIM = 128
N_HEADS = D_MODEL // (2 * HEAD_DIM)
V_DIM = 2 * HEAD_DIM
QK_W = N_HEADS * 2 * HEAD_DIM
ATTN_W = N_HEADS * V_DIM
ROPE_THETA = 10000.0
SUBLN_EPS = 1e-5
N_GROUPS = 4
EXPERTS_PER_GROUP = 8
N_EXPERTS = N_GROUPS * EXPERTS_PER_GROUP
TOP_K = 2
D_EXPERT = D_MODEL // 2
NORM_EPS = 1e-6
LAM_INIT = 0.8 - 0.6 * math.exp(-0.3 * 0)

SLOT_BLOCK = 256
HALF = D_MODEL // 2
TILE_ROWS = HALF // 128
ROUTER_ROWS = 8 + N_EXPERTS
MASK_VALUE = -1e30
LOG2E = 1.4426950408889634
VMEM_LIMIT = 56 * 1024 * 1024
EXPERT_VMEM_LIMIT = 60 * 1024 * 1024

_F32 = jnp.float32
_BF16 = jnp.bfloat16


def _dot(a, b):
    return lax.dot_general(a, b, (((1,), (0,)), ((), ())), preferred_element_type=_F32)


def _pack_halves(y):
    bits = lax.bitcast_convert_type(y.astype(_BF16).astype(_F32), jnp.uint32)
    return (bits[:, :HALF] >> 16) | bits[:, HALF:]


def _unpack_halves(p):
    lo = lax.bitcast_convert_type(p << 16, _F32)
    hi = lax.bitcast_convert_type(p & jnp.uint32(0xFFFF0000), _F32)
    return lo, hi


def _token_rows(t):
    return pl.ds(pl.multiple_of(t * TILE_ROWS, TILE_ROWS), TILE_ROWS)


def _store_token_tiles(ref, packed):
    tokens = packed.shape[0]
    for j in range(TILE_ROWS):
        ref[pl.ds(j, tokens, stride=TILE_ROWS), :] = packed[:, j * 128:(j + 1) * 128]


def _load_token_tiles(ref):
    tokens = ref.shape[0] // TILE_ROWS
    return jnp.concatenate([ref[pl.ds(j, tokens, stride=TILE_ROWS), :] for j in range(TILE_ROWS)], axis=1)


def _cparams(n_axes):
    return pltpu.CompilerParams(dimension_semantics=("arbitrary",) * n_axes,
                                vmem_limit_bytes=VMEM_LIMIT)


def _rmsnorm_kernel(x_ref, g_ref, o_ref):
    x = x_ref[...]
    y = x * lax.rsqrt(jnp.mean(x * x, axis=-1, keepdims=True) + NORM_EPS)
    o_ref[...] = (y * g_ref[...]).astype(o_ref.dtype)


def _rmsnorm(x, g, out_dtype):
    n, d = x.shape
    tm = min(512, n)
    return pl.pallas_call(
        _rmsnorm_kernel,
        grid=(n // tm,),
        in_specs=[pl.BlockSpec((tm, d), lambda i: (i, 0)),
                  pl.BlockSpec((1, d), lambda i: (0, 0))],
        out_specs=pl.BlockSpec((tm, d), lambda i: (i, 0)),
        out_shape=jax.ShapeDtypeStruct((n, d), out_dtype),
        compiler_params=_cparams(1),
        name="rmsnorm",
    )(x, g)


def _rope_table_kernel(pos_ref, inv_ref, sign_ref, cos_ref, sin_ref):
    ang = pos_ref[...].astype(_F32) * inv_ref[...]
    cos_ref[...] = jnp.cos(ang)
    sin_ref[...] = jnp.sin(ang) * sign_ref[...]


def _rope_tables(pos_col):
    n = pos_col.shape[0]
    tm = min(1024, n)
    inv = 1.0 / (ROPE_THETA ** (np.arange(0, HEAD_DIM, 2, dtype=np.float32) / np.float32(HEAD_DIM)))
    inv = np.concatenate([inv, inv]).astype(np.float32)[None, :]
    sign = np.concatenate([-np.ones(HEAD_DIM // 2), np.ones(HEAD_DIM // 2)]).astype(np.float32)[None, :]
    return pl.pallas_call(
        _rope_table_kernel,
        grid=(n // tm,),
        in_specs=[pl.BlockSpec((tm, 1), lambda i: (i, 0)),
                  pl.BlockSpec((1, HEAD_DIM), lambda i: (0, 0)),
                  pl.BlockSpec((1, HEAD_DIM), lambda i: (0, 0))],
        out_specs=[pl.BlockSpec((tm, HEAD_DIM), lambda i: (i, 0))] * 2,
        out_shape=[jax.ShapeDtypeStruct((n, HEAD_DIM), _F32)] * 2,
        compiler_params=_cparams(1),
        name="rope_tables",
    )(pos_col, jnp.asarray(inv), jnp.asarray(sign))


def _proj_plain_kernel(h_ref, w_ref, o_ref):
    o_ref[...] = _dot(h_ref[...], w_ref[...]).astype(o_ref.dtype)


def _proj_gelu_kernel(h_ref, w_ref, o_ref):
    o_ref[...] = jax.nn.gelu(_dot(h_ref[...], w_ref[...]), approximate=True).astype(o_ref.dtype)


def _proj_sigmoid_kernel(h_ref, w_ref, o_ref):
    acc = _dot(h_ref[...], w_ref[...])
    o_ref[...] = jax.nn.sigmoid(acc).astype(o_ref.dtype)


def _proj_t_kernel(h_ref, w_ref, o_ref):
    acc = _dot(h_ref[...], w_ref[...])
    for g in range(acc.shape[1] // HEAD_DIM):
        o_ref[g * HEAD_DIM:(g + 1) * HEAD_DIM, :] = acc[:, g * HEAD_DIM:(g + 1) * HEAD_DIM].T.astype(o_ref.dtype)


def _proj_rope_kernel(h_ref, w_ref, cos_ref, sin_ref, o_ref, *, scale, transpose_out):
    acc = _dot(h_ref[...], w_ref[...])
    cos = cos_ref[...]
    sin = sin_ref[...]
    for g in range(acc.shape[1] // HEAD_DIM):
        t = acc[:, g * HEAD_DIM:(g + 1) * HEAD_DIM]
        r = (t * cos + pltpu.roll(t, HEAD_DIM // 2, axis=1) * sin) * scale
        if transpose_out:
            o_ref[g * HEAD_DIM:(g + 1) * HEAD_DIM, :] = r.T.astype(o_ref.dtype)
        else:
            o_ref[:, g * HEAD_DIM:(g + 1) * HEAD_DIM] = r.astype(o_ref.dtype)


def _projection(kernel_fn, h, w, col_off, n_cols, out_dtype, extra=(), transpose_out=False, name="proj"):
    n, k = h.shape
    tm = min(1024, n)
    tn = 1024
    off = col_off // tn
    extra_specs = [pl.BlockSpec((tm, e.shape[1]), lambda j, i: (i, 0)) for e in extra]
    if transpose_out:
        out_spec = pl.BlockSpec((tn, tm), lambda j, i: (j, i))
        out_shape = jax.ShapeDtypeStruct((n_cols, n), out_dtype)
    else:
        out_spec = pl.BlockSpec((tm, tn), lambda j, i: (i, j))
        out_shape = jax.ShapeDtypeStruct((n, n_cols), out_dtype)
    return pl.pallas_call(
        kernel_fn,
        grid=(n_cols // tn, n // tm),
        in_specs=[pl.BlockSpec((tm, k), lambda j, i: (i, 0)),
                  pl.BlockSpec((k, tn), lambda j, i: (0, j + off))] + extra_specs,
        out_specs=out_spec,
        out_shape=out_shape,
        compiler_params=_cparams(2),
        name=name,
    )(h, w, *extra)


def _lru_kernel(pos_ref, u_ref, gg_ref, cw_ref, cb_ref, wrg_ref, ba_ref, bx_ref, lp_ref, o_ref,
                ubuf, hcar, a_s, b_s, h_s):
    t_rows, cb = u_ref.shape
    n_grp = t_rows // 8

    @pl.when(pl.program_id(2) == 0)
    def _():
        ubuf[0:8, :] = jnp.zeros((8, cb), _F32)
        hcar[...] = jnp.zeros((8, cb), _F32)

    ubuf[8:8 + t_rows, :] = u_ref[...]
    cw = cw_ref[...]
    uc = cb_ref[...] + cw[0:1] * ubuf[5:5 + t_rows, :]
    uc = uc + cw[1:2] * ubuf[6:6 + t_rows, :]
    uc = uc + cw[2:3] * ubuf[7:7 + t_rows, :]
    uc = uc + cw[3:4] * ubuf[8:8 + t_rows, :]
    ubuf[0:8, :] = ubuf[t_rows:t_rows + 8, :]

    ucb = uc.astype(_BF16)
    r_parts, i_parts = [], []
    for j in range(cb // LRU_BLOCK):
        g = jnp.dot(ucb[:, j * LRU_BLOCK:(j + 1) * LRU_BLOCK], wrg_ref[j], preferred_element_type=_F32)
        r_parts.append(g[:, :LRU_BLOCK])
        i_parts.append(g[:, LRU_BLOCK:])
    r = 0.5 * jnp.tanh(0.5 * (jnp.concatenate(r_parts, axis=1) + ba_ref[...])) + 0.5
    gi = 0.5 * jnp.tanh(0.5 * (jnp.concatenate(i_parts, axis=1) + bx_ref[...])) + 0.5

    log_a = (-C_LRU) * r * jax.nn.softplus(-lp_ref[...])
    a_raw = jnp.exp(log_a)
    m2 = 1.0 - a_raw * a_raw
    mult = jnp.where(m2 > 0.0, m2 * lax.rsqrt(m2), 0.0)
    reset = pos_ref[...] == 0
    a = jnp.where(reset, 0.0, a_raw)
    mult = jnp.where(reset, 1.0, mult)
    bv = uc * gi * mult

    a3 = a.reshape(n_grp, 8, cb)
    b3 = bv.reshape(n_grp, 8, cb)
    row = lax.broadcasted_iota(jnp.int32, (n_grp, 8, cb), 1)
    for d in (1, 2, 4):
        a_sh = pltpu.roll(a3, d, axis=1)
        b_sh = pltpu.roll(b3, d, axis=1)
        take = row >= d
        b3 = jnp.where(take, a3 * b_sh + b3, b3)
        a3 = jnp.where(take, a3 * a_sh, a3)
    a_s[...] = a3
    b_s[...] = b3

    def carry_step(g, carry):
        h = b_s[g] + a_s[g] * carry
        h_s[g] = h
        return jnp.broadcast_to(h[7:8, :], (8, cb))

    hcar[...] = lax.fori_loop(0, n_grp, carry_step, hcar[...], unroll=8)

    hr = h_s[...].reshape(t_rows, cb)
    o_ref[...] = (hr * gg_ref[...]).astype(o_ref.dtype)


def _lru(pos_col, u, gg, conv_w, conv_b, w_rg, b_a, b_x, lru_param, batch, seq):
    cb = 512
    t_rows = min(512, seq)
    n_cb = D_RNN // cb
    n_t = seq // t_rows
    row_map = lambda b, c, s: (b * n_t + s, 0)
    return pl.pallas_call(
        _lru_kernel,
        grid=(batch, n_cb, n_t),
        in_specs=[pl.BlockSpec((t_rows, 1), row_map),
                  pl.BlockSpec((t_rows, cb), lambda b, c, s: (b * n_t + s, c)),
                  pl.BlockSpec((t_rows, cb), lambda b, c, s: (b * n_t + s, c)),
                  pl.BlockSpec((CONV_W, cb), lambda b, c, s: (0, c)),
                  pl.BlockSpec((1, cb), lambda b, c, s: (0, c)),
                  pl.BlockSpec((cb // LRU_BLOCK, LRU_BLOCK, 2 * LRU_BLOCK), lambda b, c, s: (c, 0, 0)),
                  pl.BlockSpec((1, cb), lambda b, c, s: (0, c)),
                  pl.BlockSpec((1, cb), lambda b, c, s: (0, c)),
                  pl.BlockSpec((1, cb), lambda b, c, s: (0, c))],
        out_specs=pl.BlockSpec((t_rows, cb), lambda b, c, s: (b * n_t + s, c)),
        out_shape=jax.ShapeDtypeStruct((batch * seq, D_RNN), _BF16),
        scratch_shapes=[pltpu.VMEM((t_rows + 8, cb), _F32),
                        pltpu.VMEM((8, cb), _F32),
                        pltpu.VMEM((t_rows // 8, 8, cb), _F32),
                        pltpu.VMEM((t_rows // 8, 8, cb), _F32),
                        pltpu.VMEM((t_rows // 8, 8, cb), _F32)],
        compiler_params=_cparams(3),
        name="conv_rglru",
    )(pos_col, u, gg, conv_w, conv_b, w_rg, b_a, b_x, lru_param)


def _attn_kernel(qt_ref, k_ref, vt_ref, qtn_ref, kn_ref, lq1_ref, lk1_ref, lq2_ref, lk2_ref, g_ref, o_ref,
                 m_s, l_s, acc_s, st_s, *, tk, cq):
    tq = qt_ref.shape[1]
    q0 = pl.program_id(2) * tq
    n_chunks = tq // cq
    n_diag = tq // tk

    m_s[...] = jnp.full(m_s.shape, MASK_VALUE, _F32)
    l_s[...] = jnp.zeros(l_s.shape, _F32)
    acc_s[...] = jnp.zeros(acc_s.shape, _F32)

    def chunks(diag):
        return [c for c in range(n_chunks) if diag is None or (c + 1) * cq > diag]

    def scores_of(k_blk, q_ref, slot, diag):
        for c in chunks(diag):
            cols = slice(c * cq, (c + 1) * cq)
            for mp in range(2):
                dims = slice(mp * HEAD_DIM, (mp + 1) * HEAD_DIM)
                st_s[slot, mp, :, cols] = jnp.dot(k_blk[:, dims], q_ref[dims, cols], preferred_element_type=_F32)

    def scores(col0, slot, diag):
        scores_of(k_ref[pl.ds(col0, tk), :], qt_ref, slot, diag)

    def update(col0, slot, diag):
        vt_blk = vt_ref[:, pl.ds(col0, tk)]
        for c in chunks(diag):
            masked = diag is not None and c * cq < diag + tk - 1
            cols = slice(c * cq, (c + 1) * cq)
            for mp in range(2):
                st = st_s[slot, mp, :, cols]
                if masked:
                    kv_pos = diag + lax.broadcasted_iota(jnp.int32, (tk, cq), 0)
                    q_pos = c * cq + lax.broadcasted_iota(jnp.int32, (tk, cq), 1)
                    st = jnp.where(kv_pos <= q_pos, st, MASK_VALUE)
                m_prev = m_s[mp, :, cols]
                m_new = jnp.maximum(m_prev, jnp.max(st, axis=0, keepdims=True))
                alpha = jnp.exp2(m_prev - m_new)
                pt = jnp.exp2(st - m_new)
                l_s[mp, :, cols] = alpha * l_s[mp, :, cols] + jnp.sum(pt, axis=0, keepdims=True)
                pv = jnp.dot(vt_blk, pt.astype(_BF16), preferred_element_type=_F32)
                acc_s[mp, :, cols] = alpha * acc_s[mp, :, cols] + pv
                m_s[mp, :, cols] = m_new

    n_pairs = q0 // (2 * tk)

    @pl.when((pl.program_id(0) == 0) & (pl.program_id(1) == 0) & (pl.program_id(2) == 0))
    def _():
        scores(0, 0, None)

    def pair(t, carry):
        c0 = pl.multiple_of(2 * t * tk, tk)
        scores(c0 + tk, 1, None)
        update(c0, 0, None)
        scores(c0 + 2 * tk, 0, None)
        update(c0 + tk, 1, None)
        return carry

    lax.fori_loop(0, n_pairs, pair, 0)
    for d in range(n_diag):
        col0 = pl.multiple_of(q0 + d * tk, tk)
        if d + 1 < n_diag:
            scores(col0 + tk, (d + 1) % 2, (d + 1) * tk)
        else:
            scores_of(kn_ref[...], qtn_ref, 0, None)
        update(col0, d % 2, d * tk)

    lam = (jnp.exp(jnp.sum(lq1_ref[...] * lk1_ref[...], axis=1, keepdims=True))
           - jnp.exp(jnp.sum(lq2_ref[...] * lk2_ref[...], axis=1, keepdims=True)) + LAM_INIT)
    ot = acc_s[0] / l_s[0] - lam * (acc_s[1] / l_s[1])
    yt = ot * lax.rsqrt(jnp.mean(ot * ot, axis=0, keepdims=True) + SUBLN_EPS)
    o_ref[...] = ((yt.T * g_ref[...]) * (1.0 - LAM_INIT)).astype(o_ref.dtype)


def _attention(qt, k, vt, lq1, lk1, lq2, lk2, subln_g, batch, seq):
    tq = min(1024, seq)
    tk = min(512, seq)
    cq = min(256, tq)
    nq = seq // tq
    assert tq % (2 * tk) == 0 and seq % tq == 0
    n_tiles = batch * N_HEADS * nq

    def next_tile(b, h, i):
        flat = jnp.minimum((b * N_HEADS + h) * nq + i + 1, n_tiles - 1)
        return flat // (N_HEADS * nq), (flat // nq) % N_HEADS, flat % nq

    def next_q(b, h, i):
        nb, nh, ni = next_tile(b, h, i)
        return nh, nb * nq + ni

    def next_k(b, h, i):
        nb, nh, _ = next_tile(b, h, i)
        return nb * (seq // tk), nh

    vec = pl.BlockSpec((1, HEAD_DIM), lambda b, h, i: (0, 0))
    return pl.pallas_call(
        functools.partial(_attn_kernel, tk=tk, cq=cq),
        grid=(batch, N_HEADS, nq),
        in_specs=[pl.BlockSpec((V_DIM, tq), lambda b, h, i: (h, b * nq + i)),
                  pl.BlockSpec((seq, V_DIM), lambda b, h, i: (b, h)),
                  pl.BlockSpec((V_DIM, seq), lambda b, h, i: (h, b)),
                  pl.BlockSpec((V_DIM, tq), next_q),
                  pl.BlockSpec((tk, V_DIM), next_k),
                  vec, vec, vec, vec,
                  pl.BlockSpec((1, V_DIM), lambda b, h, i: (0, 0))],
        out_specs=pl.BlockSpec((tq, V_DIM), lambda b, h, i: (b * nq + i, h)),
        out_shape=jax.ShapeDtypeStruct((batch * seq, ATTN_W), _BF16),
        scratch_shapes=[pltpu.VMEM((2, 1, tq), _F32),
                        pltpu.VMEM((2, 1, tq), _F32),
                        pltpu.VMEM((2, V_DIM, tq), _F32),
                        pltpu.VMEM((2, 2, tk, tq), _F32)],
        compiler_params=_cparams(3),
        name="diff_attention",
    )(qt, k, vt, qt, k, lq1, lk1, lq2, lk2, subln_g)


def _merge_kernel(yr_ref, ya_ref, wr_ref, wa_ref, gr_ref, ga_ref, o_ref):
    pr = jnp.dot(yr_ref[...], wr_ref[...], preferred_element_type=_F32)
    pa = jnp.dot(ya_ref[...], wa_ref[...], preferred_element_type=_F32)
    o_ref[...] = (gr_ref[...].astype(_F32) * pr + ga_ref[...].astype(_F32) * pa).astype(o_ref.dtype)


def _merge(y_rnn, y_attn, w_r, w_a, gates):
    n, k = y_rnn.shape
    tm = min(512, n)
    tn = 1024
    n_tn = D_MODEL // tn
    return pl.pallas_call(
        _merge_kernel,
        grid=(n_tn, n // tm),
        in_specs=[pl.BlockSpec((tm, k), lambda j, i: (i, 0)),
                  pl.BlockSpec((tm, k), lambda j, i: (i, 0)),
                  pl.BlockSpec((k, tn), lambda j, i: (0, j)),
                  pl.BlockSpec((k, tn), lambda j, i: (0, j)),
                  pl.BlockSpec((tm, tn), lambda j, i: (i, j)),
                  pl.BlockSpec((tm, tn), lambda j, i: (i, n_tn + j))],
        out_specs=pl.BlockSpec((tm, tn), lambda j, i: (i, j)),
        out_shape=jax.ShapeDtypeStruct((n, D_MODEL), _BF16),
        compiler_params=_cparams(2),
        name="branch_merge",
    )(y_rnn, y_attn, w_r, w_a, gates, gates)


def _outproj_kernel(m_ref, w_ref, x_ref, g_ref, rhi_ref, rlo_ref, x2_ref, h2_ref, lt_ref):
    x2 = x_ref[...] + _dot(m_ref[...], w_ref[...])
    x2_ref[...] = x2
    h2 = (x2 * lax.rsqrt(jnp.mean(x2 * x2, axis=-1, keepdims=True) + NORM_EPS)) * g_ref[...]
    _store_token_tiles(h2_ref, _pack_halves(h2))
    h_hi = h2.astype(_BF16)
    h_lo = (h2 - h_hi.astype(_F32)).astype(_BF16)
    logits = _dot(h_hi, rhi_ref[...]) + (_dot(h_lo, rhi_ref[...]) + _dot(h_hi, rlo_ref[...]))
    lt_ref[...] = logits.T[:ROUTER_ROWS]


def _outproj(merged, w_out, x, g_ffn, w_router):
    n, d = x.shape
    tm = min(256, n)
    r_hi = w_router.astype(_BF16)
    r_lo = (w_router - r_hi.astype(_F32)).astype(_BF16)
    return pl.pallas_call(
        _outproj_kernel,
        grid=(n // tm,),
        in_specs=[pl.BlockSpec((tm, d), lambda i: (i, 0)),
                  pl.BlockSpec((d, d), lambda i: (0, 0)),
                  pl.BlockSpec((tm, d), lambda i: (i, 0)),
                  pl.BlockSpec((1, d), lambda i: (0, 0)),
                  pl.BlockSpec((d, 128), lambda i: (0, 0)),
                  pl.BlockSpec((d, 128), lambda i: (0, 0))],
        out_specs=[pl.BlockSpec((tm, d), lambda i: (i, 0)),
                   pl.BlockSpec((tm * TILE_ROWS, 128), lambda i: (i, 0)),
                   pl.BlockSpec((ROUTER_ROWS, tm), lambda i: (0, i))],
        out_shape=[jax.ShapeDtypeStruct((n, d), _F32),
                   jax.ShapeDtypeStruct((n * TILE_ROWS, 128), jnp.uint32),
                   jax.ShapeDtypeStruct((ROUTER_ROWS, n), _F32)],
        compiler_params=_cparams(1),
        name="outproj_norm_router",
    )(merged, w_out, x, g_ffn, r_hi, r_lo)


def _first_index_of_max(vals, n_rows):
    vmax = jnp.max(vals, axis=0, keepdims=True)
    rows = lax.broadcasted_iota(jnp.int32, vals.shape, 0).astype(_F32)
    idx = jnp.min(jnp.where(vals == vmax, rows, float(n_rows)), axis=0, keepdims=True)
    return vmax, idx


def _route_kernel(lt_ref, tri_ref, eid_ref, wts_ref, rank_ref, cnt_ref, carry):
    tl = lt_ref.shape[1]

    @pl.when(pl.program_id(0) == 0)
    def _():
        carry[...] = jnp.zeros(carry.shape, _F32)

    lt = lt_ref[...]
    g = lt[0:N_GROUPS]
    ge = jnp.exp(g - jnp.max(g, axis=0, keepdims=True))
    gp = ge / jnp.sum(ge, axis=0, keepdims=True)
    g_w, g_idx = _first_index_of_max(gp, N_GROUPS)
    el = jnp.zeros((EXPERTS_PER_GROUP, tl), _F32)
    for gg in range(N_GROUPS):
        el = jnp.where(g_idx == float(gg), lt[8 + gg * EXPERTS_PER_GROUP:8 + (gg + 1) * EXPERTS_PER_GROUP], el)
    ee = jnp.exp(el - jnp.max(el, axis=0, keepdims=True))
    ep = ee / jnp.sum(ee, axis=0, keepdims=True)
    v1, i1 = _first_index_of_max(ep, EXPERTS_PER_GROUP)
    rows8 = lax.broadcasted_iota(jnp.int32, ep.shape, 0).astype(_F32)
    v2, i2 = _first_index_of_max(jnp.where(rows8 == i1, -1.0, ep), EXPERTS_PER_GROUP)
    den = v1 + v2
    e1 = g_idx * float(EXPERTS_PER_GROUP) + i1
    e2 = g_idx * float(EXPERTS_PER_GROUP) + i2

    rows_e = lax.broadcasted_iota(jnp.int32, (N_EXPERTS, tl), 0).astype(_F32)
    ranks = []
    for e_sel in (e1, e2):
        onehot = rows_e == e_sel
        oh = jnp.where(onehot, 1.0, 0.0)
        before = jnp.dot(oh.astype(_BF16), tri_ref[...], preferred_element_type=_F32) + carry[...]
        ranks.append(jnp.sum(jnp.where(onehot, before, 0.0), axis=0, keepdims=True))
        carry[...] = carry[...] + jnp.sum(oh, axis=1, keepdims=True)

    zeros6 = jnp.zeros((6, tl), _F32)
    eid_ref[...] = jnp.concatenate([e1, e2, zeros6], axis=0).astype(jnp.int32)
    wts_ref[...] = jnp.concatenate([g_w * (v1 / den), g_w * (v2 / den), zeros6], axis=0)
    rank_ref[...] = jnp.concatenate(ranks + [zeros6], axis=0).astype(jnp.int32)
    cnt_ref[...] = jnp.broadcast_to(carry[...], cnt_ref.shape).astype(jnp.int32)


def _route(logits_t):
    n = logits_t.shape[1]
    tl = min(512, n)
    tri = jnp.asarray(np.triu(np.ones((tl, tl), np.float32), k=1), _BF16)
    row_block = pl.BlockSpec((8, tl), lambda i: (0, i))
    return pl.pallas_call(
        _route_kernel,
        grid=(n // tl,),
        in_specs=[pl.BlockSpec((ROUTER_ROWS, tl), lambda i: (0, i)),
                  pl.BlockSpec((tl, tl), lambda i: (0, 0))],
        out_specs=[row_block, row_block, row_block,
                   pl.BlockSpec((N_EXPERTS, 128), lambda i: (0, 0))],
        out_shape=[jax.ShapeDtypeStruct((8, n), jnp.int32),
                   jax.ShapeDtypeStruct((8, n), _F32),
                   jax.ShapeDtypeStruct((8, n), jnp.int32),
                   jax.ShapeDtypeStruct((N_EXPERTS, 128), jnp.int32)],
        scratch_shapes=[pltpu.VMEM((N_EXPERTS, 1), _F32)],
        compiler_params=_cparams(1),
        name="route_topk_rank",
    )(logits_t, tri)


def _row_copy(src_hbm, row, dst_vmem, r, sem):
    return pltpu.make_async_copy(src_hbm.at[_token_rows(row)], dst_vmem.at[_token_rows(r)], sem)


def _dispatch_kernel(dest_ref, cnt_ref, pstart_ref, pend_ref, h_ref, xs_hbm, rows, zero_rows, sem, zsem, *, n_steps):
    tm = h_ref.shape[0] // TILE_ROWS
    i = pl.program_id(0)
    n = n_steps * tm
    n_slots = xs_hbm.shape[0] // TILE_ROWS
    slot = i % 2
    n_jobs = 2 * N_EXPERTS
    jobs_per_step = -(-n_jobs // n_steps)

    def pad_job(j, wait):
        e = jnp.minimum(j, N_EXPERTS - 1)
        tail_lo = pend_ref[N_EXPERTS - 1] + (j - N_EXPERTS) * SLOT_BLOCK
        lo = jnp.where(j < N_EXPERTS, pstart_ref[e] + cnt_ref[e], tail_lo)
        hi = jnp.where(j < N_EXPERTS, pend_ref[e], jnp.minimum(tail_lo + SLOT_BLOCK, n_slots))

        def body(r, c):
            cp = pltpu.make_async_copy(zero_rows, xs_hbm.at[_token_rows(r)], zsem)
            cp.wait() if wait else cp.start()
            return c
        lax.fori_loop(lo, jnp.maximum(lo, hi), body, 0)

    def pad_jobs(wait):
        for jj in range(jobs_per_step):
            j = i * jobs_per_step + jj

            @pl.when(j < n_jobs)
            def _():
                pad_job(j, wait)

    @pl.when(i == 0)
    def _():
        zero_rows[...] = jnp.zeros(zero_rows.shape, zero_rows.dtype)

    pad_jobs(False)

    def row_copy(tile_slot, r, row):
        return pltpu.make_async_copy(rows.at[tile_slot, _token_rows(r)], xs_hbm.at[_token_rows(row)], sem.at[tile_slot])

    def drain(tile_slot):
        def body(r, c):
            for _ in range(TOP_K):
                row_copy(tile_slot, r, 0).wait()
            return c
        lax.fori_loop(0, tm, body, 0, unroll=8)

    @pl.when(i >= 2)
    def _():
        drain(slot)

    rows[slot] = h_ref[...]

    def issue(r, c):
        for k in range(TOP_K):
            row_copy(slot, r, dest_ref[k * n + i * tm + r]).start(priority=k)
        return c

    lax.fori_loop(0, tm, issue, 0, unroll=8)
    pad_jobs(True)

    @pl.when(i == n_steps - 1)
    def _():
        if n_steps >= 2:
            drain(1 - slot)
        drain(slot)


def _dispatch(dest, counts, pstarts, pends, h2, n_slots):
    n = h2.shape[0] // TILE_ROWS
    tm = min(256, n)
    return pl.pallas_call(
        functools.partial(_dispatch_kernel, n_steps=n // tm),
        grid_spec=pltpu.PrefetchScalarGridSpec(
            num_scalar_prefetch=4,
            grid=(n // tm,),
            in_specs=[pl.BlockSpec((tm * TILE_ROWS, 128), lambda i, *_: (i, 0))],
            out_specs=pl.BlockSpec(memory_space=pl.ANY),
            scratch_shapes=[pltpu.VMEM((2, tm * TILE_ROWS, 128), h2.dtype), pltpu.VMEM((TILE_ROWS, 128), h2.dtype),
                            pltpu.SemaphoreType.DMA((2,)), pltpu.SemaphoreType.DMA]),
        out_shape=jax.ShapeDtypeStruct((n_slots * TILE_ROWS, 128), h2.dtype),
        compiler_params=_cparams(1),
        name="slot_dispatch",
    )(dest, counts, pstarts, pends, h2)


def _expert_kernel(be_ref, first_ref, slot_ref, nxt_ref, x_ref, wg_hbm, wu_hbm, wd_hbm, o_ref,
                   wg_buf, wu_buf, wd_buf, sem):
    i = pl.program_id(0)

    def weight_copies(e, s):
        return (pltpu.make_async_copy(wg_hbm.at[e], wg_buf.at[s], sem.at[s, 0]),
                pltpu.make_async_copy(wu_hbm.at[e], wu_buf.at[s], sem.at[s, 1]),
                pltpu.make_async_copy(wd_hbm.at[e], wd_buf.at[s], sem.at[s, 2]))

    @pl.when(i == 0)
    def _():
        for cp in weight_copies(be_ref[0], 0):
            cp.start()

    s = slot_ref[i]

    @pl.when(first_ref[i] == 1)
    def _():
        for cp in weight_copies(be_ref[i], s):
            cp.wait()

        @pl.when(nxt_ref[i] >= 0)
        def _():
            for cp in weight_copies(nxt_ref[i], 1 - s):
                cp.start()

    x_lo, x_hi = _unpack_halves(_load_token_tiles(x_ref))
    g = _dot(x_lo, wg_buf[s, :HALF]) + _dot(x_hi, wg_buf[s, HALF:])
    u = _dot(x_lo, wu_buf[s, :HALF]) + _dot(x_hi, wu_buf[s, HALF:])
    mid = (jax.nn.silu(g) * u).astype(_BF16)
    _store_token_tiles(o_ref, _pack_halves(_dot(mid, wd_buf[s])))


def _experts(blk_e, xs, w_gate, w_up, w_down):
    p = xs.shape[0] // TILE_ROWS
    d = w_gate.shape[1]
    nb = p // SLOT_BLOCK
    first = jnp.concatenate([jnp.ones((1,), jnp.int32), (blk_e[1:] != blk_e[:-1]).astype(jnp.int32)])
    slot = (jnp.cumsum(first) - 1) % 2
    later = blk_e[None, :] > blk_e[:, None]
    nxt = jnp.min(jnp.where(later, blk_e[None, :], N_EXPERTS), axis=1)
    nxt = jnp.where(nxt < N_EXPERTS, nxt, -1).astype(jnp.int32)
    any_spec = pl.BlockSpec(memory_space=pl.ANY)
    return pl.pallas_call(
        _expert_kernel,
        grid_spec=pltpu.PrefetchScalarGridSpec(
            num_scalar_prefetch=4,
            grid=(nb,),
            in_specs=[pl.BlockSpec((SLOT_BLOCK * TILE_ROWS, 128), lambda i, *_: (i, 0)), any_spec, any_spec, any_spec],
            out_specs=pl.BlockSpec((SLOT_BLOCK * TILE_ROWS, 128), lambda i, *_: (i, 0)),
            scratch_shapes=[pltpu.VMEM((2, d, D_EXPERT), _F32),
                            pltpu.VMEM((2, d, D_EXPERT), _F32),
                            pltpu.VMEM((2, D_EXPERT, d), _F32),
                            pltpu.SemaphoreType.DMA((2, 3))]),
        out_shape=jax.ShapeDtypeStruct((p * TILE_ROWS, 128), jnp.uint32),
        compiler_params=pltpu.CompilerParams(dimension_semantics=("arbitrary",),
                                             vmem_limit_bytes=EXPERT_VMEM_LIMIT),
        name="expert_mlp",
    )(blk_e, first, slot.astype(jnp.int32), nxt, xs, w_gate, w_up, w_down)


def _combine_kernel(dest_ref, x2_ref, w0_ref, w1_ref, g_ref, ys_hbm, o_ref, rows, sem):
    tm = x2_ref.shape[0]
    n = pl.num_programs(0) * tm
    i = pl.program_id(0)

    def issue(tile, slot):
        def body(r, c):
            for k in range(TOP_K):
                _row_copy(ys_hbm, dest_ref[k * n + tile * tm + r], rows.at[slot, k], r, sem.at[slot]).start(priority=k)
            return c
        lax.fori_loop(0, tm, body, 0, unroll=8)

    @pl.when(i == 0)
    def _():
        issue(0, 0)

    @pl.when(i + 1 < pl.num_programs(0))
    def _():
        issue(i + 1, (i + 1) % 2)

    slot = i % 2

    def drain(r, c):
        for k in range(TOP_K):
            _row_copy(ys_hbm, 0, rows.at[slot, k], r, sem.at[slot]).wait()
        return c

    lax.fori_loop(0, tm, drain, 0, unroll=8)
    y0 = _unpack_halves(_load_token_tiles(rows.at[slot, 0]))
    y1 = _unpack_halves(_load_token_tiles(rows.at[slot, 1]))
    w0 = w0_ref[...]
    w1 = w1_ref[...]
    halves = (slice(0, HALF), slice(HALF, 2 * HALF))
    x3 = [x2_ref[:, cols] + (w0 * a + w1 * b) for cols, a, b in zip(halves, y0, y1)]
    ms = (jnp.sum(x3[0] * x3[0], axis=-1, keepdims=True) + jnp.sum(x3[1] * x3[1], axis=-1, keepdims=True)) / (2 * HALF)
    inv = lax.rsqrt(ms + NORM_EPS)
    for cols, v in zip(halves, x3):
        o_ref[:, cols] = (v * inv) * g_ref[:, cols]


def _combine(dest, x2, w0, w1, g_final, ys):
    n, d = x2.shape
    tm = min(256, n)
    return pl.pallas_call(
        _combine_kernel,
        grid_spec=pltpu.PrefetchScalarGridSpec(
            num_scalar_prefetch=1,
            grid=(n // tm,),
            in_specs=[pl.BlockSpec((tm, d), lambda i, dst: (i, 0)),
                      pl.BlockSpec((tm, 1), lambda i, dst: (i, 0)),
                      pl.BlockSpec((tm, 1), lambda i, dst: (i, 0)),
                      pl.BlockSpec((1, d), lambda i, dst: (0, 0)),
                      pl.BlockSpec(memory_space=pl.ANY)],
            out_specs=pl.BlockSpec((tm, d), lambda i, dst: (i, 0)),
            scratch_shapes=[pltpu.VMEM((2, TOP_K, tm * TILE_ROWS, 128), jnp.uint32), pltpu.SemaphoreType.DMA((2,))]),
        out_shape=jax.ShapeDtypeStruct((n, d), _F32),
        compiler_params=_cparams(1),
        name="combine_final_norm",
    )(dest, x2, w0, w1, g_final, ys)


def kernel(x, positions, g_mix, w_in, conv_w, conv_b, w_rg_a, b_rg_a, w_rg_x, b_rg_x, lru_param, lambda_q1, lambda_k1, lambda_q2, lambda_k2, subln_g, w_br_rnn, w_br_attn, w_out, g_ffn, w_grp_router, w_exp_router, w_gate, w_up, w_down, g_final):
    batch, seq, d = x.shape
    n = batch * seq
    xf = x.reshape(n, d)
    pos_col = positions.reshape(n, 1)
    row = lambda v: v.reshape(1, -1)

    w_rg = jnp.concatenate([w_rg_a[0], w_rg_x[0]], axis=-1).astype(_BF16)
    w_r_b = w_br_rnn[0].astype(_BF16)
    w_a_b = w_br_attn[0].astype(_BF16)
    w_out_b = w_out[0].astype(_BF16)
    w_router = jnp.concatenate(
        [w_grp_router[0], jnp.zeros((d, 8 - N_GROUPS), _F32), w_exp_router[0],
         jnp.zeros((d, 128 - ROUTER_ROWS), _F32)], axis=1)

    h = _rmsnorm(xf, row(g_mix[0]), _BF16)
    cos_t, sin_t = _rope_tables(pos_col)

    c_qk = 2 * D_RNN
    c_v = c_qk + 2 * QK_W
    c_g = c_v + ATTN_W
    u = _projection(_proj_plain_kernel, h, w_in[0], 0, D_RNN, _F32, name="proj_rnn")
    gg = _projection(_proj_gelu_kernel, h, w_in[0], D_RNN, D_RNN, _F32, name="proj_rnn_gate")
    q_scale = (HEAD_DIM ** -0.5) * LOG2E
    qt = _projection(functools.partial(_proj_rope_kernel, scale=q_scale, transpose_out=True),
                     h, w_in[0], c_qk, QK_W, _BF16, extra=(cos_t, sin_t), transpose_out=True, name="proj_q_rope")
    k = _projection(functools.partial(_proj_rope_kernel, scale=1.0, transpose_out=False),
                    h, w_in[0], c_qk + QK_W, QK_W, _BF16, extra=(cos_t, sin_t), name="proj_k_rope")
    vt = _projection(_proj_t_kernel, h, w_in[0], c_v, ATTN_W, _BF16, transpose_out=True, name="proj_v")
    gates = _projection(_proj_sigmoid_kernel, h, w_in[0], c_g, 2 * D_MODEL, _BF16, name="proj_gates")

    y_rnn = _lru(pos_col, u, gg, conv_w[0], row(conv_b[0]), w_rg, row(b_rg_a[0]), row(b_rg_x[0]),
                 row(lru_param[0]), batch, seq)
    y_attn = _attention(qt, k, vt, row(lambda_q1[0]), row(lambda_k1[0]), row(lambda_q2[0]), row(lambda_k2[0]),
                        row(subln_g[0]), batch, seq)
    merged = _merge(y_rnn, y_attn, w_r_b, w_a_b, gates)
    x2, h2, logits_t = _outproj(merged, w_out_b, xf, row(g_ffn[0]), w_router)

    eid8, wts8, rank8, cnt = _route(logits_t)
    eid, rank, counts = eid8[:TOP_K], rank8[:TOP_K], cnt[:, 0]

    pcounts = ((counts + SLOT_BLOCK - 1) // SLOT_BLOCK) * SLOT_BLOCK
    pends = jnp.cumsum(pcounts)
    pstarts = pends - pcounts
    experts = jnp.arange(N_EXPERTS, dtype=jnp.int32)
    dest = jnp.sum(jnp.where(eid[:, :, None] == experts, pstarts, 0), axis=-1) + rank
    n_slots = n * TOP_K + N_EXPERTS * SLOT_BLOCK
    blk_start = jnp.arange(n_slots // SLOT_BLOCK, dtype=jnp.int32) * SLOT_BLOCK
    blk_e = jnp.minimum(jnp.sum((blk_start[:, None] >= pends[None, :]).astype(jnp.int32), axis=1), N_EXPERTS - 1)

    dest_flat = dest.reshape(-1)
    xs = _dispatch(dest_flat, counts, pstarts, pends, h2, n_slots)
    ys = _experts(blk_e, xs, w_gate[0], w_up[0], w_down[0])
    out = _combine(dest_flat, x2, wts8[0].reshape(n, 1), wts8[1].reshape(n, 1), row(g_final), ys)
    return out.reshape(batch, seq, d)
```

```python
import functools
import math

import numpy as np
import jax
import jax.numpy as jnp
from jax import lax
from jax.experimental import pallas as pl
from jax.experimental.pallas import tpu as pltpu

D_MODEL = 2048
D_RNN = D_MODEL
LRU_BLOCK = 128
N_LRU_BLOCKS = D_RNN // LRU_BLOCK
CONV_W = 4
C_LRU = 8.0
HEAD_DIM = 128
N_HEADS = D_MODEL // (2 * HEAD_DIM)
V_DIM = 2 * HEAD_DIM
QK_W = N_HEADS * 2 * HEAD_DIM
ATTN_W = N_HEADS * V_DIM
ROPE_THETA = 10000.0
SUBLN_EPS = 1e-5
N_GROUPS = 4
EXPERTS_PER_GROUP = 8
N_EXPERTS = N_GROUPS * EXPERTS_PER_GROUP
TOP_K = 2
D_EXPERT = D_MODEL // 2
NORM_EPS = 1e-6
LAM_INIT = 0.8 - 0.6 * math.exp(-0.3 * 0)

SLOT_BLOCK = 256
HALF = D_MODEL // 2
TILE_ROWS = HALF // 128
ROUTER_ROWS = 8 + N_EXPERTS
MASK_VALUE = -1e30
LOG2E = 1.4426950408889634
VMEM_LIMIT = 56 * 1024 * 1024
EXPERT_VMEM_LIMIT = 60 * 1024 * 1024

_F32 = jnp.float32
_BF16 = jnp.bfloat16


def _dot(a, b):
    return lax.dot_general(a, b, (((1,), (0,)), ((), ())), preferred_element_type=_F32)


def _pack_halves(y):
    bits = lax.bitcast_convert_type(y.astype(_BF16).astype(_F32), jnp.uint32)
    return (bits[:, :HALF] >> 16) | bits[:, HALF:]


def _unpack_halves(p):
    lo = lax.bitcast_convert_type(p << 16, _F32)
    hi = lax.bitcast_convert_type(p & jnp.uint32(0xFFFF0000), _F32)
    return lo, hi


def _token_rows(t):
    return pl.ds(pl.multiple_of(t * TILE_ROWS, TILE_ROWS), TILE_ROWS)


def _store_token_tiles(ref, packed):
    tokens = packed.shape[0]
    for j in range(TILE_ROWS):
        ref[pl.ds(j, tokens, stride=TILE_ROWS), :] = packed[:, j * 128:(j + 1) * 128]


def _load_token_tiles(ref):
    tokens = ref.shape[0] // TILE_ROWS
    return jnp.concatenate([ref[pl.ds(j, tokens, stride=TILE_ROWS), :] for j in range(TILE_ROWS)], axis=1)


def _cparams(n_axes):
    return pltpu.CompilerParams(dimension_semantics=("arbitrary",) * n_axes,
                                vmem_limit_bytes=VMEM_LIMIT)


def _rmsnorm_kernel(x_ref, g_ref, o_ref):
    x = x_ref[...]
    y = x * lax.rsqrt(jnp.mean(x * x, axis=-1, keepdims=True) + NORM_EPS)
    o_ref[...] = (y * g_ref[...]).astype(o_ref.dtype)


def _rmsnorm(x, g, out_dtype):
    n, d = x.shape
    tm = min(512, n)
    return pl.pallas_call(
        _rmsnorm_kernel,
        grid=(n // tm,),
        in_specs=[pl.BlockSpec((tm, d), lambda i: (i, 0)),
                  pl.BlockSpec((1, d), lambda i: (0, 0))],
        out_specs=pl.BlockSpec((tm, d), lambda i: (i, 0)),
        out_shape=jax.ShapeDtypeStruct((n, d), out_dtype),
        compiler_params=_cparams(1),
        name="rmsnorm",
    )(x, g)


def _rope_table_kernel(pos_ref, inv_ref, sign_ref, cos_ref, sin_ref):
    ang = pos_ref[...].astype(_F32) * inv_ref[...]
    cos_ref[...] = jnp.cos(ang)
    sin_ref[...] = jnp.sin(ang) * sign_ref[...]


def _rope_tables(pos_col):
    n = pos_col.shape[0]
    tm = min(1024, n)
    inv = 1.0 / (ROPE_THETA ** (np.arange(0, HEAD_DIM, 2, dtype=np.float32) / np.float32(HEAD_DIM)))
    inv = np.concatenate([inv, inv]).astype(np.float32)[None, :]
    sign = np.concatenate([-np.ones(HEAD_DIM // 2), np.ones(HEAD_DIM // 2)]).astype(np.float32)[None, :]
    return pl.pallas_call(
        _rope_table_kernel,
        grid=(n // tm,),
        in_specs=[pl.BlockSpec((tm, 1), lambda i: (i, 0)),
                  pl.BlockSpec((1, HEAD_DIM), lambda i: (0, 0)),
                  pl.BlockSpec((1, HEAD_DIM), lambda i: (0, 0))],
        out_specs=[pl.BlockSpec((tm, HEAD_DIM), lambda i: (i, 0))] * 2,
        out_shape=[jax.ShapeDtypeStruct((n, HEAD_DIM), _F32)] * 2,
        compiler_params=_cparams(1),
        name="rope_tables",
    )(pos_col, jnp.asarray(inv), jnp.asarray(sign))


def _proj_plain_kernel(h_ref, w_ref, o_ref):
    o_ref[...] = _dot(h_ref[...], w_ref[...]).astype(o_ref.dtype)


def _proj_gelu_kernel(h_ref, w_ref, o_ref):
    o_ref[...] = jax.nn.gelu(_dot(h_ref[...], w_ref[...]), approximate=True).astype(o_ref.dtype)


def _proj_sigmoid_kernel(h_ref, w_ref, o_ref):
    acc = _dot(h_ref[...], w_ref[...])
    o_ref[...] = jax.nn.sigmoid(acc).astype(o_ref.dtype)


def _proj_t_kernel(h_ref, w_ref, o_ref):
    acc = _dot(h_ref[...], w_ref[...])
    for g in range(acc.shape[1] // HEAD_DIM):
        o_ref[g * HEAD_DIM:(g + 1) * HEAD_DIM, :] = acc[:, g * HEAD_DIM:(g + 1) * HEAD_DIM].T.astype(o_ref.dtype)


def _proj_rope_kernel(h_ref, w_ref, cos_ref, sin_ref, o_ref, *, scale, transpose_out):
    acc = _dot(h_ref[...], w_ref[...])
    cos = cos_ref[...]
    sin = sin_ref[...]
    for g in range(acc.shape[1] // HEAD_DIM):
        t = acc[:, g * HEAD_DIM:(g + 1) * HEAD_DIM]
        r = (t * cos + pltpu.roll(t, HEAD_DIM // 2, axis=1) * sin) * scale
        if transpose_out:
            o_ref[g * HEAD_DIM:(g + 1) * HEAD_DIM, :] = r.T.astype(o_ref.dtype)
        else:
            o_ref[:, g * HEAD_DIM:(g + 1) * HEAD_DIM] = r.astype(o_ref.dtype)


def _projection(kernel_fn, h, w, col_off, n_cols, out_dtype, extra=(), transpose_out=False, name="proj"):
    n, k = h.shape
    tm = min(1024, n)
    tn = 1024
    off = col_off // tn
    extra_specs = [pl.BlockSpec((tm, e.shape[1]), lambda j, i: (i, 0)) for e in extra]
    if transpose_out:
        out_spec = pl.BlockSpec((tn, tm), lambda j, i: (j, i))
        out_shape = jax.ShapeDtypeStruct((n_cols, n), out_dtype)
    else:
        out_spec = pl.BlockSpec((tm, tn), lambda j, i: (i, j))
        out_shape = jax.ShapeDtypeStruct((n, n_cols), out_dtype)
    return pl.pallas_call(
        kernel_fn,
        grid=(n_cols // tn, n // tm),
        in_specs=[pl.BlockSpec((tm, k), lambda j, i: (i, 0)),
                  pl.BlockSpec((k, tn), lambda j, i: (0, j + off))] + extra_specs,
        out_specs=out_spec,
        out_shape=out_shape,
        compiler_params=_cparams(2),
        name=name,
    )(h, w, *extra)


def _lru_kernel(pos_ref, u_ref, gg_ref, cw_ref, cb_ref, wrg_ref, ba_ref, bx_ref, lp_ref, o_ref,
                ubuf, hcar, a_s, b_s, h_s):
    t_rows, cb = u_ref.shape
    n_grp = t_rows // 8

    @pl.when(pl.program_id(2) == 0)
    def _():
        ubuf[0:8, :] = jnp.zeros((8, cb), _F32)
        hcar[...] = jnp.zeros((8, cb), _F32)

    ubuf[8:8 + t_rows, :] = u_ref[...]
    cw = cw_ref[...]
    uc = cb_ref[...] + cw[0:1] * ubuf[5:5 + t_rows, :]
    uc = uc + cw[1:2] * ubuf[6:6 + t_rows, :]
    uc = uc + cw[2:3] * ubuf[7:7 + t_rows, :]
    uc = uc + cw[3:4] * ubuf[8:8 + t_rows, :]
    ubuf[0:8, :] = ubuf[t_rows:t_rows + 8, :]

    ucb = uc.astype(_BF16)
    r_parts, i_parts = [], []
    for j in range(cb // LRU_BLOCK):
        g = jnp.dot(ucb[:, j * LRU_BLOCK:(j + 1) * LRU_BLOCK], wrg_ref[j], preferred_element_type=_F32)
        r_parts.append(g[:, :LRU_BLOCK])
        i_parts.append(g[:, LRU_BLOCK:])
    r = 0.5 * jnp.tanh(0.5 * (jnp.concatenate(r_parts, axis=1) + ba_ref[...])) + 0.5
    gi = 0.5 * jnp.tanh(0.5 * (jnp.concatenate(i_parts, axis=1) + bx_ref[...])) + 0.5

    log_a = (-C_LRU) * r * jax.nn.softplus(-lp_ref[...])
    a_raw = jnp.exp(log_a)
    m2 = 1.0 - a_raw * a_raw
    mult = jnp.where(m2 > 0.0, m2 * lax.rsqrt(m2), 0.0)
    reset = pos_ref[...] == 0
    a = jnp.where(reset, 0.0, a_raw)
    mult = jnp.where(reset, 1.0, mult)
    bv = uc * gi * mult

    a3 = a.reshape(n_grp, 8, cb)
    b3 = bv.reshape(n_grp, 8, cb)
    row = lax.broadcasted_iota(jnp.int32, (n_grp, 8, cb), 1)
    for d in (1, 2, 4):
        a_sh = pltpu.roll(a3, d, axis=1)
        b_sh = pltpu.roll(b3, d, axis=1)
        take = row >= d
        b3 = jnp.where(take, a3 * b_sh + b3, b3)
        a3 = jnp.where(take, a3 * a_sh, a3)
    a_s[...] = a3
    b_s[...] = b3

    def carry_step(g, carry):
        h = b_s[g] + a_s[g] * carry
        h_s[g] = h
        return jnp.broadcast_to(h[7:8, :], (8, cb))

    hcar[...] = lax.fori_loop(0, n_grp, carry_step, hcar[...], unroll=8)

    hr = h_s[...].reshape(t_rows, cb)
    o_ref[...] = (hr * gg_ref[...]).astype(o_ref.dtype)


def _lru(pos_col, u, gg, conv_w, conv_b, w_rg, b_a, b_x, lru_param, batch, seq):
    cb = 512
    t_rows = min(512, seq)
    n_cb = D_RNN // cb
    n_t = seq // t_rows
    row_map = lambda b, c, s: (b * n_t + s, 0)
    return pl.pallas_call(
        _lru_kernel,
        grid=(batch, n_cb, n_t),
        in_specs=[pl.BlockSpec((t_rows, 1), row_map),
                  pl.BlockSpec((t_rows, cb), lambda b, c, s: (b * n_t + s, c)),
                  pl.BlockSpec((t_rows, cb), lambda b, c, s: (b * n_t + s, c)),
                  pl.BlockSpec((CONV_W, cb), lambda b, c, s: (0, c)),
                  pl.BlockSpec((1, cb), lambda b, c, s: (0, c)),
                  pl.BlockSpec((cb // LRU_BLOCK, LRU_BLOCK, 2 * LRU_BLOCK), lambda b, c, s: (c, 0, 0)),
                  pl.BlockSpec((1, cb), lambda b, c, s: (0, c)),
                  pl.BlockSpec((1, cb), lambda b, c, s: (0, c)),
                  pl.BlockSpec((1, cb), lambda b, c, s: (0, c))],
        out_specs=pl.BlockSpec((t_rows, cb), lambda b, c, s: (b * n_t + s, c)),
        out_shape=jax.ShapeDtypeStruct((batch * seq, D_RNN), _BF16),
        scratch_shapes=[pltpu.VMEM((t_rows + 8, cb), _F32),
                        pltpu.VMEM((8, cb), _F32),
                        pltpu.VMEM((t_rows // 8, 8, cb), _F32),
                        pltpu.VMEM((t_rows // 8, 8, cb), _F32),
                        pltpu.VMEM((t_rows // 8, 8, cb), _F32)],
        compiler_params=_cparams(3),
        name="conv_rglru",
    )(pos_col, u, gg, conv_w, conv_b, w_rg, b_a, b_x, lru_param)


def _attn_kernel(qt_ref, k_ref, vt_ref, qtn_ref, kn_ref, lq1_ref, lk1_ref, lq2_ref, lk2_ref, g_ref, o_ref,
                 m_s, l_s, acc_s, st_s, mx_s, *, tk, cq):
    tq = qt_ref.shape[1]
    q0 = pl.program_id(2) * tq
    n_chunks = tq // cq
    n_diag = tq // tk

    m_s[...] = jnp.full(m_s.shape, MASK_VALUE, _F32)
    l_s[...] = jnp.zeros(l_s.shape, _F32)
    acc_s[...] = jnp.zeros(acc_s.shape, _F32)

    def chunks(diag):
        return [c for c in range(n_chunks) if diag is None or (c + 1) * cq > diag]

    def scores_of(k_blk, q_ref, slot, diag):
        for c in chunks(diag):
            cols = slice(c * cq, (c + 1) * cq)
            for mp in range(2):
                dims = slice(mp * HEAD_DIM, (mp + 1) * HEAD_DIM)
                st = jnp.dot(k_blk[:, dims], q_ref[dims, cols], preferred_element_type=_F32)
                st_s[slot, mp, :, cols] = st
                mx_s[slot, mp, :, cols] = jnp.max(st, axis=0, keepdims=True)

    def scores(col0, slot, diag):
        scores_of(k_ref[pl.ds(col0, tk), :], qt_ref, slot, diag)

    def update(col0, slot, diag):
        vt_blk = jnp.concatenate([vt_ref[:, pl.ds(col0, tk)], jnp.ones((16, tk), _BF16)], axis=0)
        for c in chunks(diag):
            masked = diag is not None and c * cq < diag + tk - 1
            cols = slice(c * cq, (c + 1) * cq)
            for mp in range(2):
                st = st_s[slot, mp, :, cols]
                if masked:
                    kv_pos = diag + lax.broadcasted_iota(jnp.int32, (tk, cq), 0)
                    q_pos = c * cq + lax.broadcasted_iota(jnp.int32, (tk, cq), 1)
                    st = jnp.where(kv_pos <= q_pos, st, MASK_VALUE)
                    m_cur = jnp.max(st, axis=0, keepdims=True)
                else:
                    m_cur = mx_s[slot, mp, :, cols]
                m_prev = m_s[mp, :, cols]
                m_new = jnp.maximum(m_prev, m_cur)
                alpha = jnp.exp2(m_prev - m_new)
                pt = jnp.exp2(st - m_new).astype(_BF16)
                pv = jnp.dot(vt_blk, pt, preferred_element_type=_F32)
                l_s[mp, :, cols] = alpha * l_s[mp, :, cols] + pv[V_DIM:V_DIM + 1]
                acc_s[mp, :, cols] = alpha * acc_s[mp, :, cols] + pv[:V_DIM]
                m_s[mp, :, cols] = m_new

    n_pairs = q0 // (2 * tk)

    @pl.when((pl.program_id(0) == 0) & (pl.program_id(1) == 0) & (pl.program_id(2) == 0))
    def _():
        scores(0, 0, None)

    def pair(t, carry):
        c0 = pl.multiple_of(2 * t * tk, tk)
        scores(c0 + tk, 1, None)
        update(c0, 0, None)
        scores(c0 + 2 * tk, 0, None)
        update(c0 + tk, 1, None)
        return carry

    lax.fori_loop(0, n_pairs, pair, 0)
    for d in range(n_diag):
        col0 = pl.multiple_of(q0 + d * tk, tk)
        if d + 1 < n_diag:
            scores(col0 + tk, (d + 1) % 2, (d + 1) * tk)
        else:
            scores_of(kn_ref[...], qtn_ref, 0, None)
        update(col0, d % 2, d * tk)

    lam = (jnp.exp(jnp.sum(lq1_ref[...] * lk1_ref[...], axis=1, keepdims=True))
           - jnp.exp(jnp.sum(lq2_ref[...] * lk2_ref[...], axis=1, keepdims=True)) + LAM_INIT)
    ot = acc_s[0] / l_s[0] - lam * (acc_s[1] / l_s[1])
    yt = ot * lax.rsqrt(jnp.mean(ot * ot, axis=0, keepdims=True) + SUBLN_EPS)
    o_ref[...] = ((yt.T * g_ref[...]) * (1.0 - LAM_INIT)).astype(o_ref.dtype)


def _attention(qt, k, vt, lq1, lk1, lq2, lk2, subln_g, batch, seq):
    tq = min(1024, seq)
    tk = min(512, seq)
    cq = min(256, tq)
    nq = seq // tq
    assert tq % (2 * tk) == 0 and seq % tq == 0
    n_tiles = batch * N_HEADS * nq

    def next_tile(b, h, i):
        flat = jnp.minimum((b * N_HEADS + h) * nq + i + 1, n_tiles - 1)
        return flat // (N_HEADS * nq), (flat // nq) % N_HEADS, flat % nq

    def next_q(b, h, i):
        nb, nh, ni = next_tile(b, h, i)
        return nh, nb * nq + ni

    def next_k(b, h, i):
        nb, nh, _ = next_tile(b, h, i)
        return nb * (seq // tk), nh

    vec = pl.BlockSpec((1, HEAD_DIM), lambda b, h, i: (0, 0))
    return pl.pallas_call(
        functools.partial(_attn_kernel, tk=tk, cq=cq),
        grid=(batch, N_HEADS, nq),
        in_specs=[pl.BlockSpec((V_DIM, tq), lambda b, h, i: (h, b * nq + i)),
                  pl.BlockSpec((seq, V_DIM), lambda b, h, i: (b, h)),
                  pl.BlockSpec((V_DIM, seq), lambda b, h, i: (h, b)),
                  pl.BlockSpec((V_DIM, tq), next_q),
                  pl.BlockSpec((tk, V_DIM), next_k),
                  vec, vec, vec, vec,
                  pl.BlockSpec((1, V_DIM), lambda b, h, i: (0, 0))],
        out_specs=pl.BlockSpec((tq, V_DIM), lambda b, h, i: (b * nq + i, h)),
        out_shape=jax.ShapeDtypeStruct((batch * seq, ATTN_W), _BF16),
        scratch_shapes=[pltpu.VMEM((2, 1, tq), _F32),
                        pltpu.VMEM((2, 1, tq), _F32),
                        pltpu.VMEM((2, V_DIM, tq), _F32),
                        pltpu.VMEM((2, 2, tk, tq), _F32),
                        pltpu.VMEM((2, 2, 1, tq), _F32)],
        compiler_params=_cparams(3),
        name="diff_attention",
    )(qt, k, vt, qt, k, lq1, lk1, lq2, lk2, subln_g)


def _merge_kernel(yr_ref, ya_ref, wr_ref, wa_ref, gr_ref, ga_ref, o_ref):
    pr = _dot(yr_ref[...], wr_ref[...])
    pa = _dot(ya_ref[...], wa_ref[...])
    o_ref[...] = (gr_ref[...].astype(_F32) * pr + ga_ref[...].astype(_F32) * pa).astype(o_ref.dtype)


def _merge(y_rnn, y_attn, w_r, w_a, gates):
    n, k = y_rnn.shape
    tm = min(512, n)
    tn = 1024
    n_tn = D_MODEL // tn
    return pl.pallas_call(
        _merge_kernel,
        grid=(n_tn, n // tm),
        in_specs=[pl.BlockSpec((tm, k), lambda j, i: (i, 0)),
                  pl.BlockSpec((tm, k), lambda j, i: (i, 0)),
                  pl.BlockSpec((k, tn), lambda j, i: (0, j)),
                  pl.BlockSpec((k, tn), lambda j, i: (0, j)),
                  pl.BlockSpec((tm, tn), lambda j, i: (i, j)),
                  pl.BlockSpec((tm, tn), lambda j, i: (i, n_tn + j))],
        out_specs=pl.BlockSpec((tm, tn), lambda j, i: (i, j)),
        out_shape=jax.ShapeDtypeStruct((n, D_MODEL), _BF16),
        compiler_params=_cparams(2),
        name="branch_merge",
    )(y_rnn, y_attn, w_r, w_a, gates, gates)


def _outproj_kernel(m_ref, w_ref, x_ref, g_ref, rhi_ref, rlo_ref, x2_ref, h2_ref, lt_ref):
    x2 = x_ref[...] + _dot(m_ref[...], w_ref[...])
    x2_ref[...] = x2
    h2 = (x2 * lax.rsqrt(jnp.mean(x2 * x2, axis=-1, keepdims=True) + NORM_EPS)) * g_ref[...]
    _store_token_tiles(h2_ref, _pack_halves(h2))
    h_hi = h2.astype(_BF16)
    h_lo = (h2 - h_hi.astype(_F32)).astype(_BF16)
    logits = _dot(h_hi, rhi_ref[...]) + (_dot(h_lo, rhi_ref[...]) + _dot(h_hi, rlo_ref[...]))
    lt_ref[...] = logits.T[:ROUTER_ROWS]


def _outproj(merged, w_out, x, g_ffn, w_router):
    n, d = x.shape
    tm = min(256, n)
    r_hi = w_router.astype(_BF16)
    r_lo = (w_router - r_hi.astype(_F32)).astype(_BF16)
    return pl.pallas_call(
        _outproj_kernel,
        grid=(n // tm,),
        in_specs=[pl.BlockSpec((tm, d), lambda i: (i, 0)),
                  pl.BlockSpec((d, d), lambda i: (0, 0)),
                  pl.BlockSpec((tm, d), lambda i: (i, 0)),
                  pl.BlockSpec((1, d), lambda i: (0, 0)),
                  pl.BlockSpec((d, 128), lambda i: (0, 0)),
                  pl.BlockSpec((d, 128), lambda i: (0, 0))],
        out_specs=[pl.BlockSpec((tm, d), lambda i: (i, 0)),
                   pl.BlockSpec((tm * TILE_ROWS, 128), lambda i: (i, 0)),
                   pl.BlockSpec((ROUTER_ROWS, tm), lambda i: (0, i))],
        out_shape=[jax.ShapeDtypeStruct((n, d), _F32),
                   jax.ShapeDtypeStruct((n * TILE_ROWS, 128), jnp.uint32),
                   jax.ShapeDtypeStruct((ROUTER_ROWS, n), _F32)],
        compiler_params=_cparams(1),
        name="outproj_norm_router",
    )(merged, w_out, x, g_ffn, r_hi, r_lo)


def _first_index_of_max(vals, n_rows):
    vmax = jnp.max(vals, axis=0, keepdims=True)
    rows = lax.broadcasted_iota(jnp.int32, vals.shape, 0).astype(_F32)
    idx = jnp.min(jnp.where(vals == vmax, rows, float(n_rows)), axis=0, keepdims=True)
    return vmax, idx


def _route_kernel(lt_ref, tri_ref, eid_ref, wts_ref, rank_ref, cnt_ref, carry):
    tl = lt_ref.shape[1]

    @pl.when(pl.program_id(0) == 0)
    def _():
        carry[...] = jnp.zeros(carry.shape, _F32)

    lt = lt_ref[...]
    g = lt[0:N_GROUPS]
    ge = jnp.exp(g - jnp.max(g, axis=0, keepdims=True))
    gp = ge / jnp.sum(ge, axis=0, keepdims=True)
    g_w, g_idx = _first_index_of_max(gp, N_GROUPS)
    el = jnp.zeros((EXPERTS_PER_GROUP, tl), _F32)
    for gg in range(N_GROUPS):
        el = jnp.where(g_idx == float(gg), lt[8 + gg * EXPERTS_PER_GROUP:8 + (gg + 1) * EXPERTS_PER_GROUP], el)
    ee = jnp.exp(el - jnp.max(el, axis=0, keepdims=True))
    ep = ee / jnp.sum(ee, axis=0, keepdims=True)
    v1, i1 = _first_index_of_max(ep, EXPERTS_PER_GROUP)
    rows8 = lax.broadcasted_iota(jnp.int32, ep.shape, 0).astype(_F32)
    v2, i2 = _first_index_of_max(jnp.where(rows8 == i1, -1.0, ep), EXPERTS_PER_GROUP)
    den = v1 + v2
    e1 = g_idx * float(EXPERTS_PER_GROUP) + i1
    e2 = g_idx * float(EXPERTS_PER_GROUP) + i2

    rows_e = lax.broadcasted_iota(jnp.int32, (N_EXPERTS, tl), 0).astype(_F32)
    ranks = []
    for e_sel in (e1, e2):
        onehot = rows_e == e_sel
        oh = jnp.where(onehot, 1.0, 0.0)
        before = jnp.dot(oh.astype(_BF16), tri_ref[...], preferred_element_type=_F32) + carry[...]
        ranks.append(jnp.sum(jnp.where(onehot, before, 0.0), axis=0, keepdims=True))
        carry[...] = carry[...] + jnp.sum(oh, axis=1, keepdims=True)

    zeros6 = jnp.zeros((6, tl), _F32)
    eid_ref[...] = jnp.concatenate([e1, e2, zeros6], axis=0).astype(jnp.int32)
    wts_ref[...] = jnp.concatenate([g_w * (v1 / den), g_w * (v2 / den), zeros6], axis=0)
    rank_ref[...] = jnp.concatenate(ranks + [zeros6], axis=0).astype(jnp.int32)
    cnt_ref[...] = jnp.broadcast_to(carry[...], cnt_ref.shape).astype(jnp.int32)


def _route(logits_t):
    n = logits_t.shape[1]
    tl = min(512, n)
    tri = jnp.asarray(np.triu(np.ones((tl, tl), np.float32), k=1), _BF16)
    row_block = pl.BlockSpec((8, tl), lambda i: (0, i))
    return pl.pallas_call(
        _route_kernel,
        grid=(n // tl,),
        in_specs=[pl.BlockSpec((ROUTER_ROWS, tl), lambda i: (0, i)),
                  pl.BlockSpec((tl, tl), lambda i: (0, 0))],
        out_specs=[row_block, row_block, row_block,
                   pl.BlockSpec((N_EXPERTS, 128), lambda i: (0, 0))],
        out_shape=[jax.ShapeDtypeStruct((8, n), jnp.int32),
                   jax.ShapeDtypeStruct((8, n), _F32),
                   jax.ShapeDtypeStruct((8, n), jnp.int32),
                   jax.ShapeDtypeStruct((N_EXPERTS, 128), jnp.int32)],
        scratch_shapes=[pltpu.VMEM((N_EXPERTS, 1), _F32)],
        compiler_params=_cparams(1),
        name="route_topk_rank",
    )(logits_t, tri)


def _row_copy(src_hbm, row, dst_vmem, r, sem):
    return pltpu.make_async_copy(src_hbm.at[_token_rows(row)], dst_vmem.at[_token_rows(r)], sem)


def _dispatch_kernel(dest_ref, cnt_ref, pstart_ref, pend_ref, h_ref, xs_hbm, rows, zero_rows, sem, zsem, *, n_steps):
    tm = h_ref.shape[0] // TILE_ROWS
    i = pl.program_id(0)
    n = n_steps * tm
    n_slots = xs_hbm.shape[0] // TILE_ROWS
    slot = i % 2
    n_jobs = 2 * N_EXPERTS
    jobs_per_step = -(-n_jobs // n_steps)

    def pad_job(j, wait):
        e = jnp.minimum(j, N_EXPERTS - 1)
        tail_lo = pend_ref[N_EXPERTS - 1] + (j - N_EXPERTS) * SLOT_BLOCK
        lo = jnp.where(j < N_EXPERTS, pstart_ref[e] + cnt_ref[e], tail_lo)
        hi = jnp.where(j < N_EXPERTS, pend_ref[e], jnp.minimum(tail_lo + SLOT_BLOCK, n_slots))

        def body(r, c):
            cp = pltpu.make_async_copy(zero_rows, xs_hbm.at[_token_rows(r)], zsem)
            cp.wait() if wait else cp.start()
            return c
        lax.fori_loop(lo, jnp.maximum(lo, hi), body, 0)

    def pad_jobs(wait):
        for jj in range(jobs_per_step):
            j = i * jobs_per_step + jj

            @pl.when(j < n_jobs)
            def _():
                pad_job(j, wait)

    @pl.when(i == 0)
    def _():
        zero_rows[...] = jnp.zeros(zero_rows.shape, zero_rows.dtype)

    pad_jobs(False)

    def row_copy(tile_slot, r, row):
        return pltpu.make_async_copy(rows.at[tile_slot, _token_rows(r)], xs_hbm.at[_token_rows(row)], sem.at[tile_slot])

    def drain(tile_slot):
        def body(r, c):
            for _ in range(TOP_K):
                row_copy(tile_slot, r, 0).wait()
            return c
        lax.fori_loop(0, tm, body, 0, unroll=8)

    @pl.when(i >= 2)
    def _():
        drain(slot)

    rows[slot] = h_ref[...]

    def issue(r, c):
        for k in range(TOP_K):
            row_copy(slot, r, dest_ref[k * n + i * tm + r]).start(priority=k)
        return c

    lax.fori_loop(0, tm, issue, 0, unroll=8)
    pad_jobs(True)

    @pl.when(i == n_steps - 1)
    def _():
        if n_steps >= 2:
            drain(1 - slot)
        drain(slot)


def _dispatch(dest, counts, pstarts, pends, h2, n_slots):
    n = h2.shape[0] // TILE_ROWS
    tm = min(256, n)
    return pl.pallas_call(
        functools.partial(_dispatch_kernel, n_steps=n // tm),
        grid_spec=pltpu.PrefetchScalarGridSpec(
            num_scalar_prefetch=4,
            grid=(n // tm,),
            in_specs=[pl.BlockSpec((tm * TILE_ROWS, 128), lambda i, *_: (i, 0))],
            out_specs=pl.BlockSpec(memory_space=pl.ANY),
            scratch_shapes=[pltpu.VMEM((2, tm * TILE_ROWS, 128), h2.dtype), pltpu.VMEM((TILE_ROWS, 128), h2.dtype),
                            pltpu.SemaphoreType.DMA((2,)), pltpu.SemaphoreType.DMA]),
        out_shape=jax.ShapeDtypeStruct((n_slots * TILE_ROWS, 128), h2.dtype),
        compiler_params=_cparams(1),
        name="slot_dispatch",
    )(dest, counts, pstarts, pends, h2)


def _expert_kernel(be_ref, first_ref, slot_ref, nxt_ref, x_ref, wg_hbm, wu_hbm, wd_hbm, o_ref,
                   wg_buf, wu_buf, wd_buf, sem):
    i = pl.program_id(0)

    def weight_copies(e, s):
        return (pltpu.make_async_copy(wg_hbm.at[e], wg_buf.at[s], sem.at[s, 0]),
                pltpu.make_async_copy(wu_hbm.at[e], wu_buf.at[s], sem.at[s, 1]),
                pltpu.make_async_copy(wd_hbm.at[e], wd_buf.at[s], sem.at[s, 2]))

    @pl.when(i == 0)
    def _():
        for cp in weight_copies(be_ref[0], 0):
            cp.start()

    s = slot_ref[i]

    @pl.when(first_ref[i] == 1)
    def _():
        for cp in weight_copies(be_ref[i], s):
            cp.wait()

        @pl.when(nxt_ref[i] >= 0)
        def _():
            for cp in weight_copies(nxt_ref[i], 1 - s):
                cp.start()

    x_lo, x_hi = _unpack_halves(_load_token_tiles(x_ref))
    g = _dot(x_lo, wg_buf[s, :HALF]) + _dot(x_hi, wg_buf[s, HALF:])
    u = _dot(x_lo, wu_buf[s, :HALF]) + _dot(x_hi, wu_buf[s, HALF:])
    mid = (jax.nn.silu(g) * u).astype(_BF16)
    _store_token_tiles(o_ref, _pack_halves(_dot(mid, wd_buf[s])))


def _experts(blk_e, xs, w_gate, w_up, w_down):
    p = xs.shape[0] // TILE_ROWS
    d = w_gate.shape[1]
    nb = p // SLOT_BLOCK
    first = jnp.concatenate([jnp.ones((1,), jnp.int32), (blk_e[1:] != blk_e[:-1]).astype(jnp.int32)])
    slot = (jnp.cumsum(first) - 1) % 2
    later = blk_e[None, :] > blk_e[:, None]
    nxt = jnp.min(jnp.where(later, blk_e[None, :], N_EXPERTS), axis=1)
    nxt = jnp.where(nxt < N_EXPERTS, nxt, -1).astype(jnp.int32)
    any_spec = pl.BlockSpec(memory_space=pl.ANY)
    return pl.pallas_call(
        _expert_kernel,
        grid_spec=pltpu.PrefetchScalarGridSpec(
            num_scalar_prefetch=4,
            grid=(nb,),
            in_specs=[pl.BlockSpec((SLOT_BLOCK * TILE_ROWS, 128), lambda i, *_: (i, 0)), any_spec, any_spec, any_spec],
            out_specs=pl.BlockSpec((SLOT_BLOCK * TILE_ROWS, 128), lambda i, *_: (i, 0)),
            scratch_shapes=[pltpu.VMEM((2, d, D_EXPERT), _F32),
                            pltpu.VMEM((2, d, D_EXPERT), _F32),
                            pltpu.VMEM((2, D_EXPERT, d), _F32),
                            pltpu.SemaphoreType.DMA((2, 3))]),
        out_shape=jax.ShapeDtypeStruct((p * TILE_ROWS, 128), jnp.uint32),
        compiler_params=pltpu.CompilerParams(dimension_semantics=("arbitrary",),
                                             vmem_limit_bytes=EXPERT_VMEM_LIMIT),
        name="expert_mlp",
    )(blk_e, first, slot.astype(jnp.int32), nxt, xs, w_gate, w_up, w_down)


def _combine_kernel(dest_ref, x2_ref, w0_ref, w1_ref, g_ref, ys_hbm, o_ref, rows, sem):
    tm = x2_ref.shape[0]
    n = pl.num_programs(0) * tm
    i = pl.program_id(0)

    def issue(tile, slot):
        def body(r, c):
            for k in range(TOP_K):
                _row_copy(ys_hbm, dest_ref[k * n + tile * tm + r], rows.at[slot, k], r, sem.at[slot]).start(priority=k)
            return c
        lax.fori_loop(0, tm, body, 0, unroll=8)

    @pl.when(i == 0)
    def _():
        issue(0, 0)

    @pl.when(i + 1 < pl.num_programs(0))
    def _():
        issue(i + 1, (i + 1) % 2)

    slot = i % 2

    def drain(r, c):
        for k in range(TOP_K):
            _row_copy(ys_hbm, 0, rows.at[slot, k], r, sem.at[slot]).wait()
        return c

    lax.fori_loop(0, tm, drain, 0, unroll=8)
    y0 = _unpack_halves(_load_token_tiles(rows.at[slot, 0]))
    y1 = _unpack_halves(_load_token_tiles(rows.at[slot, 1]))
    w0 = w0_ref[...]
    w1 = w1_ref[...]
    halves = (slice(0, HALF), slice(HALF, 2 * HALF))
    x3 = [x2_ref[:, cols] + (w0 * a + w1 * b) for cols, a, b in zip(halves, y0, y1)]
    ms = (jnp.sum(x3[0] * x3[0], axis=-1, keepdims=True) + jnp.sum(x3[1] * x3[1], axis=-1, keepdims=True)) / (2 * HALF)
    inv = lax.rsqrt(ms + NORM_EPS)
    for cols, v in zip(halves, x3):
        o_ref[:, cols] = (v * inv) * g_ref[:, cols]


def _combine(dest, x2, w0, w1, g_final, ys):
    n, d = x2.shape
    tm = min(256, n)
    return pl.pallas_call(
        _combine_kernel,
        grid_spec=pltpu.PrefetchScalarGridSpec(
            num_scalar_prefetch=1,
            grid=(n // tm,),
            in_specs=[pl.BlockSpec((tm, d), lambda i, dst: (i, 0)),
                      pl.BlockSpec((tm, 1), lambda i, dst: (i, 0)),
                      pl.BlockSpec((tm, 1), lambda i, dst: (i, 0)),
                      pl.BlockSpec((1, d), lambda i, dst: (0, 0)),
                      pl.BlockSpec(memory_space=pl.ANY)],
            out_specs=pl.BlockSpec((tm, d), lambda i, dst: (i, 0)),
            scratch_shapes=[pltpu.VMEM((2, TOP_K, tm * TILE_ROWS, 128), jnp.uint32), pltpu.SemaphoreType.DMA((2,))]),
        out_shape=jax.ShapeDtypeStruct((n, d), _F32),
        compiler_params=_cparams(1),
        name="combine_final_norm",
    )(dest, x2, w0, w1, g_final, ys)


def kernel(x, positions, g_mix, w_in, conv_w, conv_b, w_rg_a, b_rg_a, w_rg_x, b_rg_x, lru_param, lambda_q1, lambda_k1, lambda_q2, lambda_k2, subln_g, w_br_rnn, w_br_attn, w_out, g_ffn, w_grp_router, w_exp_router, w_gate, w_up, w_down, g_final):
    batch, seq, d = x.shape
    n = batch * seq
    xf = x.reshape(n, d)
    pos_col = positions.reshape(n, 1)
    row = lambda v: v.reshape(1, -1)

    w_rg = jnp.concatenate([w_rg_a[0], w_rg_x[0]], axis=-1).astype(_BF16)
    w_router = jnp.concatenate(
        [w_grp_router[0], jnp.zeros((d, 8 - N_GROUPS), _F32), w_exp_router[0],
         jnp.zeros((d, 128 - ROUTER_ROWS), _F32)], axis=1)

    h = _rmsnorm(xf, row(g_mix[0]), _BF16)
    cos_t, sin_t = _rope_tables(pos_col)

    c_qk = 2 * D_RNN
    c_v = c_qk + 2 * QK_W
    c_g = c_v + ATTN_W
    u = _projection(_proj_plain_kernel, h, w_in[0], 0, D_RNN, _F32, name="proj_rnn")
    gg = _projection(_proj_gelu_kernel, h, w_in[0], D_RNN, D_RNN, _F32, name="proj_rnn_gate")
    q_scale = (HEAD_DIM ** -0.5) * LOG2E
    qt = _projection(functools.partial(_proj_rope_kernel, scale=q_scale, transpose_out=True),
                     h, w_in[0], c_qk, QK_W, _BF16, extra=(cos_t, sin_t), transpose_out=True, name="proj_q_rope")
    k = _projection(functools.partial(_proj_rope_kernel, scale=1.0, transpose_out=False),
                    h, w_in[0], c_qk + QK_W, QK_W, _BF16, extra=(cos_t, sin_t), name="proj_k_rope")
    vt = _projection(_proj_t_kernel, h, w_in[0], c_v, ATTN_W, _BF16, transpose_out=True, name="proj_v")
    gates = _projection(_proj_sigmoid_kernel, h, w_in[0], c_g, 2 * D_MODEL, _BF16, name="proj_gates")

    y_rnn = _lru(pos_col, u, gg, conv_w[0], row(conv_b[0]), w_rg, row(b_rg_a[0]), row(b_rg_x[0]),
                 row(lru_param[0]), batch, seq)
    y_attn = _attention(qt, k, vt, row(lambda_q1[0]), row(lambda_k1[0]), row(lambda_q2[0]), row(lambda_k2[0]),
                        row(subln_g[0]), batch, seq)
    merged = _merge(y_rnn, y_attn, w_br_rnn[0], w_br_attn[0], gates)
    x2, h2, logits_t = _outproj(merged, w_out[0], xf, row(g_ffn[0]), w_router)

    eid8, wts8, rank8, cnt = _route(logits_t)
    eid, rank, counts = eid8[:TOP_K], rank8[:TOP_K], cnt[:, 0]

    pcounts = ((counts + SLOT_BLOCK - 1) // SLOT_BLOCK) * SLOT_BLOCK
    pends = jnp.cumsum(pcounts)
    pstarts = pends - pcounts
    experts = jnp.arange(N_EXPERTS, dtype=jnp.int32)
    dest = jnp.sum(jnp.where(eid[:, :, None] == experts, pstarts, 0), axis=-1) + rank
    n_slots = n * TOP_K + N_EXPERTS * SLOT_BLOCK
    blk_start = jnp.arange(n_slots // SLOT_BLOCK, dtype=jnp.int32) * SLOT_BLOCK
    blk_e = jnp.minimum(jnp.sum((blk_start[:, None] >= pends[None, :]).astype(jnp.int32), axis=1), N_EXPERTS - 1)

    dest_flat = dest.reshape(-1)
    xs = _dispatch(dest_flat, counts, pstarts, pends, h2, n_slots)
    ys = _experts(blk_e, xs, w_gate[0], w_up[0], w_down[0])
    out = _combine(dest_flat, x2, wts8[0].reshape(n, 1), wts8[1].reshape(n, 1), row(g_final), ys)
    return out.reshape(batch, seq, d)
```

```python
import functools
import math

import numpy as np
import jax
import jax.numpy as jnp
from jax import lax
from jax.experimental import pallas as pl
from jax.experimental.pallas import tpu as pltpu

D_MODEL = 2048
D_RNN = D_MODEL
LRU_BLOCK = 128
N_LRU_BLOCKS = D_RNN // LRU_BLOCK
CONV_W = 4
C_LRU = 8.0
HEAD_DIM = 128
N_HEADS = D_MODEL // (2 * HEAD_DIM)
V_DIM = 2 * HEAD_DIM
QK_W = N_HEADS * 2 * HEAD_DIM
ATTN_W = N_HEADS * V_DIM
ROPE_THETA = 10000.0
SUBLN_EPS = 1e-5
N_GROUPS = 4
EXPERTS_PER_GROUP = 8
N_EXPERTS = N_GROUPS * EXPERTS_PER_GROUP
TOP_K = 2
D_EXPERT = D_MODEL // 2
NORM_EPS = 1e-6
LAM_INIT = 0.8 - 0.6 * math.exp(-0.3 * 0)

SLOT_BLOCK = 256
HALF = D_MODEL // 2
TILE_ROWS = HALF // 128
ROUTER_ROWS = 8 + N_EXPERTS
MASK_VALUE = -1e30
LOG2E = 1.4426950408889634
VMEM_LIMIT = 56 * 1024 * 1024
EXPERT_VMEM_LIMIT = 60 * 1024 * 1024

_F32 = jnp.float32
_BF16 = jnp.bfloat16


def _dot(a, b):
    return lax.dot_general(a, b, (((1,), (0,)), ((), ())), preferred_element_type=_F32)


def _pack_halves(y):
    bits = lax.bitcast_convert_type(y.astype(_BF16).astype(_F32), jnp.uint32)
    return (bits[:, :HALF] >> 16) | bits[:, HALF:]


def _unpack_halves(p):
    lo = lax.bitcast_convert_type(p << 16, _F32)
    hi = lax.bitcast_convert_type(p & jnp.uint32(0xFFFF0000), _F32)
    return lo, hi


def _token_rows(t):
    return pl.ds(pl.multiple_of(t * TILE_ROWS, TILE_ROWS), TILE_ROWS)


def _store_token_tiles(ref, packed):
    tokens = packed.shape[0]
    for j in range(TILE_ROWS):
        ref[pl.ds(j, tokens, stride=TILE_ROWS), :] = packed[:, j * 128:(j + 1) * 128]


def _load_token_tiles(ref):
    tokens = ref.shape[0] // TILE_ROWS
    return jnp.concatenate([ref[pl.ds(j, tokens, stride=TILE_ROWS), :] for j in range(TILE_ROWS)], axis=1)


def _cparams(n_axes):
    return pltpu.CompilerParams(dimension_semantics=("arbitrary",) * n_axes,
                                vmem_limit_bytes=VMEM_LIMIT)


def _rmsnorm_kernel(x_ref, g_ref, o_ref):
    x = x_ref[...]
    y = x * lax.rsqrt(jnp.mean(x * x, axis=-1, keepdims=True) + NORM_EPS)
    o_ref[...] = (y * g_ref[...]).astype(o_ref.dtype)


def _rmsnorm(x, g, out_dtype):
    n, d = x.shape
    tm = min(512, n)
    return pl.pallas_call(
        _rmsnorm_kernel,
        grid=(n // tm,),
        in_specs=[pl.BlockSpec((tm, d), lambda i: (i, 0)),
                  pl.BlockSpec((1, d), lambda i: (0, 0))],
        out_specs=pl.BlockSpec((tm, d), lambda i: (i, 0)),
        out_shape=jax.ShapeDtypeStruct((n, d), out_dtype),
        compiler_params=_cparams(1),
        name="rmsnorm",
    )(x, g)


def _rope_table_kernel(pos_ref, inv_ref, sign_ref, cos_ref, sin_ref):
    ang = pos_ref[...].astype(_F32) * inv_ref[...]
    cos_ref[...] = jnp.cos(ang)
    sin_ref[...] = jnp.sin(ang) * sign_ref[...]


def _rope_tables(pos_col):
    n = pos_col.shape[0]
    tm = min(1024, n)
    inv = 1.0 / (ROPE_THETA ** (np.arange(0, HEAD_DIM, 2, dtype=np.float32) / np.float32(HEAD_DIM)))
    inv = np.concatenate([inv, inv]).astype(np.float32)[None, :]
    sign = np.concatenate([-np.ones(HEAD_DIM // 2), np.ones(HEAD_DIM // 2)]).astype(np.float32)[None, :]
    return pl.pallas_call(
        _rope_table_kernel,
        grid=(n // tm,),
        in_specs=[pl.BlockSpec((tm, 1), lambda i: (i, 0)),
                  pl.BlockSpec((1, HEAD_DIM), lambda i: (0, 0)),
                  pl.BlockSpec((1, HEAD_DIM), lambda i: (0, 0))],
        out_specs=[pl.BlockSpec((tm, HEAD_DIM), lambda i: (i, 0))] * 2,
        out_shape=[jax.ShapeDtypeStruct((n, HEAD_DIM), _F32)] * 2,
        compiler_params=_cparams(1),
        name="rope_tables",
    )(pos_col, jnp.asarray(inv), jnp.asarray(sign))


def _proj_plain_kernel(h_ref, w_ref, o_ref):
    o_ref[...] = _dot(h_ref[...], w_ref[...]).astype(o_ref.dtype)


def _proj_gelu_kernel(h_ref, w_ref, o_ref):
    o_ref[...] = jax.nn.gelu(_dot(h_ref[...], w_ref[...]), approximate=True).astype(o_ref.dtype)


def _proj_sigmoid_kernel(h_ref, w_ref, o_ref):
    acc = _dot(h_ref[...], w_ref[...])
    o_ref[...] = (0.5 * jnp.tanh(0.5 * acc) + 0.5).astype(o_ref.dtype)


def _proj_t_kernel(h_ref, w_ref, o_ref):
    acc = _dot(h_ref[...], w_ref[...])
    for g in range(acc.shape[1] // HEAD_DIM):
        o_ref[g * HEAD_DIM:(g + 1) * HEAD_DIM, :] = acc[:, g * HEAD_DIM:(g + 1) * HEAD_DIM].T.astype(o_ref.dtype)


def _proj_rope_kernel(h_ref, w_ref, cos_ref, sin_ref, o_ref, *, scale, transpose_out):
    acc = _dot(h_ref[...], w_ref[...])
    cos = cos_ref[...]
    sin = sin_ref[...]
    for g in range(acc.shape[1] // HEAD_DIM):
        t = acc[:, g * HEAD_DIM:(g + 1) * HEAD_DIM]
        r = (t * cos + pltpu.roll(t, HEAD_DIM // 2, axis=1) * sin) * scale
        if transpose_out:
            o_ref[g * HEAD_DIM:(g + 1) * HEAD_DIM, :] = r.T.astype(o_ref.dtype)
        else:
            o_ref[:, g * HEAD_DIM:(g + 1) * HEAD_DIM] = r.astype(o_ref.dtype)


def _projection(kernel_fn, h, w, col_off, n_cols, out_dtype, extra=(), transpose_out=False, name="proj"):
    n, k = h.shape
    tm = min(1024, n)
    tn = 1024
    off = col_off // tn
    extra_specs = [pl.BlockSpec((tm, e.shape[1]), lambda j, i: (i, 0)) for e in extra]
    if transpose_out:
        out_spec = pl.BlockSpec((tn, tm), lambda j, i: (j, i))
        out_shape = jax.ShapeDtypeStruct((n_cols, n), out_dtype)
    else:
        out_spec = pl.BlockSpec((tm, tn), lambda j, i: (i, j))
        out_shape = jax.ShapeDtypeStruct((n, n_cols), out_dtype)
    return pl.pallas_call(
        kernel_fn,
        grid=(n_cols // tn, n // tm),
        in_specs=[pl.BlockSpec((tm, k), lambda j, i: (i, 0)),
                  pl.BlockSpec((k, tn), lambda j, i: (0, j + off))] + extra_specs,
        out_specs=out_spec,
        out_shape=out_shape,
        compiler_params=_cparams(2),
        name=name,
    )(h, w, *extra)


def _lru_kernel(pos_ref, u_ref, gg_ref, cw_ref, cb_ref, wrg_ref, ba_ref, bx_ref, lp_ref, o_ref,
                ubuf, hcar, a_s, b_s, h_s):
    t_rows, cb = u_ref.shape
    n_grp = t_rows // 8

    @pl.when(pl.program_id(2) == 0)
    def _():
        ubuf[0:8, :] = jnp.zeros((8, cb), _F32)
        hcar[...] = jnp.zeros((8, cb), _F32)

    ubuf[8:8 + t_rows, :] = u_ref[...]
    cw = cw_ref[...]
    uc = cb_ref[...] + cw[0:1] * ubuf[5:5 + t_rows, :]
    uc = uc + cw[1:2] * ubuf[6:6 + t_rows, :]
    uc = uc + cw[2:3] * ubuf[7:7 + t_rows, :]
    uc = uc + cw[3:4] * ubuf[8:8 + t_rows, :]
    ubuf[0:8, :] = ubuf[t_rows:t_rows + 8, :]

    ucb = uc.astype(_BF16)
    r_parts, i_parts = [], []
    for j in range(cb // LRU_BLOCK):
        g = jnp.dot(ucb[:, j * LRU_BLOCK:(j + 1) * LRU_BLOCK], wrg_ref[j], preferred_element_type=_F32)
        r_parts.append(g[:, :LRU_BLOCK])
        i_parts.append(g[:, LRU_BLOCK:])
    t_r = jnp.tanh(jnp.concatenate(r_parts, axis=1) + ba_ref[...])
    gi = 0.5 * jnp.tanh(jnp.concatenate(i_parts, axis=1) + bx_ref[...]) + 0.5

    half_c_sp = (-0.5 * C_LRU) * jax.nn.softplus(-lp_ref[...])
    log_a = half_c_sp * t_r + half_c_sp
    a_raw = jnp.exp(log_a)
    m2 = 1.0 - a_raw * a_raw
    mult = jnp.where(m2 > 0.0, m2 * lax.rsqrt(m2), 0.0)
    reset = pos_ref[...] == 0
    a = jnp.where(reset, 0.0, a_raw)
    mult = jnp.where(reset, 1.0, mult)
    bv = uc * gi * mult

    a3 = a.reshape(n_grp, 8, cb)
    b3 = bv.reshape(n_grp, 8, cb)
    row = lax.broadcasted_iota(jnp.int32, (n_grp, 8, cb), 1)
    for d in (1, 2, 4):
        a_sh = pltpu.roll(a3, d, axis=1)
        b_sh = pltpu.roll(b3, d, axis=1)
        take = row >= d
        b3 = jnp.where(take, a3 * b_sh + b3, b3)
        a3 = jnp.where(take, a3 * a_sh, a3)
    a_s[...] = a3
    b_s[...] = b3

    def carry_step(g, carry):
        h = b_s[g] + a_s[g] * carry
        h_s[g] = h
        return jnp.broadcast_to(h[7:8, :], (8, cb))

    hcar[...] = lax.fori_loop(0, n_grp, carry_step, hcar[...], unroll=8)

    hr = h_s[...].reshape(t_rows, cb)
    o_ref[...] = (hr * gg_ref[...]).astype(o_ref.dtype)


def _lru(pos_col, u, gg, conv_w, conv_b, w_rg, b_a, b_x, lru_param, batch, seq):
    cb = 512
    t_rows = min(512, seq)
    n_cb = D_RNN // cb
    n_t = seq // t_rows
    row_map = lambda b, c, s: (b * n_t + s, 0)
    return pl.pallas_call(
        _lru_kernel,
        grid=(batch, n_cb, n_t),
        in_specs=[pl.BlockSpec((t_rows, 1), row_map),
                  pl.BlockSpec((t_rows, cb), lambda b, c, s: (b * n_t + s, c)),
                  pl.BlockSpec((t_rows, cb), lambda b, c, s: (b * n_t + s, c)),
                  pl.BlockSpec((CONV_W, cb), lambda b, c, s: (0, c)),
                  pl.BlockSpec((1, cb), lambda b, c, s: (0, c)),
                  pl.BlockSpec((cb // LRU_BLOCK, LRU_BLOCK, 2 * LRU_BLOCK), lambda b, c, s: (c, 0, 0)),
                  pl.BlockSpec((1, cb), lambda b, c, s: (0, c)),
                  pl.BlockSpec((1, cb), lambda b, c, s: (0, c)),
                  pl.BlockSpec((1, cb), lambda b, c, s: (0, c))],
        out_specs=pl.BlockSpec((t_rows, cb), lambda b, c, s: (b * n_t + s, c)),
        out_shape=jax.ShapeDtypeStruct((batch * seq, D_RNN), _BF16),
        scratch_shapes=[pltpu.VMEM((t_rows + 8, cb), _F32),
                        pltpu.VMEM((8, cb), _F32),
                        pltpu.VMEM((t_rows // 8, 8, cb), _F32),
                        pltpu.VMEM((t_rows // 8, 8, cb), _F32),
                        pltpu.VMEM((t_rows // 8, 8, cb), _F32)],
        compiler_params=_cparams(3),
        name="conv_rglru",
    )(pos_col, u, gg, conv_w, conv_b, w_rg, b_a, b_x, lru_param)


def _attn_kernel(qt_ref, k_ref, vt_ref, qtn_ref, kn_ref, lq1_ref, lk1_ref, lq2_ref, lk2_ref, g_ref, o_ref,
                 m_s, l_s, acc_s, st_s, mx_s, *, tk, cq):
    tq = qt_ref.shape[1]
    q0 = pl.program_id(2) * tq
    n_chunks = tq // cq
    n_diag = tq // tk

    m_s[...] = jnp.full(m_s.shape, MASK_VALUE, _F32)
    l_s[...] = jnp.zeros(l_s.shape, _F32)
    acc_s[...] = jnp.zeros(acc_s.shape, _F32)

    def chunks(diag):
        return [c for c in range(n_chunks) if diag is None or (c + 1) * cq > diag]

    def scores_of(k_blk, q_ref, slot, diag):
        for c in chunks(diag):
            cols = slice(c * cq, (c + 1) * cq)
            for mp in range(2):
                dims = slice(mp * HEAD_DIM, (mp + 1) * HEAD_DIM)
                st = jnp.dot(k_blk[:, dims], q_ref[dims, cols], preferred_element_type=_F32)
                st_s[slot, mp, :, cols] = st
                mx_s[slot, mp, :, cols] = jnp.max(st, axis=0, keepdims=True)

    def scores(col0, slot, diag):
        scores_of(k_ref[pl.ds(col0, tk), :], qt_ref, slot, diag)

    def update(col0, slot, diag):
        vt_blk = jnp.concatenate([vt_ref[:, pl.ds(col0, tk)], jnp.ones((16, tk), _BF16)], axis=0)
        for c in chunks(diag):
            masked = diag is not None and c * cq < diag + tk - 1
            cols = slice(c * cq, (c + 1) * cq)
            for mp in range(2):
                st = st_s[slot, mp, :, cols]
                if masked:
                    kv_pos = diag + lax.broadcasted_iota(jnp.int32, (tk, cq), 0)
                    q_pos = c * cq + lax.broadcasted_iota(jnp.int32, (tk, cq), 1)
                    st = jnp.where(kv_pos <= q_pos, st, MASK_VALUE)
                    m_cur = jnp.max(st, axis=0, keepdims=True)
                else:
                    m_cur = mx_s[slot, mp, :, cols]
                m_prev = m_s[mp, :, cols]
                m_new = jnp.maximum(m_prev, m_cur)
                alpha = jnp.exp2(m_prev - m_new)
                pt = jnp.exp2(st - m_new).astype(_BF16)
                pv = jnp.dot(vt_blk, pt, preferred_element_type=_F32)
                l_s[mp, :, cols] = alpha * l_s[mp, :, cols] + pv[V_DIM:V_DIM + 1]
                acc_s[mp, :, cols] = alpha * acc_s[mp, :, cols] + pv[:V_DIM]
                m_s[mp, :, cols] = m_new

    n_pairs = q0 // (2 * tk)

    @pl.when((pl.program_id(0) == 0) & (pl.program_id(1) == 0) & (pl.program_id(2) == 0))
    def _():
        scores(0, 0, None)

    def pair(t, carry):
        c0 = pl.multiple_of(2 * t * tk, tk)
        scores(c0 + tk, 1, None)
        update(c0, 0, None)
        scores(c0 + 2 * tk, 0, None)
        update(c0 + tk, 1, None)
        return carry

    lax.fori_loop(0, n_pairs, pair, 0)
    for d in range(n_diag):
        col0 = pl.multiple_of(q0 + d * tk, tk)
        if d + 1 < n_diag:
            scores(col0 + tk, (d + 1) % 2, (d + 1) * tk)
        else:
            scores_of(kn_ref[...], qtn_ref, 0, None)
        update(col0, d % 2, d * tk)

    lam = (jnp.exp(jnp.sum(lq1_ref[...] * lk1_ref[...], axis=1, keepdims=True))
           - jnp.exp(jnp.sum(lq2_ref[...] * lk2_ref[...], axis=1, keepdims=True)) + LAM_INIT)
    ot = acc_s[0] / l_s[0] - lam * (acc_s[1] / l_s[1])
    yt = ot * lax.rsqrt(jnp.mean(ot * ot, axis=0, keepdims=True) + SUBLN_EPS)
    o_ref[...] = ((yt.T * g_ref[...]) * (1.0 - LAM_INIT)).astype(o_ref.dtype)


def _attention(qt, k, vt, lq1, lk1, lq2, lk2, subln_g, batch, seq):
    tq = min(1024, seq)
    tk = min(512, seq)
    cq = min(256, tq)
    nq = seq // tq
    assert tq % (2 * tk) == 0 and seq % tq == 0
    n_tiles = batch * N_HEADS * nq

    def next_tile(b, h, i):
        flat = jnp.minimum((b * N_HEADS + h) * nq + i + 1, n_tiles - 1)
        return flat // (N_HEADS * nq), (flat // nq) % N_HEADS, flat % nq

    def next_q(b, h, i):
        nb, nh, ni = next_tile(b, h, i)
        return nh, nb * nq + ni

    def next_k(b, h, i):
        nb, nh, _ = next_tile(b, h, i)
        return nb * (seq // tk), nh

    vec = pl.BlockSpec((1, HEAD_DIM), lambda b, h, i: (0, 0))
    return pl.pallas_call(
        functools.partial(_attn_kernel, tk=tk, cq=cq),
        grid=(batch, N_HEADS, nq),
        in_specs=[pl.BlockSpec((V_DIM, tq), lambda b, h, i: (h, b * nq + i)),
                  pl.BlockSpec((seq, V_DIM), lambda b, h, i: (b, h)),
                  pl.BlockSpec((V_DIM, seq), lambda b, h, i: (h, b)),
                  pl.BlockSpec((V_DIM, tq), next_q),
                  pl.BlockSpec((tk, V_DIM), next_k),
                  vec, vec, vec, vec,
                  pl.BlockSpec((1, V_DIM), lambda b, h, i: (0, 0))],
        out_specs=pl.BlockSpec((tq, V_DIM), lambda b, h, i: (b * nq + i, h)),
        out_shape=jax.ShapeDtypeStruct((batch * seq, ATTN_W), _BF16),
        scratch_shapes=[pltpu.VMEM((2, 1, tq), _F32),
                        pltpu.VMEM((2, 1, tq), _F32),
                        pltpu.VMEM((2, V_DIM, tq), _F32),
                        pltpu.VMEM((2, 2, tk, tq), _F32),
                        pltpu.VMEM((2, 2, 1, tq), _F32)],
        compiler_params=_cparams(3),
        name="diff_attention",
    )(qt, k, vt, qt, k, lq1, lk1, lq2, lk2, subln_g)


def _merge_kernel(yr_ref, ya_ref, wr_ref, wa_ref, gr_ref, ga_ref, o_ref):
    pr = _dot(yr_ref[...], wr_ref[...])
    pa = _dot(ya_ref[...], wa_ref[...])
    o_ref[...] = (gr_ref[...].astype(_F32) * pr + ga_ref[...].astype(_F32) * pa).astype(o_ref.dtype)


def _merge(y_rnn, y_attn, w_r, w_a, gates):
    n, k = y_rnn.shape
    tm = min(512, n)
    tn = 1024
    n_tn = D_MODEL // tn
    return pl.pallas_call(
        _merge_kernel,
        grid=(n_tn, n // tm),
        in_specs=[pl.BlockSpec((tm, k), lambda j, i: (i, 0)),
                  pl.BlockSpec((tm, k), lambda j, i: (i, 0)),
                  pl.BlockSpec((k, tn), lambda j, i: (0, j)),
                  pl.BlockSpec((k, tn), lambda j, i: (0, j)),
                  pl.BlockSpec((tm, tn), lambda j, i: (i, j)),
                  pl.BlockSpec((tm, tn), lambda j, i: (i, n_tn + j))],
        out_specs=pl.BlockSpec((tm, tn), lambda j, i: (i, j)),
        out_shape=jax.ShapeDtypeStruct((n, D_MODEL), _BF16),
        compiler_params=_cparams(2),
        name="branch_merge",
    )(y_rnn, y_attn, w_r, w_a, gates, gates)


def _outproj_kernel(m_ref, w_ref, x_ref, g_ref, r_ref, x2_ref, h2_ref, lt_ref):
    x2 = x_ref[...] + _dot(m_ref[...], w_ref[...])
    x2_ref[...] = x2
    h2 = (x2 * lax.rsqrt(jnp.mean(x2 * x2, axis=-1, keepdims=True) + NORM_EPS)) * g_ref[...]
    _store_token_tiles(h2_ref, _pack_halves(h2))
    h_hi = h2.astype(_BF16)
    h_lo = (h2 - h_hi.astype(_F32)).astype(_BF16)
    hi = _dot(h_hi, r_ref[...])
    logits = hi[:, :128] + (_dot(h_lo, r_ref[:, :128]) + hi[:, 128:])
    lt_ref[...] = logits.T[:ROUTER_ROWS]


def _outproj(merged, w_out, x, g_ffn, w_router):
    n, d = x.shape
    tm = min(256, n)
    r_hi = w_router.astype(_BF16)
    r_cat = jnp.concatenate([r_hi, (w_router - r_hi.astype(_F32)).astype(_BF16)], axis=1)
    return pl.pallas_call(
        _outproj_kernel,
        grid=(n // tm,),
        in_specs=[pl.BlockSpec((tm, d), lambda i: (i, 0)),
                  pl.BlockSpec((d, d), lambda i: (0, 0)),
                  pl.BlockSpec((tm, d), lambda i: (i, 0)),
                  pl.BlockSpec((1, d), lambda i: (0, 0)),
                  pl.BlockSpec((d, 256), lambda i: (0, 0))],
        out_specs=[pl.BlockSpec((tm, d), lambda i: (i, 0)),
                   pl.BlockSpec((tm * TILE_ROWS, 128), lambda i: (i, 0)),
                   pl.BlockSpec((ROUTER_ROWS, tm), lambda i: (0, i))],
        out_shape=[jax.ShapeDtypeStruct((n, d), _F32),
                   jax.ShapeDtypeStruct((n * TILE_ROWS, 128), jnp.uint32),
                   jax.ShapeDtypeStruct((ROUTER_ROWS, n), _F32)],
        compiler_params=_cparams(1),
        name="outproj_norm_router",
    )(merged, w_out, x, g_ffn, r_cat)


def _first_index_of_max(vals, n_rows):
    vmax = jnp.max(vals, axis=0, keepdims=True)
    rows = lax.broadcasted_iota(jnp.int32, vals.shape, 0).astype(_F32)
    idx = jnp.min(jnp.where(vals == vmax, rows, float(n_rows)), axis=0, keepdims=True)
    return vmax, idx


def _route_kernel(lt_ref, tri_ref, eid_ref, wts_ref, rank_ref, cnt_ref, carry):
    tl = lt_ref.shape[1]

    @pl.when(pl.program_id(0) == 0)
    def _():
        carry[...] = jnp.zeros(carry.shape, _F32)

    lt = lt_ref[...]
    g = lt[0:N_GROUPS]
    ge = jnp.exp(g - jnp.max(g, axis=0, keepdims=True))
    gp = ge / jnp.sum(ge, axis=0, keepdims=True)
    g_w, g_idx = _first_index_of_max(gp, N_GROUPS)
    el = jnp.zeros((EXPERTS_PER_GROUP, tl), _F32)
    for gg in range(N_GROUPS):
        el = jnp.where(g_idx == float(gg), lt[8 + gg * EXPERTS_PER_GROUP:8 + (gg + 1) * EXPERTS_PER_GROUP], el)
    ee = jnp.exp(el - jnp.max(el, axis=0, keepdims=True))
    ep = ee / jnp.sum(ee, axis=0, keepdims=True)
    v1, i1 = _first_index_of_max(ep, EXPERTS_PER_GROUP)
    rows8 = lax.broadcasted_iota(jnp.int32, ep.shape, 0).astype(_F32)
    v2, i2 = _first_index_of_max(jnp.where(rows8 == i1, -1.0, ep), EXPERTS_PER_GROUP)
    den = v1 + v2
    e1 = g_idx * float(EXPERTS_PER_GROUP) + i1
    e2 = g_idx * float(EXPERTS_PER_GROUP) + i2

    rows_e = lax.broadcasted_iota(jnp.int32, (N_EXPERTS, tl), 0).astype(_F32)
    ranks = []
    for e_sel in (e1, e2):
        onehot = rows_e == e_sel
        oh = jnp.where(onehot, 1.0, 0.0)
        before = jnp.dot(oh.astype(_BF16), tri_ref[...], preferred_element_type=_F32) + carry[...]
        ranks.append(jnp.sum(jnp.where(onehot, before, 0.0), axis=0, keepdims=True))
        carry[...] = carry[...] + jnp.sum(oh, axis=1, keepdims=True)

    zeros6 = jnp.zeros((6, tl), _F32)
    eid_ref[...] = jnp.concatenate([e1, e2, zeros6], axis=0).astype(jnp.int32)
    wts_ref[...] = jnp.concatenate([g_w * (v1 / den), g_w * (v2 / den), zeros6], axis=0)
    rank_ref[...] = jnp.concatenate(ranks + [zeros6], axis=0).astype(jnp.int32)
    cnt_ref[...] = jnp.broadcast_to(carry[...], cnt_ref.shape).astype(jnp.int32)


def _route(logits_t):
    n = logits_t.shape[1]
    tl = min(512, n)
    tri = jnp.asarray(np.triu(np.ones((tl, tl), np.float32), k=1), _BF16)
    row_block = pl.BlockSpec((8, tl), lambda i: (0, i))
    return pl.pallas_call(
        _route_kernel,
        grid=(n // tl,),
        in_specs=[pl.BlockSpec((ROUTER_ROWS, tl), lambda i: (0, i)),
                  pl.BlockSpec((tl, tl), lambda i: (0, 0))],
        out_specs=[row_block, row_block, row_block,
                   pl.BlockSpec((N_EXPERTS, 128), lambda i: (0, 0))],
        out_shape=[jax.ShapeDtypeStruct((8, n), jnp.int32),
                   jax.ShapeDtypeStruct((8, n), _F32),
                   jax.ShapeDtypeStruct((8, n), jnp.int32),
                   jax.ShapeDtypeStruct((N_EXPERTS, 128), jnp.int32)],
        scratch_shapes=[pltpu.VMEM((N_EXPERTS, 1), _F32)],
        compiler_params=_cparams(1),
        name="route_topk_rank",
    )(logits_t, tri)


def _row_copy(src_hbm, row, dst_vmem, r, sem):
    return pltpu.make_async_copy(src_hbm.at[_token_rows(row)], dst_vmem.at[_token_rows(r)], sem)


def _dispatch_kernel(dest_ref, cnt_ref, pstart_ref, pend_ref, h_ref, xs_hbm, rows, zero_rows, sem, zsem, *, n_steps):
    tm = h_ref.shape[0] // TILE_ROWS
    i = pl.program_id(0)
    n = n_steps * tm
    n_slots = xs_hbm.shape[0] // TILE_ROWS
    slot = i % 2
    n_jobs = 2 * N_EXPERTS
    jobs_per_step = -(-n_jobs // n_steps)

    def pad_job(j, wait):
        e = jnp.minimum(j, N_EXPERTS - 1)
        tail_lo = pend_ref[N_EXPERTS - 1] + (j - N_EXPERTS) * SLOT_BLOCK
        lo = jnp.where(j < N_EXPERTS, pstart_ref[e] + cnt_ref[e], tail_lo)
        hi = jnp.where(j < N_EXPERTS, pend_ref[e], jnp.minimum(tail_lo + SLOT_BLOCK, n_slots))

        def body(r, c):
            cp = pltpu.make_async_copy(zero_rows, xs_hbm.at[_token_rows(r)], zsem)
            cp.wait() if wait else cp.start()
            return c
        lax.fori_loop(lo, jnp.maximum(lo, hi), body, 0)

    def pad_jobs(wait):
        for jj in range(jobs_per_step):
            j = i * jobs_per_step + jj

            @pl.when(j < n_jobs)
            def _():
                pad_job(j, wait)

    @pl.when(i == 0)
    def _():
        zero_rows[...] = jnp.zeros(zero_rows.shape, zero_rows.dtype)

    pad_jobs(False)

    def row_copy(tile_slot, r, row):
        return pltpu.make_async_copy(rows.at[tile_slot, _token_rows(r)], xs_hbm.at[_token_rows(row)], sem.at[tile_slot])

    def drain(tile_slot):
        def body(r, c):
            for _ in range(TOP_K):
                row_copy(tile_slot, r, 0).wait()
            return c
        lax.fori_loop(0, tm, body, 0, unroll=8)

    @pl.when(i >= 2)
    def _():
        drain(slot)

    rows[slot] = h_ref[...]

    def issue(r, c):
        for k in range(TOP_K):
            row_copy(slot, r, dest_ref[k * n + i * tm + r]).start(priority=k)
        return c

    lax.fori_loop(0, tm, issue, 0, unroll=8)
    pad_jobs(True)

    @pl.when(i == n_steps - 1)
    def _():
        if n_steps >= 2:
            drain(1 - slot)
        drain(slot)


def _dispatch(dest, counts, pstarts, pends, h2, n_slots):
    n = h2.shape[0] // TILE_ROWS
    tm = min(256, n)
    return pl.pallas_call(
        functools.partial(_dispatch_kernel, n_steps=n // tm),
        grid_spec=pltpu.PrefetchScalarGridSpec(
            num_scalar_prefetch=4,
            grid=(n // tm,),
            in_specs=[pl.BlockSpec((tm * TILE_ROWS, 128), lambda i, *_: (i, 0))],
            out_specs=pl.BlockSpec(memory_space=pl.ANY),
            scratch_shapes=[pltpu.VMEM((2, tm * TILE_ROWS, 128), h2.dtype), pltpu.VMEM((TILE_ROWS, 128), h2.dtype),
                            pltpu.SemaphoreType.DMA((2,)), pltpu.SemaphoreType.DMA]),
        out_shape=jax.ShapeDtypeStruct((n_slots * TILE_ROWS, 128), h2.dtype),
        compiler_params=_cparams(1),
        name="slot_dispatch",
    )(dest, counts, pstarts, pends, h2)


def _expert_kernel(be_ref, first_ref, slot_ref, nxt_ref, nused_ref, x_ref, wg_hbm, wu_hbm, wd_hbm, o_ref,
                   wg_buf, wu_buf, wd_buf, sem):
    i = pl.program_id(0)

    def weight_copies(e, s):
        return (pltpu.make_async_copy(wg_hbm.at[e], wg_buf.at[s], sem.at[s, 0]),
                pltpu.make_async_copy(wu_hbm.at[e], wu_buf.at[s], sem.at[s, 1]),
                pltpu.make_async_copy(wd_hbm.at[e], wd_buf.at[s], sem.at[s, 2]))

    @pl.when(i == 0)
    def _():
        for cp in weight_copies(be_ref[0], 0):
            cp.start()

    s = slot_ref[i]

    @pl.when(first_ref[i] == 1)
    def _():
        for cp in weight_copies(be_ref[i], s):
            cp.wait()

        @pl.when(nxt_ref[i] >= 0)
        def _():
            for cp in weight_copies(nxt_ref[i], 1 - s):
                cp.start()

    used = i < nused_ref[0]

    @pl.when(used)
    def _():
        x_lo, x_hi = _unpack_halves(_load_token_tiles(x_ref))
        g = _dot(x_lo, wg_buf[s, :HALF]) + _dot(x_hi, wg_buf[s, HALF:])
        u = _dot(x_lo, wu_buf[s, :HALF]) + _dot(x_hi, wu_buf[s, HALF:])
        mid = (jax.nn.silu(g) * u).astype(_BF16)
        _store_token_tiles(o_ref, _pack_halves(_dot(mid, wd_buf[s])))

    @pl.when(jnp.logical_not(used))
    def _():
        o_ref[...] = jnp.zeros(o_ref.shape, o_ref.dtype)


def _experts(blk_e, n_used, xs, w_gate, w_up, w_down):
    p = xs.shape[0] // TILE_ROWS
    d = w_gate.shape[1]
    nb = p // SLOT_BLOCK
    first = jnp.concatenate([jnp.ones((1,), jnp.int32), (blk_e[1:] != blk_e[:-1]).astype(jnp.int32)])
    slot = (jnp.cumsum(first) - 1) % 2
    later = blk_e[None, :] > blk_e[:, None]
    nxt = jnp.min(jnp.where(later, blk_e[None, :], N_EXPERTS), axis=1)
    nxt = jnp.where(nxt < N_EXPERTS, nxt, -1).astype(jnp.int32)
    any_spec = pl.BlockSpec(memory_space=pl.ANY)
    return pl.pallas_call(
        _expert_kernel,
        grid_spec=pltpu.PrefetchScalarGridSpec(
            num_scalar_prefetch=5,
            grid=(nb,),
            in_specs=[pl.BlockSpec((SLOT_BLOCK * TILE_ROWS, 128), lambda i, *_: (i, 0)), any_spec, any_spec, any_spec],
            out_specs=pl.BlockSpec((SLOT_BLOCK * TILE_ROWS, 128), lambda i, *_: (i, 0)),
            scratch_shapes=[pltpu.VMEM((2, d, D_EXPERT), _F32),
                            pltpu.VMEM((2, d, D_EXPERT), _F32),
                            pltpu.VMEM((2, D_EXPERT, d), _F32),
                            pltpu.SemaphoreType.DMA((2, 3))]),
        out_shape=jax.ShapeDtypeStruct((p * TILE_ROWS, 128), jnp.uint32),
        compiler_params=pltpu.CompilerParams(dimension_semantics=("arbitrary",),
                                             vmem_limit_bytes=EXPERT_VMEM_LIMIT),
        name="expert_mlp",
    )(blk_e, first, slot.astype(jnp.int32), nxt, n_used, xs, w_gate, w_up, w_down)


def _combine_kernel(dest_ref, x2_ref, w0_ref, w1_ref, g_ref, ys_hbm, o_ref, rows, sem):
    tm = x2_ref.shape[0]
    n = pl.num_programs(0) * tm
    i = pl.program_id(0)

    def issue(tile, slot):
        def body(r, c):
            for k in range(TOP_K):
                _row_copy(ys_hbm, dest_ref[k * n + tile * tm + r], rows.at[slot, k], r, sem.at[slot]).start(priority=k)
            return c
        lax.fori_loop(0, tm, body, 0, unroll=8)

    @pl.when(i == 0)
    def _():
        issue(0, 0)

    @pl.when(i + 1 < pl.num_programs(0))
    def _():
        issue(i + 1, (i + 1) % 2)

    slot = i % 2

    def drain(r, c):
        for k in range(TOP_K):
            _row_copy(ys_hbm, 0, rows.at[slot, k], r, sem.at[slot]).wait()
        return c

    lax.fori_loop(0, tm, drain, 0, unroll=8)
    y0 = _unpack_halves(_load_token_tiles(rows.at[slot, 0]))
    y1 = _unpack_halves(_load_token_tiles(rows.at[slot, 1]))
    w0 = w0_ref[...]
    w1 = w1_ref[...]
    halves = (slice(0, HALF), slice(HALF, 2 * HALF))
    x3 = [x2_ref[:, cols] + (w0 * a + w1 * b) for cols, a, b in zip(halves, y0, y1)]
    ms = (jnp.sum(x3[0] * x3[0], axis=-1, keepdims=True) + jnp.sum(x3[1] * x3[1], axis=-1, keepdims=True)) / (2 * HALF)
    inv = lax.rsqrt(ms + NORM_EPS)
    for cols, v in zip(halves, x3):
        o_ref[:, cols] = (v * inv) * g_ref[:, cols]


def _combine(dest, x2, w0, w1, g_final, ys):
    n, d = x2.shape
    tm = min(256, n)
    return pl.pallas_call(
        _combine_kernel,
        grid_spec=pltpu.PrefetchScalarGridSpec(
            num_scalar_prefetch=1,
            grid=(n // tm,),
            in_specs=[pl.BlockSpec((tm, d), lambda i, dst: (i, 0)),
                      pl.BlockSpec((tm, 1), lambda i, dst: (i, 0)),
                      pl.BlockSpec((tm, 1), lambda i, dst: (i, 0)),
                      pl.BlockSpec((1, d), lambda i, dst: (0, 0)),
                      pl.BlockSpec(memory_space=pl.ANY)],
            out_specs=pl.BlockSpec((tm, d), lambda i, dst: (i, 0)),
            scratch_shapes=[pltpu.VMEM((2, TOP_K, tm * TILE_ROWS, 128), jnp.uint32), pltpu.SemaphoreType.DMA((2,))]),
        out_shape=jax.ShapeDtypeStruct((n, d), _F32),
        compiler_params=_cparams(1),
        name="combine_final_norm",
    )(dest, x2, w0, w1, g_final, ys)


def kernel(x, positions, g_mix, w_in, conv_w, conv_b, w_rg_a, b_rg_a, w_rg_x, b_rg_x, lru_param, lambda_q1, lambda_k1, lambda_q2, lambda_k2, subln_g, w_br_rnn, w_br_attn, w_out, g_ffn, w_grp_router, w_exp_router, w_gate, w_up, w_down, g_final):
    batch, seq, d = x.shape
    n = batch * seq
    xf = x.reshape(n, d)
    pos_col = positions.reshape(n, 1)
    row = lambda v: v.reshape(1, -1)

    w_rg = (0.5 * jnp.concatenate([w_rg_a[0], w_rg_x[0]], axis=-1)).astype(_BF16)
    w_router = jnp.concatenate(
        [w_grp_router[0], jnp.zeros((d, 8 - N_GROUPS), _F32), w_exp_router[0],
         jnp.zeros((d, 128 - ROUTER_ROWS), _F32)], axis=1)

    h = _rmsnorm(xf, row(g_mix[0]), _BF16)
    cos_t, sin_t = _rope_tables(pos_col)

    c_qk = 2 * D_RNN
    c_v = c_qk + 2 * QK_W
    c_g = c_v + ATTN_W
    u = _projection(_proj_plain_kernel, h, w_in[0], 0, D_RNN, _F32, name="proj_rnn")
    gg = _projection(_proj_gelu_kernel, h, w_in[0], D_RNN, D_RNN, _F32, name="proj_rnn_gate")
    q_scale = (HEAD_DIM ** -0.5) * LOG2E
    qt = _projection(functools.partial(_proj_rope_kernel, scale=q_scale, transpose_out=True),
                     h, w_in[0], c_qk, QK_W, _BF16, extra=(cos_t, sin_t), transpose_out=True, name="proj_q_rope")
    k = _projection(functools.partial(_proj_rope_kernel, scale=1.0, transpose_out=False),
                    h, w_in[0], c_qk + QK_W, QK_W, _BF16, extra=(cos_t, sin_t), name="proj_k_rope")
    vt = _projection(_proj_t_kernel, h, w_in[0], c_v, ATTN_W, _BF16, transpose_out=True, name="proj_v")
    gates = _projection(_proj_sigmoid_kernel, h, w_in[0], c_g, 2 * D_MODEL, _BF16, name="proj_gates")

    y_rnn = _lru(pos_col, u, gg, conv_w[0], row(conv_b[0]), w_rg, row(0.5 * b_rg_a[0]), row(0.5 * b_rg_x[0]),
                 row(lru_param[0]), batch, seq)
    y_attn = _attention(qt, k, vt, row(lambda_q1[0]), row(lambda_k1[0]), row(lambda_q2[0]), row(lambda_k2[0]),
                        row(subln_g[0]), batch, seq)
    merged = _merge(y_rnn, y_attn, w_br_rnn[0], w_br_attn[0], gates)
    x2, h2, logits_t = _outproj(merged, w_out[0], xf, row(g_ffn[0]), w_router)

    eid8, wts8, rank8, cnt = _route(logits_t)
    eid, rank, counts = eid8[:TOP_K], rank8[:TOP_K], cnt[:, 0]

    pcounts = ((counts + SLOT_BLOCK - 1) // SLOT_BLOCK) * SLOT_BLOCK
    pends = jnp.cumsum(pcounts)
    pstarts = pends - pcounts
    experts = jnp.arange(N_EXPERTS, dtype=jnp.int32)
    dest = jnp.sum(jnp.where(eid[:, :, None] == experts, pstarts, 0), axis=-1) + rank
    n_slots = n * TOP_K + N_EXPERTS * SLOT_BLOCK
    blk_start = jnp.arange(n_slots // SLOT_BLOCK, dtype=jnp.int32) * SLOT_BLOCK
    blk_e = jnp.minimum(jnp.sum((blk_start[:, None] >= pends[None, :]).astype(jnp.int32), axis=1), N_EXPERTS - 1)

    dest_flat = dest.reshape(-1)
    xs = _dispatch(dest_flat, counts, pstarts, pends, h2, n_slots)
    n_used = (pends[N_EXPERTS - 1:] // SLOT_BLOCK).astype(jnp.int32)
    ys = _experts(blk_e, n_used, xs, w_gate[0], w_up[0], w_down[0])
    out = _combine(dest_flat, x2, wts8[0].reshape(n, 1), wts8[1].reshape(n, 1), row(g_final), ys)
    return out.reshape(batch, seq, d)
```

```python
import functools
import math

import numpy as np
import jax
import jax.numpy as jnp
from jax import lax
from jax.experimental import pallas as pl
from jax.experimental.pallas import tpu as pltpu

D_MODEL = 2048
D_RNN = D_MODEL
LRU_BLOCK = 128
N_LRU_BLOCKS = D_RNN // LRU_BLOCK
CONV_W = 4
C_LRU = 8.0
HEAD_DIM = 128
N_HEADS = D_MODEL // (2 * HEAD_DIM)
V_DIM = 2 * HEAD_DIM
QK_W = N_HEADS * 2 * HEAD_DIM
ATTN_W = N_HEADS * V_DIM
ROPE_THETA = 10000.0
SUBLN_EPS = 1e-5
N_GROUPS = 4
EXPERTS_PER_GROUP = 8
N_EXPERTS = N_GROUPS * EXPERTS_PER_GROUP
TOP_K = 2
D_EXPERT = D_MODEL // 2
NORM_EPS = 1e-6
LAM_INIT = 0.8 - 0.6 * math.exp(-0.3 * 0)

SLOT_BLOCK = 256
HALF = D_MODEL // 2
TILE_ROWS = HALF // 128
ROUTER_ROWS = 8 + N_EXPERTS
MASK_VALUE = -1e30
LOG2E = 1.4426950408889634
VMEM_LIMIT = 56 * 1024 * 1024
EXPERT_VMEM_LIMIT = 60 * 1024 * 1024

_F32 = jnp.float32
_BF16 = jnp.bfloat16


def _dot(a, b):
    return lax.dot_general(a, b, (((1,), (0,)), ((), ())), preferred_element_type=_F32)


def _pack_halves(y):
    bits = lax.bitcast_convert_type(y.astype(_BF16).astype(_F32), jnp.uint32)
    return (bits[:, :HALF] >> 16) | bits[:, HALF:]


def _unpack_halves(p):
    lo = lax.bitcast_convert_type(p << 16, _F32)
    hi = lax.bitcast_convert_type(p & jnp.uint32(0xFFFF0000), _F32)
    return lo, hi


def _token_rows(t):
    return pl.ds(pl.multiple_of(t * TILE_ROWS, TILE_ROWS), TILE_ROWS)


def _store_token_tiles(ref, packed):
    tokens = packed.shape[0]
    for j in range(TILE_ROWS):
        ref[pl.ds(j, tokens, stride=TILE_ROWS), :] = packed[:, j * 128:(j + 1) * 128]


def _load_token_tiles(ref):
    tokens = ref.shape[0] // TILE_ROWS
    return jnp.concatenate([ref[pl.ds(j, tokens, stride=TILE_ROWS), :] for j in range(TILE_ROWS)], axis=1)


def _cparams(n_axes):
    return pltpu.CompilerParams(dimension_semantics=("arbitrary",) * n_axes,
                                vmem_limit_bytes=VMEM_LIMIT)


def _rmsnorm_kernel(x_ref, g_ref, o_ref):
    x = x_ref[...]
    y = x * lax.rsqrt(jnp.mean(x * x, axis=-1, keepdims=True) + NORM_EPS)
    o_ref[...] = (y * g_ref[...]).astype(o_ref.dtype)


def _rmsnorm(x, g, out_dtype):
    n, d = x.shape
    tm = min(512, n)
    return pl.pallas_call(
        _rmsnorm_kernel,
        grid=(n // tm,),
        in_specs=[pl.BlockSpec((tm, d), lambda i: (i, 0)),
                  pl.BlockSpec((1, d), lambda i: (0, 0))],
        out_specs=pl.BlockSpec((tm, d), lambda i: (i, 0)),
        out_shape=jax.ShapeDtypeStruct((n, d), out_dtype),
        compiler_params=_cparams(1),
        name="rmsnorm",
    )(x, g)


def _rope_table_kernel(pos_ref, inv_ref, sign_ref, cos_ref, sin_ref):
    ang = pos_ref[...].astype(_F32) * inv_ref[...]
    cos_ref[...] = jnp.cos(ang)
    sin_ref[...] = jnp.sin(ang) * sign_ref[...]


def _rope_tables(pos_col):
    n = pos_col.shape[0]
    tm = min(1024, n)
    inv = 1.0 / (ROPE_THETA ** (np.arange(0, HEAD_DIM, 2, dtype=np.float32) / np.float32(HEAD_DIM)))
    inv = np.concatenate([inv, inv]).astype(np.float32)[None, :]
    sign = np.concatenate([-np.ones(HEAD_DIM // 2), np.ones(HEAD_DIM // 2)]).astype(np.float32)[None, :]
    return pl.pallas_call(
        _rope_table_kernel,
        grid=(n // tm,),
        in_specs=[pl.BlockSpec((tm, 1), lambda i: (i, 0)),
                  pl.BlockSpec((1, HEAD_DIM), lambda i: (0, 0)),
                  pl.BlockSpec((1, HEAD_DIM), lambda i: (0, 0))],
        out_specs=[pl.BlockSpec((tm, HEAD_DIM), lambda i: (i, 0))] * 2,
        out_shape=[jax.ShapeDtypeStruct((n, HEAD_DIM), _F32)] * 2,
        compiler_params=_cparams(1),
        name="rope_tables",
    )(pos_col, jnp.asarray(inv), jnp.asarray(sign))


def _proj_plain_kernel(h_ref, w_ref, o_ref):
    o_ref[...] = _dot(h_ref[...], w_ref[...]).astype(o_ref.dtype)


def _proj_gelu_kernel(h_ref, w_ref, o_ref):
    o_ref[...] = jax.nn.gelu(_dot(h_ref[...], w_ref[...]), approximate=True).astype(o_ref.dtype)


def _proj_sigmoid_kernel(h_ref, w_ref, o_ref):
    acc = _dot(h_ref[...], w_ref[...])
    o_ref[...] = (0.5 * jnp.tanh(0.5 * acc) + 0.5).astype(o_ref.dtype)


def _proj_t_kernel(h_ref, w_ref, o_ref):
    acc = _dot(h_ref[...], w_ref[...])
    for g in range(acc.shape[1] // HEAD_DIM):
        o_ref[g * HEAD_DIM:(g + 1) * HEAD_DIM, :] = acc[:, g * HEAD_DIM:(g + 1) * HEAD_DIM].T.astype(o_ref.dtype)


def _proj_rope_kernel(h_ref, w_ref, cos_ref, sin_ref, o_ref, *, scale, transpose_out):
    acc = _dot(h_ref[...], w_ref[...])
    cos = cos_ref[...]
    sin = sin_ref[...]
    for g in range(acc.shape[1] // HEAD_DIM):
        t = acc[:, g * HEAD_DIM:(g + 1) * HEAD_DIM]
        r = (t * cos + pltpu.roll(t, HEAD_DIM // 2, axis=1) * sin) * scale
        if transpose_out:
            o_ref[g * HEAD_DIM:(g + 1) * HEAD_DIM, :] = r.T.astype(o_ref.dtype)
        else:
            o_ref[:, g * HEAD_DIM:(g + 1) * HEAD_DIM] = r.astype(o_ref.dtype)


def _projection(kernel_fn, h, w, col_off, n_cols, out_dtype, extra=(), transpose_out=False, name="proj"):
    n, k = h.shape
    tm = min(1024, n)
    tn = 1024
    off = col_off // tn
    extra_specs = [pl.BlockSpec((tm, e.shape[1]), lambda j, i: (i, 0)) for e in extra]
    if transpose_out:
        out_spec = pl.BlockSpec((tn, tm), lambda j, i: (j, i))
        out_shape = jax.ShapeDtypeStruct((n_cols, n), out_dtype)
    else:
        out_spec = pl.BlockSpec((tm, tn), lambda j, i: (i, j))
        out_shape = jax.ShapeDtypeStruct((n, n_cols), out_dtype)
    return pl.pallas_call(
        kernel_fn,
        grid=(n_cols // tn, n // tm),
        in_specs=[pl.BlockSpec((tm, k), lambda j, i: (i, 0)),
                  pl.BlockSpec((k, tn), lambda j, i: (0, j + off))] + extra_specs,
        out_specs=out_spec,
        out_shape=out_shape,
        compiler_params=_cparams(2),
        name=name,
    )(h, w, *extra)


def _lru_kernel(pos_ref, u_ref, gg_ref, cw_ref, cb_ref, wrg_ref, ba_ref, bx_ref, lp_ref, o_ref,
                ubuf, hcar, a_s, b_s, h_s):
    t_rows, cb = u_ref.shape
    n_grp = t_rows // 8

    @pl.when(pl.program_id(2) == 0)
    def _():
        ubuf[0:8, :] = jnp.zeros((8, cb), _F32)
        hcar[...] = jnp.zeros((8, cb), _F32)

    ubuf[8:8 + t_rows, :] = u_ref[...]
    cw = cw_ref[...]
    uc = cb_ref[...] + cw[0:1] * ubuf[5:5 + t_rows, :]
    uc = uc + cw[1:2] * ubuf[6:6 + t_rows, :]
    uc = uc + cw[2:3] * ubuf[7:7 + t_rows, :]
    uc = uc + cw[3:4] * ubuf[8:8 + t_rows, :]
    ubuf[0:8, :] = ubuf[t_rows:t_rows + 8, :]

    ucb = uc.astype(_BF16)
    r_parts, i_parts = [], []
    for j in range(cb // LRU_BLOCK):
        g = jnp.dot(ucb[:, j * LRU_BLOCK:(j + 1) * LRU_BLOCK], wrg_ref[j], preferred_element_type=_F32)
        r_parts.append(g[:, :LRU_BLOCK])
        i_parts.append(g[:, LRU_BLOCK:])
    t_r = jnp.tanh(jnp.concatenate(r_parts, axis=1) + ba_ref[...])
    gi = 0.5 * jnp.tanh(jnp.concatenate(i_parts, axis=1) + bx_ref[...]) + 0.5

    half_c_sp = (-0.5 * C_LRU) * jax.nn.softplus(-lp_ref[...])
    log_a = half_c_sp * t_r + half_c_sp
    a_raw = jnp.exp(log_a)
    m2 = 1.0 - a_raw * a_raw
    mult = jnp.where(m2 > 0.0, m2 * lax.rsqrt(m2), 0.0)
    reset = pos_ref[...] == 0
    a = jnp.where(reset, 0.0, a_raw)
    mult = jnp.where(reset, 1.0, mult)
    bv = uc * gi * mult

    a3 = a.reshape(n_grp, 8, cb)
    b3 = bv.reshape(n_grp, 8, cb)
    row = lax.broadcasted_iota(jnp.int32, (n_grp, 8, cb), 1)
    for d in (1, 2, 4):
        a_sh = pltpu.roll(a3, d, axis=1)
        b_sh = pltpu.roll(b3, d, axis=1)
        take = row >= d
        b3 = jnp.where(take, a3 * b_sh + b3, b3)
        a3 = jnp.where(take, a3 * a_sh, a3)
    a_s[...] = a3
    b_s[...] = b3

    def carry_step(g, carry):
        h = b_s[g] + a_s[g] * carry
        h_s[g] = h
        return jnp.broadcast_to(h[7:8, :], (8, cb))

    hcar[...] = lax.fori_loop(0, n_grp, carry_step, hcar[...], unroll=8)

    hr = h_s[...].reshape(t_rows, cb)
    o_ref[...] = (hr * gg_ref[...]).astype(o_ref.dtype)


def _lru(pos_col, u, gg, conv_w, conv_b, w_rg, b_a, b_x, lru_param, batch, seq):
    cb = 512
    t_rows = min(1024, seq)
    n_cb = D_RNN // cb
    n_t = seq // t_rows
    row_map = lambda b, c, s: (b * n_t + s, 0)
    return pl.pallas_call(
        _lru_kernel,
        grid=(batch, n_cb, n_t),
        in_specs=[pl.BlockSpec((t_rows, 1), row_map),
                  pl.BlockSpec((t_rows, cb), lambda b, c, s: (b * n_t + s, c)),
                  pl.BlockSpec((t_rows, cb), lambda b, c, s: (b * n_t + s, c)),
                  pl.BlockSpec((CONV_W, cb), lambda b, c, s: (0, c)),
                  pl.BlockSpec((1, cb), lambda b, c, s: (0, c)),
                  pl.BlockSpec((cb // LRU_BLOCK, LRU_BLOCK, 2 * LRU_BLOCK), lambda b, c, s: (c, 0, 0)),
                  pl.BlockSpec((1, cb), lambda b, c, s: (0, c)),
                  pl.BlockSpec((1, cb), lambda b, c, s: (0, c)),
                  pl.BlockSpec((1, cb), lambda b, c, s: (0, c))],
        out_specs=pl.BlockSpec((t_rows, cb), lambda b, c, s: (b * n_t + s, c)),
        out_shape=jax.ShapeDtypeStruct((batch * seq, D_RNN), _BF16),
        scratch_shapes=[pltpu.VMEM((t_rows + 8, cb), _F32),
                        pltpu.VMEM((8, cb), _F32),
                        pltpu.VMEM((t_rows // 8, 8, cb), _F32),
                        pltpu.VMEM((t_rows // 8, 8, cb), _F32),
                        pltpu.VMEM((t_rows // 8, 8, cb), _F32)],
        compiler_params=_cparams(3),
        name="conv_rglru",
    )(pos_col, u, gg, conv_w, conv_b, w_rg, b_a, b_x, lru_param)


def _attn_kernel(qt_ref, k_ref, vt_ref, qtn_ref, kn_ref, lq1_ref, lk1_ref, lq2_ref, lk2_ref, g_ref, o_ref,
                 m_s, l_s, acc_s, st_s, mx_s, *, tk, cq):
    tq = qt_ref.shape[1]
    q0 = pl.program_id(2) * tq
    n_chunks = tq // cq
    n_diag = tq // tk

    m_s[...] = jnp.full(m_s.shape, MASK_VALUE, _F32)
    l_s[...] = jnp.zeros(l_s.shape, _F32)
    acc_s[...] = jnp.zeros(acc_s.shape, _F32)

    def chunks(diag):
        return [c for c in range(n_chunks) if diag is None or (c + 1) * cq > diag]

    def scores_of(k_blk, q_ref, slot, diag):
        for c in chunks(diag):
            cols = slice(c * cq, (c + 1) * cq)
            for mp in range(2):
                dims = slice(mp * HEAD_DIM, (mp + 1) * HEAD_DIM)
                st = jnp.dot(k_blk[:, dims], q_ref[dims, cols], preferred_element_type=_F32)
                st_s[slot, mp, :, cols] = st
                mx_s[slot, mp, :, cols] = jnp.max(st, axis=0, keepdims=True)

    def scores(col0, slot, diag):
        scores_of(k_ref[pl.ds(col0, tk), :], qt_ref, slot, diag)

    def update(col0, slot, diag):
        vt_blk = jnp.concatenate([vt_ref[:, pl.ds(col0, tk)], jnp.ones((16, tk), _BF16)], axis=0)
        for c in chunks(diag):
            masked = diag is not None and c * cq < diag + tk - 1
            cols = slice(c * cq, (c + 1) * cq)
            for mp in range(2):
                st = st_s[slot, mp, :, cols]
                if masked:
                    kv_pos = diag + lax.broadcasted_iota(jnp.int32, (tk, cq), 0)
                    q_pos = c * cq + lax.broadcasted_iota(jnp.int32, (tk, cq), 1)
                    st = jnp.where(kv_pos <= q_pos, st, MASK_VALUE)
                    m_cur = jnp.max(st, axis=0, keepdims=True)
                else:
                    m_cur = mx_s[slot, mp, :, cols]
                m_prev = m_s[mp, :, cols]
                m_new = jnp.maximum(m_prev, m_cur)
                alpha = jnp.exp2(m_prev - m_new)
                pt = jnp.exp2(st - m_new).astype(_BF16)
                pv = jnp.dot(vt_blk, pt, preferred_element_type=_F32)
                l_s[mp, :, cols] = alpha * l_s[mp, :, cols] + pv[V_DIM:V_DIM + 1]
                acc_s[mp, :, cols] = alpha * acc_s[mp, :, cols] + pv[:V_DIM]
                m_s[mp, :, cols] = m_new

    n_pairs = q0 // (2 * tk)

    @pl.when((pl.program_id(0) == 0) & (pl.program_id(1) == 0) & (pl.program_id(2) == 0))
    def _():
        scores(0, 0, None)

    def pair(t, carry):
        c0 = pl.multiple_of(2 * t * tk, tk)
        scores(c0 + tk, 1, None)
        update(c0, 0, None)
        scores(c0 + 2 * tk, 0, None)
        update(c0 + tk, 1, None)
        return carry

    lax.fori_loop(0, n_pairs, pair, 0)
    for d in range(n_diag):
        col0 = pl.multiple_of(q0 + d * tk, tk)
        if d + 1 < n_diag:
            scores(col0 + tk, (d + 1) % 2, (d + 1) * tk)
        else:
            scores_of(kn_ref[...], qtn_ref, 0, None)
        update(col0, d % 2, d * tk)

    lam = (jnp.exp(jnp.sum(lq1_ref[...] * lk1_ref[...], axis=1, keepdims=True))
           - jnp.exp(jnp.sum(lq2_ref[...] * lk2_ref[...], axis=1, keepdims=True)) + LAM_INIT)
    ot = acc_s[0] / l_s[0] - lam * (acc_s[1] / l_s[1])
    yt = ot * lax.rsqrt(jnp.mean(ot * ot, axis=0, keepdims=True) + SUBLN_EPS)
    o_ref[...] = ((yt.T * g_ref[...]) * (1.0 - LAM_INIT)).astype(o_ref.dtype)


def _attention(qt, k, vt, lq1, lk1, lq2, lk2, subln_g, batch, seq):
    tq = min(1024, seq)
    tk = min(512, seq)
    cq = min(512, tq)
    nq = seq // tq
    assert tq % (2 * tk) == 0 and seq % tq == 0
    n_tiles = batch * N_HEADS * nq

    def next_tile(b, h, i):
        flat = jnp.minimum((b * N_HEADS + h) * nq + i + 1, n_tiles - 1)
        return flat // (N_HEADS * nq), (flat // nq) % N_HEADS, flat % nq

    def next_q(b, h, i):
        nb, nh, ni = next_tile(b, h, i)
        return nh, nb * nq + ni

    def next_k(b, h, i):
        nb, nh, _ = next_tile(b, h, i)
        return nb * (seq // tk), nh

    vec = pl.BlockSpec((1, HEAD_DIM), lambda b, h, i: (0, 0))
    return pl.pallas_call(
        functools.partial(_attn_kernel, tk=tk, cq=cq),
        grid=(batch, N_HEADS, nq),
        in_specs=[pl.BlockSpec((V_DIM, tq), lambda b, h, i: (h, b * nq + i)),
                  pl.BlockSpec((seq, V_DIM), lambda b, h, i: (b, h)),
                  pl.BlockSpec((V_DIM, seq), lambda b, h, i: (h, b)),
                  pl.BlockSpec((V_DIM, tq), next_q),
                  pl.BlockSpec((tk, V_DIM), next_k),
                  vec, vec, vec, vec,
                  pl.BlockSpec((1, V_DIM), lambda b, h, i: (0, 0))],
        out_specs=pl.BlockSpec((tq, V_DIM), lambda b, h, i: (b * nq + i, h)),
        out_shape=jax.ShapeDtypeStruct((batch * seq, ATTN_W), _BF16),
        scratch_shapes=[pltpu.VMEM((2, 1, tq), _F32),
                        pltpu.VMEM((2, 1, tq), _F32),
                        pltpu.VMEM((2, V_DIM, tq), _F32),
                        pltpu.VMEM((2, 2, tk, tq), _F32),
                        pltpu.VMEM((2, 2, 1, tq), _F32)],
        compiler_params=_cparams(3),
        name="diff_attention",
    )(qt, k, vt, qt, k, lq1, lk1, lq2, lk2, subln_g)


def _merge_kernel(yr_ref, ya_ref, wr_ref, wa_ref, gr_ref, ga_ref, o_ref):
    pr = _dot(yr_ref[...], wr_ref[...])
    pa = _dot(ya_ref[...], wa_ref[...])
    o_ref[...] = (gr_ref[...].astype(_F32) * pr + ga_ref[...].astype(_F32) * pa).astype(o_ref.dtype)


def _merge(y_rnn, y_attn, w_r, w_a, gates):
    n, k = y_rnn.shape
    tm = min(512, n)
    tn = 1024
    n_tn = D_MODEL // tn
    return pl.pallas_call(
        _merge_kernel,
        grid=(n_tn, n // tm),
        in_specs=[pl.BlockSpec((tm, k), lambda j, i: (i, 0)),
                  pl.BlockSpec((tm, k), lambda j, i: (i, 0)),
                  pl.BlockSpec((k, tn), lambda j, i: (0, j)),
                  pl.BlockSpec((k, tn), lambda j, i: (0, j)),
                  pl.BlockSpec((tm, tn), lambda j, i: (i, j)),
                  pl.BlockSpec((tm, tn), lambda j, i: (i, n_tn + j))],
        out_specs=pl.BlockSpec((tm, tn), lambda j, i: (i, j)),
        out_shape=jax.ShapeDtypeStruct((n, D_MODEL), _BF16),
        compiler_params=_cparams(2),
        name="branch_merge",
    )(y_rnn, y_attn, w_r, w_a, gates, gates)


def _outproj_kernel(m_ref, w_ref, x_ref, g_ref, r_ref, x2_ref, h2_ref, lt_ref):
    x2 = x_ref[...] + _dot(m_ref[...], w_ref[...])
    x2_ref[...] = x2
    h2 = (x2 * lax.rsqrt(jnp.mean(x2 * x2, axis=-1, keepdims=True) + NORM_EPS)) * g_ref[...]
    _store_token_tiles(h2_ref, _pack_halves(h2))
    h_hi = h2.astype(_BF16)
    h_lo = (h2 - h_hi.astype(_F32)).astype(_BF16)
    hi = _dot(h_hi, r_ref[...])
    logits = hi[:, :128] + (_dot(h_lo, r_ref[:, :128]) + hi[:, 128:])
    lt_ref[...] = logits.T[:ROUTER_ROWS]


def _outproj(merged, w_out, x, g_ffn, w_router):
    n, d = x.shape
    tm = min(256, n)
    r_hi = w_router.astype(_BF16)
    r_cat = jnp.concatenate([r_hi, (w_router - r_hi.astype(_F32)).astype(_BF16)], axis=1)
    return pl.pallas_call(
        _outproj_kernel,
        grid=(n // tm,),
        in_specs=[pl.BlockSpec((tm, d), lambda i: (i, 0)),
                  pl.BlockSpec((d, d), lambda i: (0, 0)),
                  pl.BlockSpec((tm, d), lambda i: (i, 0)),
                  pl.BlockSpec((1, d), lambda i: (0, 0)),
                  pl.BlockSpec((d, 256), lambda i: (0, 0))],
        out_specs=[pl.BlockSpec((tm, d), lambda i: (i, 0)),
                   pl.BlockSpec((tm * TILE_ROWS, 128), lambda i: (i, 0)),
                   pl.BlockSpec((ROUTER_ROWS, tm), lambda i: (0, i))],
        out_shape=[jax.ShapeDtypeStruct((n, d), _F32),
                   jax.ShapeDtypeStruct((n * TILE_ROWS, 128), jnp.uint32),
                   jax.ShapeDtypeStruct((ROUTER_ROWS, n), _F32)],
        compiler_params=_cparams(1),
        name="outproj_norm_router",
    )(merged, w_out, x, g_ffn, r_cat)


def _first_index_of_max(vals, n_rows):
    vmax = jnp.max(vals, axis=0, keepdims=True)
    rows = lax.broadcasted_iota(jnp.int32, vals.shape, 0).astype(_F32)
    idx = jnp.min(jnp.where(vals == vmax, rows, float(n_rows)), axis=0, keepdims=True)
    return vmax, idx


def _route_kernel(lt_ref, tri_ref, eid_ref, wts_ref, rank_ref, cnt_ref, carry):
    tl = lt_ref.shape[1]

    @pl.when(pl.program_id(0) == 0)
    def _():
        carry[...] = jnp.zeros(carry.shape, _F32)

    lt = lt_ref[...]
    g = lt[0:N_GROUPS]
    ge = jnp.exp(g - jnp.max(g, axis=0, keepdims=True))
    gp = ge / jnp.sum(ge, axis=0, keepdims=True)
    g_w, g_idx = _first_index_of_max(gp, N_GROUPS)
    el = jnp.zeros((EXPERTS_PER_GROUP, tl), _F32)
    for gg in range(N_GROUPS):
        el = jnp.where(g_idx == float(gg), lt[8 + gg * EXPERTS_PER_GROUP:8 + (gg + 1) * EXPERTS_PER_GROUP], el)
    ee = jnp.exp(el - jnp.max(el, axis=0, keepdims=True))
    ep = ee / jnp.sum(ee, axis=0, keepdims=True)
    v1, i1 = _first_index_of_max(ep, EXPERTS_PER_GROUP)
    rows8 = lax.broadcasted_iota(jnp.int32, ep.shape, 0).astype(_F32)
    v2, i2 = _first_index_of_max(jnp.where(rows8 == i1, -1.0, ep), EXPERTS_PER_GROUP)
    den = v1 + v2
    e1 = g_idx * float(EXPERTS_PER_GROUP) + i1
    e2 = g_idx * float(EXPERTS_PER_GROUP) + i2

    rows_e = lax.broadcasted_iota(jnp.int32, (N_EXPERTS, tl), 0).astype(_F32)
    ranks = []
    for e_sel in (e1, e2):
        onehot = rows_e == e_sel
        oh = jnp.where(onehot, 1.0, 0.0)
        before = jnp.dot(oh.astype(_BF16), tri_ref[...], preferred_element_type=_F32) + carry[...]
        ranks.append(jnp.sum(jnp.where(onehot, before, 0.0), axis=0, keepdims=True))
        carry[...] = carry[...] + jnp.sum(oh, axis=1, keepdims=True)

    zeros6 = jnp.zeros((6, tl), _F32)
    eid_ref[...] = jnp.concatenate([e1, e2, zeros6], axis=0).astype(jnp.int32)
    wts_ref[...] = jnp.concatenate([g_w * (v1 / den), g_w * (v2 / den), zeros6], axis=0)
    rank_ref[...] = jnp.concatenate(ranks + [zeros6], axis=0).astype(jnp.int32)
    cnt_ref[...] = jnp.broadcast_to(carry[...], cnt_ref.shape).astype(jnp.int32)


def _route(logits_t):
    n = logits_t.shape[1]
    tl = min(512, n)
    tri = jnp.asarray(np.triu(np.ones((tl, tl), np.float32), k=1), _BF16)
    row_block = pl.BlockSpec((8, tl), lambda i: (0, i))
    return pl.pallas_call(
        _route_kernel,
        grid=(n // tl,),
        in_specs=[pl.BlockSpec((ROUTER_ROWS, tl), lambda i: (0, i)),
                  pl.BlockSpec((tl, tl), lambda i: (0, 0))],
        out_specs=[row_block, row_block, row_block,
                   pl.BlockSpec((N_EXPERTS, 128), lambda i: (0, 0))],
        out_shape=[jax.ShapeDtypeStruct((8, n), jnp.int32),
                   jax.ShapeDtypeStruct((8, n), _F32),
                   jax.ShapeDtypeStruct((8, n), jnp.int32),
                   jax.ShapeDtypeStruct((N_EXPERTS, 128), jnp.int32)],
        scratch_shapes=[pltpu.VMEM((N_EXPERTS, 1), _F32)],
        compiler_params=_cparams(1),
        name="route_topk_rank",
    )(logits_t, tri)


def _row_copy(src_hbm, row, dst_vmem, r, sem):
    return pltpu.make_async_copy(src_hbm.at[_token_rows(row)], dst_vmem.at[_token_rows(r)], sem)


def _dispatch_kernel(dest_ref, cnt_ref, pstart_ref, pend_ref, h_ref, xs_hbm, rows, zero_rows, sem, zsem, *, n_steps):
    tm = h_ref.shape[0] // TILE_ROWS
    i = pl.program_id(0)
    n = n_steps * tm
    n_slots = xs_hbm.shape[0] // TILE_ROWS
    slot = i % 2
    n_jobs = 2 * N_EXPERTS
    jobs_per_step = -(-n_jobs // n_steps)

    def pad_job(j, wait):
        e = jnp.minimum(j, N_EXPERTS - 1)
        tail_lo = pend_ref[N_EXPERTS - 1] + (j - N_EXPERTS) * SLOT_BLOCK
        lo = jnp.where(j < N_EXPERTS, pstart_ref[e] + cnt_ref[e], tail_lo)
        hi = jnp.where(j < N_EXPERTS, pend_ref[e], jnp.minimum(tail_lo + SLOT_BLOCK, n_slots))

        def body(r, c):
            cp = pltpu.make_async_copy(zero_rows, xs_hbm.at[_token_rows(r)], zsem)
            cp.wait() if wait else cp.start()
            return c
        lax.fori_loop(lo, jnp.maximum(lo, hi), body, 0)

    def pad_jobs(wait):
        for jj in range(jobs_per_step):
            j = i * jobs_per_step + jj

            @pl.when(j < n_jobs)
            def _():
                pad_job(j, wait)

    @pl.when(i == 0)
    def _():
        zero_rows[...] = jnp.zeros(zero_rows.shape, zero_rows.dtype)

    pad_jobs(False)

    def row_copy(tile_slot, r, row):
        return pltpu.make_async_copy(rows.at[tile_slot, _token_rows(r)], xs_hbm.at[_token_rows(row)], sem.at[tile_slot])

    def drain(tile_slot):
        def body(r, c):
            for _ in range(TOP_K):
                row_copy(tile_slot, r, 0).wait()
            return c
        lax.fori_loop(0, tm, body, 0, unroll=8)

    @pl.when(i >= 2)
    def _():
        drain(slot)

    rows[slot] = h_ref[...]

    def issue(r, c):
        for k in range(TOP_K):
            row_copy(slot, r, dest_ref[k * n + i * tm + r]).start(priority=k)
        return c

    lax.fori_loop(0, tm, issue, 0, unroll=8)
    pad_jobs(True)

    @pl.when(i == n_steps - 1)
    def _():
        if n_steps >= 2:
            drain(1 - slot)
        drain(slot)


def _dispatch(dest, counts, pstarts, pends, h2, n_slots):
    n = h2.shape[0] // TILE_ROWS
    tm = min(512, n)
    return pl.pallas_call(
        functools.partial(_dispatch_kernel, n_steps=n // tm),
        grid_spec=pltpu.PrefetchScalarGridSpec(
            num_scalar_prefetch=4,
            grid=(n // tm,),
            in_specs=[pl.BlockSpec((tm * TILE_ROWS, 128), lambda i, *_: (i, 0))],
            out_specs=pl.BlockSpec(memory_space=pl.ANY),
            scratch_shapes=[pltpu.VMEM((2, tm * TILE_ROWS, 128), h2.dtype), pltpu.VMEM((TILE_ROWS, 128), h2.dtype),
                            pltpu.SemaphoreType.DMA((2,)), pltpu.SemaphoreType.DMA]),
        out_shape=jax.ShapeDtypeStruct((n_slots * TILE_ROWS, 128), h2.dtype),
        compiler_params=_cparams(1),
        name="slot_dispatch",
    )(dest, counts, pstarts, pends, h2)


def _expert_kernel(be_ref, first_ref, slot_ref, nxt_ref, nused_ref, x_ref, wg_hbm, wu_hbm, wd_hbm, o_ref,
                   wg_buf, wu_buf, wd_buf, sem):
    i = pl.program_id(0)

    def weight_copies(e, s):
        return (pltpu.make_async_copy(wg_hbm.at[e], wg_buf.at[s], sem.at[s, 0]),
                pltpu.make_async_copy(wu_hbm.at[e], wu_buf.at[s], sem.at[s, 1]),
                pltpu.make_async_copy(wd_hbm.at[e], wd_buf.at[s], sem.at[s, 2]))

    @pl.when(i == 0)
    def _():
        for cp in weight_copies(be_ref[0], 0):
            cp.start()

    s = slot_ref[i]

    @pl.when(first_ref[i] == 1)
    def _():
        for cp in weight_copies(be_ref[i], s):
            cp.wait()

        @pl.when(nxt_ref[i] >= 0)
        def _():
            for cp in weight_copies(nxt_ref[i], 1 - s):
                cp.start()

    used = i < nused_ref[0]

    @pl.when(used)
    def _():
        x_lo, x_hi = _unpack_halves(_load_token_tiles(x_ref))
        g = _dot(x_lo, wg_buf[s, :HALF]) + _dot(x_hi, wg_buf[s, HALF:])
        u = _dot(x_lo, wu_buf[s, :HALF]) + _dot(x_hi, wu_buf[s, HALF:])
        mid = (jax.nn.silu(g) * u).astype(_BF16)
        _store_token_tiles(o_ref, _pack_halves(_dot(mid, wd_buf[s])))

    @pl.when(jnp.logical_not(used))
    def _():
        o_ref[...] = jnp.zeros(o_ref.shape, o_ref.dtype)


def _experts(blk_e, n_used, xs, w_gate, w_up, w_down):
    p = xs.shape[0] // TILE_ROWS
    d = w_gate.shape[1]
    nb = p // SLOT_BLOCK
    first = jnp.concatenate([jnp.ones((1,), jnp.int32), (blk_e[1:] != blk_e[:-1]).astype(jnp.int32)])
    slot = (jnp.cumsum(first) - 1) % 2
    later = blk_e[None, :] > blk_e[:, None]
    nxt = jnp.min(jnp.where(later, blk_e[None, :], N_EXPERTS), axis=1)
    nxt = jnp.where(nxt < N_EXPERTS, nxt, -1).astype(jnp.int32)
    any_spec = pl.BlockSpec(memory_space=pl.ANY)
    return pl.pallas_call(
        _expert_kernel,
        grid_spec=pltpu.PrefetchScalarGridSpec(
            num_scalar_prefetch=5,
            grid=(nb,),
            in_specs=[pl.BlockSpec((SLOT_BLOCK * TILE_ROWS, 128), lambda i, *_: (i, 0)), any_spec, any_spec, any_spec],
            out_specs=pl.BlockSpec((SLOT_BLOCK * TILE_ROWS, 128), lambda i, *_: (i, 0)),
            scratch_shapes=[pltpu.VMEM((2, d, D_EXPERT), _F32),
                            pltpu.VMEM((2, d, D_EXPERT), _F32),
                            pltpu.VMEM((2, D_EXPERT, d), _F32),
                            pltpu.SemaphoreType.DMA((2, 3))]),
        out_shape=jax.ShapeDtypeStruct((p * TILE_ROWS, 128), jnp.uint32),
        compiler_params=pltpu.CompilerParams(dimension_semantics=("arbitrary",),
                                             vmem_limit_bytes=EXPERT_VMEM_LIMIT),
        name="expert_mlp",
    )(blk_e, first, slot.astype(jnp.int32), nxt, n_used, xs, w_gate, w_up, w_down)


def _combine_kernel(dest_ref, x2_ref, w0_ref, w1_ref, g_ref, ys_hbm, o_ref, rows, sem):
    tm = x2_ref.shape[0]
    n = pl.num_programs(0) * tm
    i = pl.program_id(0)

    def issue(tile, slot):
        def body(r, c):
            for k in range(TOP_K):
                _row_copy(ys_hbm, dest_ref[k * n + tile * tm + r], rows.at[slot, k], r, sem.at[slot]).start(priority=k)
            return c
        lax.fori_loop(0, tm, body, 0, unroll=8)

    @pl.when(i == 0)
    def _():
        issue(0, 0)

    @pl.when(i + 1 < pl.num_programs(0))
    def _():
        issue(i + 1, (i + 1) % 2)

    slot = i % 2

    def drain(r, c):
        for k in range(TOP_K):
            _row_copy(ys_hbm, 0, rows.at[slot, k], r, sem.at[slot]).wait()
        return c

    lax.fori_loop(0, tm, drain, 0, unroll=8)
    y0 = _unpack_halves(_load_token_tiles(rows.at[slot, 0]))
    y1 = _unpack_halves(_load_token_tiles(rows.at[slot, 1]))
    w0 = w0_ref[...]
    w1 = w1_ref[...]
    halves = (slice(0, HALF), slice(HALF, 2 * HALF))
    x3 = [x2_ref[:, cols] + (w0 * a + w1 * b) for cols, a, b in zip(halves, y0, y1)]
    ms = (jnp.sum(x3[0] * x3[0], axis=-1, keepdims=True) + jnp.sum(x3[1] * x3[1], axis=-1, keepdims=True)) / (2 * HALF)
    inv = lax.rsqrt(ms + NORM_EPS)
    for cols, v in zip(halves, x3):
        o_ref[:, cols] = (v * inv) * g_ref[:, cols]


def _combine(dest, x2, w0, w1, g_final, ys):
    n, d = x2.shape
    tm = min(512, n)
    return pl.pallas_call(
        _combine_kernel,
        grid_spec=pltpu.PrefetchScalarGridSpec(
            num_scalar_prefetch=1,
            grid=(n // tm,),
            in_specs=[pl.BlockSpec((tm, d), lambda i, dst: (i, 0)),
                      pl.BlockSpec((tm, 1), lambda i, dst: (i, 0)),
                      pl.BlockSpec((tm, 1), lambda i, dst: (i, 0)),
                      pl.BlockSpec((1, d), lambda i, dst: (0, 0)),
                      pl.BlockSpec(memory_space=pl.ANY)],
            out_specs=pl.BlockSpec((tm, d), lambda i, dst: (i, 0)),
            scratch_shapes=[pltpu.VMEM((2, TOP_K, tm * TILE_ROWS, 128), jnp.uint32), pltpu.SemaphoreType.DMA((2,))]),
        out_shape=jax.ShapeDtypeStruct((n, d), _F32),
        compiler_params=_cparams(1),
        name="combine_final_norm",
    )(dest, x2, w0, w1, g_final, ys)


def kernel(x, positions, g_mix, w_in, conv_w, conv_b, w_rg_a, b_rg_a, w_rg_x, b_rg_x, lru_param, lambda_q1, lambda_k1, lambda_q2, lambda_k2, subln_g, w_br_rnn, w_br_attn, w_out, g_ffn, w_grp_router, w_exp_router, w_gate, w_up, w_down, g_final):
    batch, seq, d = x.shape
    n = batch * seq
    xf = x.reshape(n, d)
    pos_col = positions.reshape(n, 1)
    row = lambda v: v.reshape(1, -1)

    w_rg = (0.5 * jnp.concatenate([w_rg_a[0], w_rg_x[0]], axis=-1)).astype(_BF16)
    w_router = jnp.concatenate(
        [w_grp_router[0], jnp.zeros((d, 8 - N_GROUPS), _F32), w_exp_router[0],
         jnp.zeros((d, 128 - ROUTER_ROWS), _F32)], axis=1)

    h = _rmsnorm(xf, row(g_mix[0]), _BF16)
    cos_t, sin_t = _rope_tables(pos_col)

    c_qk = 2 * D_RNN
    c_v = c_qk + 2 * QK_W
    c_g = c_v + ATTN_W
    u = _projection(_proj_plain_kernel, h, w_in[0], 0, D_RNN, _F32, name="proj_rnn")
    gg = _projection(_proj_gelu_kernel, h, w_in[0], D_RNN, D_RNN, _F32, name="proj_rnn_gate")
    q_scale = (HEAD_DIM ** -0.5) * LOG2E
    qt = _projection(functools.partial(_proj_rope_kernel, scale=q_scale, transpose_out=True),
                     h, w_in[0], c_qk, QK_W, _BF16, extra=(cos_t, sin_t), transpose_out=True, name="proj_q_rope")
    k = _projection(functools.partial(_proj_rope_kernel, scale=1.0, transpose_out=False),
                    h, w_in[0], c_qk + QK_W, QK_W, _BF16, extra=(cos_t, sin_t), name="proj_k_rope")
    vt = _projection(_proj_t_kernel, h, w_in[0], c_v, ATTN_W, _BF16, transpose_out=True, name="proj_v")
    gates = _projection(_proj_sigmoid_kernel, h, w_in[0], c_g, 2 * D_MODEL, _BF16, name="proj_gates")

    y_rnn = _lru(pos_col, u, gg, conv_w[0], row(conv_b[0]), w_rg, row(0.5 * b_rg_a[0]), row(0.5 * b_rg_x[0]),
                 row(lru_param[0]), batch, seq)
    y_attn = _attention(qt, k, vt, row(lambda_q1[0]), row(lambda_k1[0]), row(lambda_q2[0]), row(lambda_k2[0]),
                        row(subln_g[0]), batch, seq)
    merged = _merge(y_rnn, y_attn, w_br_rnn[0], w_br_attn[0], gates)
    x2, h2, logits_t = _outproj(merged, w_out[0], xf, row(g_ffn[0]), w_router)

    eid8, wts8, rank8, cnt = _route(logits_t)
    eid, rank, counts = eid8[:TOP_K], rank8[:TOP_K], cnt[:, 0]

    pcounts = ((counts + SLOT_BLOCK - 1) // SLOT_BLOCK) * SLOT_BLOCK
    pends = jnp.cumsum(pcounts)
    pstarts = pends - pcounts
    experts = jnp.arange(N_EXPERTS, dtype=jnp.int32)
    dest = jnp.sum(jnp.where(eid[:, :, None] == experts, pstarts, 0), axis=-1) + rank
    n_slots = n * TOP_K + N_EXPERTS * SLOT_BLOCK
    blk_start = jnp.arange(n_slots // SLOT_BLOCK, dtype=jnp.int32) * SLOT_BLOCK
    blk_e = jnp.minimum(jnp.sum((blk_start[:, None] >= pends[None, :]).astype(jnp.int32), axis=1), N_EXPERTS - 1)

    dest_flat = dest.reshape(-1)
    xs = _dispatch(dest_flat, counts, pstarts, pends, h2, n_slots)
    n_used = (pends[N_EXPERTS - 1:] // SLOT_BLOCK).astype(jnp.int32)
    ys = _experts(blk_e, n_used, xs, w_gate[0], w_up[0], w_down[0])
    out = _combine(dest_flat, x2, wts8[0].reshape(n, 1), wts8[1].reshape(n, 1), row(g_final), ys)
    return out.reshape(batch, seq, d)
```
